```python
import jax, jax.numpy as jnp
from jax import lax
import numpy as np

D_MODEL = 1024
BATCH = 8
SEQ = 2048
DEPTH = 4
DEC_BATCH = 128
DEC_SEQ = 4
PAST_LEN = 2048
PAGE_SIZE = 128

HEAD_DIM = 64
N_HEADS_TOTAL = D_MODEL // HEAD_DIM
RET_HEADS = N_HEADS_TOTAL // 4
MLSTM_HEADS = N_HEADS_TOTAL // 4
FOX_HEADS = N_HEADS_TOTAL - RET_HEADS - MLSTM_HEADS
RET_W = RET_HEADS * HEAD_DIM
MLSTM_W = MLSTM_HEADS * HEAD_DIM
FOX_W = FOX_HEADS * HEAD_DIM
MIX_W = RET_W + MLSTM_W + FOX_W
MEM_LEN = 256
MEM_HEADS = 4
MEM_W = MEM_HEADS * HEAD_DIM
D_FF = 4 * D_MODEL
CHUNK = 128
Q_BLOCK = 128
ROPE_BASE = 10000.0
EPS = 1e-6
IN_SIZES = (RET_W, RET_W, RET_W, RET_W,
            MLSTM_W, MLSTM_W, MLSTM_W, MLSTM_W, MLSTM_HEADS, MLSTM_HEADS,
            FOX_W, FOX_W, FOX_W, FOX_HEADS)
IN_W = sum(IN_SIZES)

kernel_name = 'hybrid_ret_mlstm_fox_headgroup_decoder_step'

F32 = jnp.float32


def _split_points():
    return [int(v) for v in np.cumsum(IN_SIZES)[:-1]]


def rmsnorm(x, g):
    xf = x.astype(F32)
    y = xf * lax.rsqrt(jnp.mean(xf * xf, axis=-1, keepdims=True) + EPS)
    return (y * g.astype(F32)).astype(x.dtype)


def head_rmsnorm(x, g):
    B, T = x.shape[:2]
    y = x * lax.rsqrt(jnp.mean(x * x, axis=-1, keepdims=True) + EPS)
    return y.reshape(B, T, -1) * g.astype(F32)


def rotary(x, pos):
    half = HEAD_DIM // 2
    inv = ROPE_BASE ** (-jnp.arange(half, dtype=F32) / half)
    ang = pos.astype(F32)[:, None] * inv[None, :]
    cos = jnp.cos(ang)[None, :, None, :]
    sin = jnp.sin(ang)[None, :, None, :]
    x1, x2 = x[..., :half].astype(F32), x[..., half:].astype(F32)
    return jnp.concatenate([x1 * cos - x2 * sin, x1 * sin + x2 * cos], axis=-1)


def retention_log_decay():
    return jnp.log1p(-jnp.exp2(-5.0 - jnp.arange(RET_HEADS, dtype=F32)))


def chunked_scan(step, state, xs):
    B, T = xs[0].shape[:2]
    nc = T // CHUNK
    split = lambda a: jnp.moveaxis(a.reshape((B, nc, CHUNK) + a.shape[2:]), 1, 0)
    state, ys = lax.scan(lambda st, xc: step(st, *xc), state, tuple(split(a) for a in xs))
    ys = jnp.moveaxis(ys, 0, 1).reshape((B, T) + ys.shape[3:])
    return state, ys


def retention_chunk(S, q, k, v, log_gamma):
    L = q.shape[1]
    idx = jnp.arange(L, dtype=F32)
    diff = idx[:, None] - idx[None, :]
    causal = diff >= 0
    decay = jnp.where(causal[None], jnp.exp(log_gamma[:, None, None] * jnp.where(causal, diff, 0.0)[None]), 0.0)
    scores = jnp.einsum('bthd,bshd->bhts', q, k) * decay[None]
    inner = jnp.einsum('bhts,bshd->bthd', scores, v)
    q_decay = jnp.exp(log_gamma[None, :] * (idx[:, None] + 1.0))
    cross = jnp.einsum('bthd,bhde->bthe', q, S) * q_decay[None, :, :, None]
    k_decay = jnp.exp(log_gamma[None, :] * (L - 1.0 - idx[:, None]))
    S_new = jnp.exp(log_gamma * L)[None, :, None, None] * S + jnp.einsum('bshd,bshe->bhde', k * k_decay[None, :, :, None], v)
    return S_new, inner + cross


def mlstm_chunk(state, q, k, v, i_pre, logf):
    C, n, m = state
    L = q.shape[1]
    F = jnp.cumsum(logf, axis=1)
    causal = jnp.tril(jnp.ones((L, L), dtype=bool))
    logD = F[:, :, None, :] - F[:, None, :, :] + i_pre[:, None, :, :]
    logD = jnp.where(causal[None, :, :, None], logD, -jnp.inf)
    inter = F + m[:, None, :]
    m_t = jnp.maximum(inter, jnp.max(logD, axis=2))
    D = jnp.exp(logD - m_t[:, :, None, :])
    w_inter = jnp.exp(inter - m_t)
    qk = jnp.einsum('bthd,bshd->btsh', q, k) * D
    num = jnp.einsum('btsh,bshd->bthd', qk, v) + w_inter[..., None] * jnp.einsum('bhvk,bthk->bthv', C, q)
    den = jnp.sum(qk, axis=2) + w_inter * jnp.einsum('bhk,bthk->bth', n, q)
    h = num / jnp.maximum(jnp.abs(den), jnp.exp(-m_t))[..., None]
    m_L = m_t[:, -1]
    w_s = jnp.exp(F[:, -1:, :] - F + i_pre - m_L[:, None, :])
    w_C = jnp.exp(F[:, -1] + m - m_L)
    C_new = w_C[..., None, None] * C + jnp.einsum('bsh,bshv,bshk->bhvk', w_s, v, k)
    n_new = w_C[..., None] * n + jnp.einsum('bsh,bshk->bhk', w_s, k)
    return (C_new, n_new, m_L), h


def fox_prompt(q, k, v, logf):
    B, S = q.shape[:2]
    scale = HEAD_DIM ** -0.5
    FT = jnp.cumsum(logf, axis=1).transpose(0, 2, 1)
    kpos = jnp.arange(S)

    def block(i):
        start = i * Q_BLOCK
        qb = lax.dynamic_slice_in_dim(q, start, Q_BLOCK, axis=1)
        Fq = lax.dynamic_slice_in_dim(FT, start, Q_BLOCK, axis=2)
        logits = jnp.einsum('bthd,bshd->bhts', qb, k, preferred_element_type=F32) * scale + (Fq[..., :, None] - FT[..., None, :])
        qpos = start + jnp.arange(Q_BLOCK)
        logits = jnp.where(kpos[None, :] <= qpos[:, None], logits, -jnp.inf)
        p = jax.nn.softmax(logits, axis=-1)
        return jnp.einsum('bhts,bshd->bthd', p.astype(v.dtype), v)

    out = lax.map(block, jnp.arange(S // Q_BLOCK))
    return jnp.moveaxis(out, 0, 1).reshape(B, S, FOX_HEADS, HEAD_DIM)


def fox_decode(q, k_all, v_all, logf_all):
    T = q.shape[1]
    N = k_all.shape[1]
    P = N - T
    scale = HEAD_DIM ** -0.5
    FT = jnp.cumsum(logf_all, axis=1).transpose(0, 2, 1)
    logits = jnp.einsum('bthd,bshd->bhts', q, k_all, preferred_element_type=F32) * scale + (FT[..., P:, None] - FT[..., None, :])
    mask = jnp.arange(N)[None, :] <= (P + jnp.arange(T))[:, None]
    p = jax.nn.softmax(jnp.where(mask, logits, -jnp.inf), axis=-1)
    return jnp.einsum('bhts,bshd->bthd', p.astype(v_all.dtype), v_all)


def gather_pages(pool, page_table):
    g = pool[page_table]
    return g.reshape((page_table.shape[0], -1) + pool.shape[2:])


def project_groups(h, w_in, b_in, pos):
    B, T, _ = h.shape
    z = h @ w_in + b_in
    (rq, rk, rv, rg, mq, mk, mv, mo, mi, mf, fq, fk, fv, ff) = jnp.split(z, _split_points(), axis=-1)
    heads = lambda a: a.reshape(B, T, -1, HEAD_DIM)
    scale = HEAD_DIM ** -0.5
    ret = (rotary(heads(rq), pos), rotary(heads(rk), pos) * scale, heads(rv).astype(F32), jax.nn.silu(rg.astype(F32)))
    mls = (heads(mq).astype(F32), heads(mk).astype(F32) * scale, heads(mv).astype(F32),
           jax.nn.sigmoid(heads(mo).astype(F32)), mi.astype(F32), jax.nn.log_sigmoid(mf.astype(F32)))
    fox = (heads(fq), heads(fk), heads(fv), jax.nn.log_sigmoid(ff.astype(F32)))
    return ret, mls, fox


def merge_groups(ret_o, ret_gate, g_ret, mls_h, mls_o, g_mlstm, fox_o, w_out, dtype):
    B, T = ret_o.shape[:2]
    r = head_rmsnorm(ret_o, g_ret) * ret_gate
    m = head_rmsnorm(mls_o * mls_h, g_mlstm)
    f = fox_o.reshape(B, T, FOX_W).astype(F32)
    return jnp.concatenate([r, m, f], axis=-1).astype(dtype) @ w_out


def mixer_prompt(h, w_in, b_in, g_ret, g_mlstm, w_out, pos, log_gamma):
    B = h.shape[0]
    ret, mls, fox = project_groups(h, w_in, b_in, pos)
    S0 = jnp.zeros((B, RET_HEADS, HEAD_DIM, HEAD_DIM), F32)
    S_fin, ret_o = chunked_scan(lambda st, q, k, v: retention_chunk(st, q, k, v, log_gamma), S0, ret[:3])
    st0 = (jnp.zeros((B, MLSTM_HEADS, HEAD_DIM, HEAD_DIM), F32), jnp.zeros((B, MLSTM_HEADS, HEAD_DIM), F32),
           jnp.zeros((B, MLSTM_HEADS), F32))
    (C, n, m), mls_h = chunked_scan(mlstm_chunk, st0, (mls[0], mls[1], mls[2], mls[4], mls[5]))
    fox_o = fox_prompt(*fox)
    y = merge_groups(ret_o, ret[3], g_ret, mls_h, mls[3], g_mlstm, fox_o, w_out, h.dtype)
    return y, (S_fin, C, n, m, fox[1], fox[2], fox[3])


def mixer_sample(h, w_in, b_in, g_ret, g_mlstm, w_out, pos, log_gamma, S, C, n, m, past_k, past_v, past_logf):
    ret, mls, fox = project_groups(h, w_in, b_in, pos)
    S_new, ret_o = retention_chunk(S.astype(F32), ret[0], ret[1], ret[2], log_gamma)
    (C1, n1, m1), mls_h = mlstm_chunk((C.astype(F32), n.astype(F32), m.astype(F32)), mls[0], mls[1], mls[2], mls[4], mls[5])
    k_all = jnp.concatenate([past_k, fox[1].astype(past_k.dtype)], axis=1)
    v_all = jnp.concatenate([past_v, fox[2].astype(past_v.dtype)], axis=1)
    lf_all = jnp.concatenate([past_logf.astype(F32), fox[3]], axis=1)
    fox_o = fox_decode(fox[0], k_all, v_all, lf_all)
    y = merge_groups(ret_o, ret[3], g_ret, mls_h, mls[3], g_mlstm, fox_o, w_out, h.dtype)
    return y, (S_new, C1, n1, m1, fox[1], fox[2], fox[3])


def memory_kv(mem, g_mem, w_ckv):
    B, M, _ = mem.shape
    k, v = jnp.split(rmsnorm(mem, g_mem) @ w_ckv, 2, axis=-1)
    return k.reshape(B, M, MEM_HEADS, HEAD_DIM), v.reshape(B, M, MEM_HEADS, HEAD_DIM)


def cross_attn(h, mk, mv, w_q, w_o):
    B, T, _ = h.shape
    q = (h @ w_q).reshape(B, T, MEM_HEADS, HEAD_DIM)
    logits = jnp.einsum('bthd,bmhd->bhtm', q, mk, preferred_element_type=F32) * (HEAD_DIM ** -0.5)
    p = jax.nn.softmax(logits, axis=-1)
    o = jnp.einsum('bhtm,bmhd->bthd', p.astype(mv.dtype), mv).reshape(B, T, MEM_W)
    return o.astype(h.dtype) @ w_o


def sq_relu_mlp(h, w_up, w_down):
    return jnp.square(jax.nn.relu(h @ w_up)) @ w_down


def setup_inputs(seed: int = 0) -> dict:
    key = jax.random.key(seed)
    k = jax.random.split(key, 32)

    def nrm(i, shape, s=1.0):
        return s * jax.random.normal(k[i], shape, F32)

    n_pages = PAST_LEN // PAGE_SIZE
    n_used = DEC_BATCH * n_pages
    n_pool = n_used + max(1, n_used // 4)
    page_table = jax.random.permutation(k[12], n_pool)[:n_used].reshape(DEC_BATCH, n_pages).astype(jnp.int32)

    off = [np.zeros(s, np.float32) for s in IN_SIZES]
    off[9] = np.linspace(3.0, 6.0, MLSTM_HEADS, dtype=np.float32)
    off[13] = np.linspace(2.0, 6.0, FOX_HEADS, dtype=np.float32)
    b_in = nrm(15, (DEPTH, IN_W), 0.02) + jnp.asarray(np.concatenate(off))[None, :]
    fox_fbias = jnp.asarray(off[13])
    gain = lambda i, shape: 1.0 + nrm(i, shape, 0.02)

    return {
        'x_prompt': nrm(0, (BATCH, SEQ, D_MODEL)),
        'x_sample': nrm(1, (DEC_BATCH, DEC_SEQ, D_MODEL)),
        'mem_prompt': nrm(2, (BATCH, MEM_LEN, D_MODEL)),
        'state_ret': nrm(3, (DEPTH, DEC_BATCH, RET_HEADS, HEAD_DIM, HEAD_DIM), 0.5),
        'state_mlstm_c': nrm(4, (DEPTH, DEC_BATCH, MLSTM_HEADS, HEAD_DIM, HEAD_DIM), 0.5),
        'state_mlstm_n': nrm(5, (DEPTH, DEC_BATCH, MLSTM_HEADS, HEAD_DIM), 0.5),
        'state_mlstm_m': 1.0 + nrm(6, (DEPTH, DEC_BATCH, MLSTM_HEADS)),
        'cache_fox_k': nrm(7, (DEPTH, n_pool, PAGE_SIZE, FOX_HEADS, HEAD_DIM)),
        'cache_fox_v': nrm(8, (DEPTH, n_pool, PAGE_SIZE, FOX_HEADS, HEAD_DIM)),
        'cache_fox_logf': jax.nn.log_sigmoid(fox_fbias + nrm(9, (DEPTH, n_pool, PAGE_SIZE, FOX_HEADS))),
        'cache_mem_k': nrm(10, (DEPTH, DEC_BATCH, MEM_LEN, MEM_HEADS, HEAD_DIM)),
        'cache_mem_v': nrm(11, (DEPTH, DEC_BATCH, MEM_LEN, MEM_HEADS, HEAD_DIM)),
        'page_table': page_table,
        'g_mix': gain(13, (DEPTH, D_MODEL)),
        'w_in': nrm(14, (DEPTH, D_MODEL, IN_W), D_MODEL ** -0.5),
        'b_in': b_in,
        'g_ret': gain(16, (DEPTH, RET_W)),
        'g_mlstm': gain(17, (DEPTH, MLSTM_W)),
        'w_out': nrm(18, (DEPTH, MIX_W, D_MODEL), MIX_W ** -0.5),
        'g_cross': gain(19, (DEPTH, D_MODEL)),
        'g_mem': gain(20, (DEPTH, D_MODEL)),
        'w_cq': nrm(21, (DEPTH, D_MODEL, MEM_W), D_MODEL ** -0.5),
        'w_ckv': nrm(22, (DEPTH, D_MODEL, 2 * MEM_W), D_MODEL ** -0.5),
        'w_co': nrm(23, (DEPTH, MEM_W, D_MODEL), MEM_W ** -0.5),
        'g_mlp': gain(24, (DEPTH, D_MODEL)),
        'w_up': nrm(25, (DEPTH, D_MODEL, D_FF), D_MODEL ** -0.5),
        'w_down': nrm(26, (DEPTH, D_FF, D_MODEL), D_FF ** -0.5),
        'g_final': gain(27, (D_MODEL,)),
    }


def reference(x_prompt, x_sample, mem_prompt, state_ret, state_mlstm_c, state_mlstm_n, state_mlstm_m,
              cache_fox_k, cache_fox_v, cache_fox_logf, cache_mem_k, cache_mem_v, page_table,
              g_mix, w_in, b_in, g_ret, g_mlstm, w_out, g_cross, g_mem, w_cq, w_ckv, w_co,
              g_mlp, w_up, w_down, g_final):
    S = x_prompt.shape[1]
    T = x_sample.shape[1]
    past = page_table.shape[1] * cache_fox_k.shape[2]
    pos_p = jnp.arange(S, dtype=jnp.int32)
    pos_s = past + jnp.arange(T, dtype=jnp.int32)
    log_gamma = retention_log_decay()
    xp, xs = x_prompt, x_sample
    new_p = [[] for _ in range(9)]
    new_s = [[] for _ in range(7)]
    for l in range(DEPTH):
        yp, st_p = mixer_prompt(rmsnorm(xp, g_mix[l]), w_in[l], b_in[l], g_ret[l], g_mlstm[l], w_out[l], pos_p, log_gamma)
        ys, st_s = mixer_sample(rmsnorm(xs, g_mix[l]), w_in[l], b_in[l], g_ret[l], g_mlstm[l], w_out[l], pos_s, log_gamma,
                                state_ret[l], state_mlstm_c[l], state_mlstm_n[l], state_mlstm_m[l],
                                gather_pages(cache_fox_k[l], page_table), gather_pages(cache_fox_v[l], page_table),
                                gather_pages(cache_fox_logf[l], page_table))
        xp = xp + yp
        xs = xs + ys
        mk_p, mv_p = memory_kv(mem_prompt, g_mem[l], w_ckv[l])
        xp = xp + cross_attn(rmsnorm(xp, g_cross[l]), mk_p, mv_p, w_cq[l], w_co[l])
        xs = xs + cross_attn(rmsnorm(xs, g_cross[l]), cache_mem_k[l], cache_mem_v[l], w_cq[l], w_co[l])
        xp = xp + sq_relu_mlp(rmsnorm(xp, g_mlp[l]), w_up[l], w_down[l])
        xs = xs + sq_relu_mlp(rmsnorm(xs, g_mlp[l]), w_up[l], w_down[l])
        for lst, a in zip(new_p, st_p + (mk_p, mv_p)):
            lst.append(a)
        for lst, a in zip(new_s, st_s):
            lst.append(a)
    y_prompt = rmsnorm(xp, g_final)
    y_sample = rmsnorm(xs, g_final)
    ret_p, c_p, n_p, m_p, fk_p, fv_p, flf_p, memk_p, memv_p = [jnp.stack(a, axis=0) for a in new_p]
    ret_s, c_s, n_s, m_s, fk_s, fv_s, flf_s = [jnp.stack(a, axis=0) for a in new_s]
    return (y_prompt, y_sample, ret_p, ret_s, c_p, c_s, n_p, n_s, m_p, m_s,
            fk_p, fk_s, fv_p, fv_s, flf_p, flf_s, memk_p, memv_p)
```

```python
import functools
import math

import numpy as np
import jax
import jax.numpy as jnp
from jax import lax
from jax.experimental import pallas as pl
from jax.experimental.pallas import tpu as pltpu

F32 = jnp.float32
BF16 = jnp.bfloat16

HEAD_DIM = 64
HALF = HEAD_DIM // 2
EPS = 1e-6
ROPE_BASE = 10000.0
CHUNK = 128
SUBLANES = 8
LANES = 128
GATE_ROWS = 16
SCALE = HEAD_DIM ** -0.5
VMEM_LIMIT = 56 * 1024 * 1024
NEG_INF = float("-inf")


def _cparams(*sem):
    return pltpu.CompilerParams(dimension_semantics=sem, vmem_limit_bytes=VMEM_LIMIT)


def _dot(a, b):
    return jnp.dot(a.astype(BF16), b.astype(BF16), preferred_element_type=F32)


def _dot_nt(a, b):
    return lax.dot_general(a.astype(BF16), b.astype(BF16), (((1,), (1,)), ((), ())),
                           preferred_element_type=F32)


def _dot_tn(a, b):
    return lax.dot_general(a.astype(BF16), b.astype(BF16), (((0,), (0,)), ((), ())),
                           preferred_element_type=F32)


def _split3(x):
    hi = x.astype(BF16)
    r1 = x - hi.astype(F32)
    mid = r1.astype(BF16)
    lo = (r1 - mid.astype(F32)).astype(BF16)
    return hi, mid, lo


def _cumsum_rows(x, n):
    t = lax.broadcasted_iota(jnp.int32, (n, n), 0)
    s = lax.broadcasted_iota(jnp.int32, (n, n), 1)
    tri = jnp.where(t >= s, 1.0, 0.0).astype(BF16)
    return sum(jnp.dot(tri, p, preferred_element_type=F32) for p in _split3(x))


def _cumsum_lanes(x, n):
    t = lax.broadcasted_iota(jnp.int32, (n, n), 0)
    s = lax.broadcasted_iota(jnp.int32, (n, n), 1)
    tri = jnp.where(t <= s, 1.0, 0.0).astype(BF16)
    return sum(jnp.dot(p, tri, preferred_element_type=F32) for p in _split3(x))


def _rms(x, g):
    return x * lax.rsqrt(jnp.mean(x * x, axis=-1, keepdims=True) + EPS) * g


def _sigmoid(x):
    return 1.0 / (1.0 + jnp.exp(-x))


def _log_sigmoid(x):
    return jnp.minimum(x, 0.0) - jnp.log1p(jnp.exp(-jnp.abs(x)))


def _head_norm(o):
    return o * lax.rsqrt(jnp.mean(o * o, axis=-1, keepdims=True) + EPS)


def _head_mask(rows, width, h):
    lane = lax.broadcasted_iota(jnp.int32, (rows, width), 1)
    return (lane // HEAD_DIM) == h


def _norm_matmul_kernel(x_ref, g_ref, w_ref, b_ref, o_ref, xn_ref):
    @pl.when(pl.program_id(1) == 0)
    def _():
        xn_ref[...] = _rms(x_ref[...], g_ref[...]).astype(BF16)

    o_ref[...] = jnp.dot(xn_ref[...], w_ref[...], preferred_element_type=F32) + b_ref[...]


def norm_matmul(x, g, w, b, *, tm, tn):
    M, D = x.shape
    N = w.shape[1]
    return pl.pallas_call(
        _norm_matmul_kernel,
        out_shape=jax.ShapeDtypeStruct((M, N), F32),
        grid=(M // tm, N // tn),
        in_specs=[pl.BlockSpec((tm, D), lambda i, j: (i, 0)),
                  pl.BlockSpec((1, D), lambda i, j: (0, 0)),
                  pl.BlockSpec((D, tn), lambda i, j: (0, j)),
                  pl.BlockSpec((1, tn), lambda i, j: (0, j))],
        out_specs=pl.BlockSpec((tm, tn), lambda i, j: (i, j)),
        scratch_shapes=[pltpu.VMEM((tm, D), BF16)],
        compiler_params=_cparams("parallel", "arbitrary"),
        name="norm_matmul",
    )(x, g.reshape(1, D), w, b.reshape(1, N))


def _gates_kernel(z_ref, o_ref, *, n_raw):
    z = z_ref[...]
    lane = lax.broadcasted_iota(jnp.int32, z.shape, 1)
    o_ref[...] = jnp.where(lane >= n_raw, _log_sigmoid(z), z)


def gate_activations(z, *, col_block, n_raw, tm):
    M = z.shape[0]
    return pl.pallas_call(
        functools.partial(_gates_kernel, n_raw=n_raw),
        out_shape=jax.ShapeDtypeStruct((M, LANES), F32),
        grid=(M // tm,),
        in_specs=[pl.BlockSpec((tm, LANES), lambda i: (i, col_block))],
        out_specs=pl.BlockSpec((tm, LANES), lambda i: (i, 0)),
        compiler_params=_cparams("parallel"),
        name="gate_activations",
    )(z)


def _out_proj_kernel(x_ref, r_ref, m_ref, f_ref, wr_ref, wm_ref, wf_ref, o_ref):
    o_ref[...] = (x_ref[...] + _dot(r_ref[...], wr_ref[...]) + _dot(m_ref[...], wm_ref[...])
                  + _dot(f_ref[...], wf_ref[...]))


def out_proj(x, r, m, f, w_out, *, tm):
    M, D = x.shape
    wr, wm, wf = r.shape[1], m.shape[1], f.shape[1]
    w_r, w_m, w_f = w_out[:wr], w_out[wr:wr + wm], w_out[wr + wm:]
    row = lambda w: pl.BlockSpec((tm, w), lambda i: (i, 0))
    full = lambda a: pl.BlockSpec(a.shape, lambda i: (0, 0))
    return pl.pallas_call(
        _out_proj_kernel,
        out_shape=jax.ShapeDtypeStruct((M, D), F32),
        grid=(M // tm,),
        in_specs=[row(D), row(wr), row(wm), row(wf), full(w_r), full(w_m), full(w_f)],
        out_specs=row(D),
        compiler_params=_cparams("parallel"),
        name="out_proj",
    )(x, r, m, f, w_r, w_m, w_f)


def _mlp_kernel(x_ref, g_ref, wu_ref, wd_ref, o_ref, xn_ref, acc_ref):
    j = pl.program_id(1)

    @pl.when(j == 0)
    def _():
        xn_ref[...] = _rms(x_ref[...], g_ref[...]).astype(BF16)
        acc_ref[...] = x_ref[...]

    h = jnp.maximum(jnp.dot(xn_ref[...], wu_ref[...], preferred_element_type=F32), 0.0)
    acc_ref[...] += jnp.dot((h * h).astype(BF16), wd_ref[...], preferred_element_type=F32)

    @pl.when(j == pl.num_programs(1) - 1)
    def _():
        o_ref[...] = acc_ref[...]


def sq_relu_mlp(x, g, w_up, w_down, *, tm, tf):
    M, D = x.shape
    FF = w_up.shape[1]
    return pl.pallas_call(
        _mlp_kernel,
        out_shape=jax.ShapeDtypeStruct((M, D), F32),
        grid=(M // tm, FF // tf),
        in_specs=[pl.BlockSpec((tm, D), lambda i, j: (i, 0)),
                  pl.BlockSpec((1, D), lambda i, j: (0, 0)),
                  pl.BlockSpec((D, tf), lambda i, j: (0, j)),
                  pl.BlockSpec((tf, D), lambda i, j: (j, 0))],
        out_specs=pl.BlockSpec((tm, D), lambda i, j: (i, 0)),
        scratch_shapes=[pltpu.VMEM((tm, D), BF16), pltpu.VMEM((tm, D), F32)],
        compiler_params=_cparams("parallel", "arbitrary"),
        name="sq_relu_mlp",
    )(x, g.reshape(1, D), w_up, w_down)


def _final_norm_kernel(x_ref, g_ref, o_ref):
    o_ref[...] = _rms(x_ref[...], g_ref[...])


def final_norm(x, g, *, tm):
    M, D = x.shape
    return pl.pallas_call(
        _final_norm_kernel,
        out_shape=jax.ShapeDtypeStruct((M, D), F32),
        grid=(M // tm,),
        in_specs=[pl.BlockSpec((tm, D), lambda i: (i, 0)), pl.BlockSpec((1, D), lambda i: (0, 0))],
        out_specs=pl.BlockSpec((tm, D), lambda i: (i, 0)),
        compiler_params=_cparams("parallel"),
        name="final_norm",
    )(x, g.reshape(1, D))


def _retention_kernel(q_ref, k_ref, v_ref, g_ref, cos_ref, sin_ref, s0_ref, gain_ref,
                      y_ref, sout_ref, s_sc, *, L, valid, log_gamma):
    c = pl.program_id(1)
    n_heads = len(log_gamma)
    W = n_heads * HEAD_DIM

    @pl.when(c == 0)
    def _():
        s_sc[...] = s0_ref[0]

    cos = cos_ref[...]
    sin = sin_ref[...]
    lane = lax.broadcasted_iota(jnp.int32, (L, W), 1)
    first_half = (lane % HEAD_DIM) < HALF

    def rope(x):
        swapped = jnp.where(first_half, pltpu.roll(x, W - HALF, 1), pltpu.roll(x, HALF, 1))
        return x * cos + swapped * sin

    q = rope(q_ref[...])
    k = rope(k_ref[...]) * SCALE
    v = v_ref[...]
    t_i = lax.broadcasted_iota(jnp.int32, (L, L), 0)
    s_i = lax.broadcasted_iota(jnp.int32, (L, L), 1)
    causal = t_i >= s_i
    diff = jnp.where(causal, (t_i - s_i).astype(F32), 0.0)
    tcol = lax.broadcasted_iota(jnp.int32, (L, 1), 0).astype(F32)
    row_valid = tcol < valid
    outs = []
    for h in range(n_heads):
        lg = log_gamma[h]
        sl = slice(h * HEAD_DIM, (h + 1) * HEAD_DIM)
        qh, kh, vh = q[:, sl], k[:, sl], v[:, sl]
        decay = jnp.where(causal, jnp.exp(lg * diff), 0.0)
        inner = _dot(_dot_nt(qh, kh) * decay, vh)
        S = s_sc[h]
        cross = _dot(qh, S) * jnp.exp(lg * (tcol + 1.0))
        k_dec = jnp.where(row_valid, kh * jnp.exp(lg * (valid - 1.0 - tcol)), 0.0)
        s_sc[h] = math.exp(lg * valid) * S + _dot_tn(k_dec, vh)
        outs.append(_head_norm(inner + cross))
    y_ref[...] = jnp.concatenate(outs, axis=-1) * gain_ref[...] * (g_ref[...] * _sigmoid(g_ref[...]))

    @pl.when(c == pl.num_programs(1) - 1)
    def _():
        sout_ref[0] = s_sc[...]


def retention(z, cos, sin, s0, gain, *, n_seq, L, valid, col0, log_gamma):
    M = z.shape[0]
    n_heads = len(log_gamma)
    W = n_heads * HEAD_DIM
    nc = M // (n_seq * L)
    cb = col0 // W
    zspec = lambda j: pl.BlockSpec((L, W), lambda b, c: (b * nc + c, cb + j))
    tab = pl.BlockSpec((L, W), lambda b, c: (c, 0))
    st = pl.BlockSpec((1, n_heads, HEAD_DIM, HEAD_DIM), lambda b, c: (b, 0, 0, 0))
    return pl.pallas_call(
        functools.partial(_retention_kernel, L=L, valid=valid, log_gamma=log_gamma),
        out_shape=(jax.ShapeDtypeStruct((M, W), F32),
                   jax.ShapeDtypeStruct((n_seq, n_heads, HEAD_DIM, HEAD_DIM), F32)),
        grid=(n_seq, nc),
        in_specs=[zspec(0), zspec(1), zspec(2), zspec(3), tab, tab, st,
                  pl.BlockSpec((1, W), lambda b, c: (0, 0))],
        out_specs=(pl.BlockSpec((L, W), lambda b, c: (b * nc + c, 0)), st),
        scratch_shapes=[pltpu.VMEM((n_heads, HEAD_DIM, HEAD_DIM), F32)],
        compiler_params=_cparams("parallel", "arbitrary"),
        name="retention",
    )(z, z, z, z, cos, sin, s0, gain.reshape(1, W))


def _mlstm_kernel(q_ref, k_ref, v_ref, og_ref, gc_ref, gr_ref, c0_ref, n0_ref, m0_ref, gain_ref,
                  y_ref, cout_ref, nout_ref, mout_ref, c_sc, n_sc, m_sc, *, L, valid, n_heads):
    c = pl.program_id(1)

    @pl.when(c == 0)
    def _():
        c_sc[...] = c0_ref[0]
        n_sc[...] = n0_ref[0]
        m_sc[...] = m0_ref[0]

    q = q_ref[...]
    k = k_ref[...] * SCALE
    v = v_ref[...]
    og = _sigmoid(og_ref[...])
    gc = gc_ref[...]
    gr = gr_ref[0]
    f_col = _cumsum_rows(gc, L)
    f_row = _cumsum_lanes(gr, L)
    t_i = lax.broadcasted_iota(jnp.int32, (L, L), 0)
    s_i = lax.broadcasted_iota(jnp.int32, (L, L), 1)
    causal = t_i >= s_i
    row_valid = lax.broadcasted_iota(jnp.int32, (L, 1), 0) < valid
    outs = []
    for h in range(n_heads):
        sl = slice(h * HEAD_DIM, (h + 1) * HEAD_DIM)
        qh, kh, vh = q[:, sl], k[:, sl], v[:, sl]
        fc = f_col[:, n_heads + h:n_heads + h + 1]
        fr = f_row[n_heads + h:n_heads + h + 1, :]
        ic = gc[:, h:h + 1]
        ir = gr[h:h + 1, :]
        m_prev = m_sc[:, h:h + 1]
        C = c_sc[h]
        n = n_sc[h:h + 1, :]
        log_d = jnp.where(causal, fc - fr + ir, NEG_INF)
        inter = fc + m_prev
        m_t = jnp.maximum(inter, jnp.max(log_d, axis=1, keepdims=True))
        d = jnp.exp(log_d - m_t)
        w_inter = jnp.exp(inter - m_t)
        qk = _dot_nt(qh, kh) * d
        num = _dot(qk, vh) + w_inter * _dot_nt(qh, C)
        den = jnp.sum(qk, axis=1, keepdims=True) + w_inter * jnp.sum(qh * n, axis=1, keepdims=True)
        hh = num / jnp.maximum(jnp.abs(den), jnp.exp(-m_t))
        outs.append(_head_norm(og[:, sl] * hh))
        m_last = m_t[valid - 1:valid, :]
        f_last = fc[valid - 1:valid, :]
        w_s = jnp.where(row_valid, jnp.exp(f_last - fc + ic - m_last), 0.0)
        w_c = jnp.exp(f_last + m_prev - m_last)
        c_sc[h] = w_c * C + _dot_tn(vh * w_s, kh)
        n_sc[h:h + 1, :] = w_c * n + jnp.sum(kh * w_s, axis=0, keepdims=True)
        m_sc[:, h:h + 1] = m_last
    y_ref[...] = jnp.concatenate(outs, axis=-1) * gain_ref[...]

    @pl.when(c == pl.num_programs(1) - 1)
    def _():
        cout_ref[0] = c_sc[...]
        nout_ref[0] = n_sc[...]
        mout_ref[0] = m_sc[...]


def mlstm(z, gc, gr, c0, n0, m0, gain, *, n_seq, L, valid, col0, n_heads):
    M = z.shape[0]
    W = n_heads * HEAD_DIM
    nc = M // (n_seq * L)
    cb = col0 // W
    zspec = lambda j: pl.BlockSpec((L, W), lambda b, c: (b * nc + c, cb + j))
    cst = pl.BlockSpec((1, n_heads, HEAD_DIM, HEAD_DIM), lambda b, c: (b, 0, 0, 0))
    nst = pl.BlockSpec((1, n_heads, HEAD_DIM), lambda b, c: (b, 0, 0))
    mst = pl.BlockSpec((1, 1, n_heads), lambda b, c: (b, 0, 0))
    return pl.pallas_call(
        functools.partial(_mlstm_kernel, L=L, valid=valid, n_heads=n_heads),
        out_shape=(jax.ShapeDtypeStruct((M, W), F32),
                   jax.ShapeDtypeStruct((n_seq, n_heads, HEAD_DIM, HEAD_DIM), F32),
                   jax.ShapeDtypeStruct((n_seq, n_heads, HEAD_DIM), F32),
                   jax.ShapeDtypeStruct((n_seq, 1, n_heads), F32)),
        grid=(n_seq, nc),
        in_specs=[zspec(0), zspec(1), zspec(2), zspec(3),
                  pl.BlockSpec((L, LANES), lambda b, c: (b * nc + c, 0)),
                  pl.BlockSpec((1, GATE_ROWS, L), lambda b, c: (b, 0, c)),
                  cst, nst, mst,
                  pl.BlockSpec((1, W), lambda b, c: (0, 0))],
        out_specs=(pl.BlockSpec((L, W), lambda b, c: (b * nc + c, 0)), cst, nst, mst),
        scratch_shapes=[pltpu.VMEM((n_heads, HEAD_DIM, HEAD_DIM), F32),
                        pltpu.VMEM((n_heads, HEAD_DIM), F32),
                        pltpu.VMEM((1, n_heads), F32)],
        compiler_params=_cparams("parallel", "arbitrary"),
        name="mlstm",
    )(z, z, z, z, gc, gr, c0, n0, m0.reshape(n_seq, 1, n_heads), gain.reshape(1, W))


def _gate_cumsum_kernel(gc_ref, gr_ref, fc_ref, fr_ref, cc_sc, cr_sc, *, L):
    @pl.when(pl.program_id(1) == 0)
    def _():
        cc_sc[...] = jnp.zeros_like(cc_sc)
        cr_sc[...] = jnp.zeros_like(cr_sc)

    fc = _cumsum_rows(gc_ref[...], L) + cc_sc[...]
    fr = _cumsum_lanes(gr_ref[0], L) + cr_sc[...]
    fc_ref[...] = fc
    fr_ref[0] = fr
    cc_sc[...] = fc[L - 1:L, :]
    cr_sc[...] = fr[:, L - 1:L]


def gate_cumsum(gc, gr, *, n_seq, L):
    M = gc.shape[0]
    nc = M // (n_seq * L)
    return pl.pallas_call(
        functools.partial(_gate_cumsum_kernel, L=L),
        out_shape=(jax.ShapeDtypeStruct(gc.shape, F32), jax.ShapeDtypeStruct(gr.shape, F32)),
        grid=(n_seq, nc),
        in_specs=[pl.BlockSpec((L, LANES), lambda b, c: (b * nc + c, 0)),
                  pl.BlockSpec((1, GATE_ROWS, L), lambda b, c: (b, 0, c))],
        out_specs=(pl.BlockSpec((L, LANES), lambda b, c: (b * nc + c, 0)),
                   pl.BlockSpec((1, GATE_ROWS, L), lambda b, c: (b, 0, c))),
        scratch_shapes=[pltpu.VMEM((1, LANES), F32), pltpu.VMEM((GATE_ROWS, 1), F32)],
        compiler_params=_cparams("parallel", "arbitrary"),
        name="gate_cumsum",
    )(gc, gr)


def _fox_flash_kernel(q_ref, k_ref, v_ref, fc_ref, fr_ref, o_ref, m_sc, l_sc, acc_sc,
                      *, tq, tk, n_heads, g0):
    qi = pl.program_id(1)
    kj = pl.program_id(2)

    @pl.when(kj == 0)
    def _():
        m_sc[...] = jnp.full_like(m_sc, NEG_INF)
        l_sc[...] = jnp.zeros_like(l_sc)
        acc_sc[...] = jnp.zeros_like(acc_sc)

    @pl.when(kj <= qi)
    def _():
        q = q_ref[...]
        k = k_ref[...]
        v = v_ref[...]
        fc = fc_ref[...]
        fr = fr_ref[0]
        row = qi * tq + lax.broadcasted_iota(jnp.int32, (tq, tk), 0)
        col = kj * tk + lax.broadcasted_iota(jnp.int32, (tq, tk), 1)
        visible = col <= row
        for h in range(n_heads):
            sl = slice(h * HEAD_DIM, (h + 1) * HEAD_DIM)
            s = (_dot_nt(q[:, sl], k[:, sl]) * SCALE
                 + (fc[:, g0 + h:g0 + h + 1] - fr[g0 + h:g0 + h + 1, :]))
            s = jnp.where(visible, s, NEG_INF)
            m_old = m_sc[:, h:h + 1]
            m_new = jnp.maximum(m_old, jnp.max(s, axis=1, keepdims=True))
            p = jnp.exp(s - m_new)
            alpha = jnp.exp(m_old - m_new)
            l_sc[:, h:h + 1] = alpha * l_sc[:, h:h + 1] + jnp.sum(p, axis=1, keepdims=True)
            acc_sc[:, sl] = alpha * acc_sc[:, sl] + _dot(p, v[:, sl])
            m_sc[:, h:h + 1] = m_new

    @pl.when(kj == pl.num_programs(2) - 1)
    def _():
        o_ref[...] = jnp.concatenate(
            [acc_sc[:, h * HEAD_DIM:(h + 1) * HEAD_DIM] / l_sc[:, h:h + 1] for h in range(n_heads)],
            axis=-1)


def fox_prompt(z, fc, fr, *, n_seq, col0, n_heads, g0, tq):
    M = z.shape[0]
    W = n_heads * HEAD_DIM
    S = M // n_seq
    nq = S // tq
    cb = col0 // W
    return pl.pallas_call(
        functools.partial(_fox_flash_kernel, tq=tq, tk=tq, n_heads=n_heads, g0=g0),
        out_shape=jax.ShapeDtypeStruct((M, W), F32),
        grid=(n_seq, nq, nq),
        in_specs=[pl.BlockSpec((tq, W), lambda b, i, j: (b * nq + i, cb)),
                  pl.BlockSpec((tq, W), lambda b, i, j: (b * nq + jnp.minimum(i, j), cb + 1)),
                  pl.BlockSpec((tq, W), lambda b, i, j: (b * nq + jnp.minimum(i, j), cb + 2)),
                  pl.BlockSpec((tq, LANES), lambda b, i, j: (b * nq + i, 0)),
                  pl.BlockSpec((1, GATE_ROWS, tq), lambda b, i, j: (b, 0, jnp.minimum(i, j)))],
        out_specs=pl.BlockSpec((tq, W), lambda b, i, j: (b * nq + i, 0)),
        scratch_shapes=[pltpu.VMEM((tq, LANES), F32), pltpu.VMEM((tq, LANES), F32),
                        pltpu.VMEM((tq, W), F32)],
        compiler_params=_cparams("parallel", "arbitrary", "arbitrary"),
        name="fox_prompt",
    )(z, z, z, fc, fr)


def _fox_decode_kernel(pt_ref, q_ref, kn_ref, vn_ref, gc_ref, *rest, valid, n_heads, g0, pages_per_step):
    k_refs = rest[:pages_per_step]
    v_refs = rest[pages_per_step:2 * pages_per_step]
    lf_refs = rest[2 * pages_per_step:3 * pages_per_step]
    o_ref, m_sc, l_sc, acc_sc, carry_sc = rest[3 * pages_per_step:]
    del pt_ref
    j = pl.program_id(1)
    T = SUBLANES
    W = n_heads * HEAD_DIM
    R = n_heads * T

    q = q_ref[...] * SCALE
    q_exp = jnp.concatenate([jnp.where(_head_mask(T, W, h), q, 0.0) for h in range(n_heads)], axis=0)
    gc = gc_ref[...]
    t_i = lax.broadcasted_iota(jnp.int32, (T, LANES), 0)
    cum_new = sum(jnp.where(t_i >= u, gc[u:u + 1, :], 0.0) for u in range(valid))
    f_new = jnp.concatenate([cum_new[:, g0 + h:g0 + h + 1] for h in range(n_heads)], axis=0)

    @pl.when(j == 0)
    def _():
        kn = kn_ref[...]
        vn = vn_ref[...]
        t_row = lax.broadcasted_iota(jnp.int32, (R, 1), 0) % T
        cols = []
        for u in range(valid):
            f_u = jnp.concatenate(
                [jnp.broadcast_to(cum_new[u:u + 1, g0 + h:g0 + h + 1], (T, 1)) for h in range(n_heads)], axis=0)
            s_u = jnp.sum(q_exp * kn[u:u + 1, :], axis=1, keepdims=True) + (f_new - f_u)
            cols.append(jnp.where(t_row >= u, s_u, NEG_INF))
        m0 = functools.reduce(jnp.maximum, cols)
        ps = [jnp.exp(cu - m0) for cu in cols]
        m_sc[...] = m0
        l_sc[...] = sum(ps)
        acc_sc[...] = sum(p * vn[u:u + 1, :] for u, p in enumerate(ps))
        carry_sc[...] = jnp.zeros_like(carry_sc)

    s_a = lax.broadcasted_iota(jnp.int32, (LANES, LANES), 0)
    s_b = lax.broadcasted_iota(jnp.int32, (LANES, LANES), 1)
    later = jnp.where(s_a > s_b, 1.0, 0.0).astype(BF16)
    for i in range(pages_per_step):
        kp = k_refs[i][...]
        vp = v_refs[i][...]
        lf = lf_refs[i][...]
        carry = carry_sc[...]
        suffix = sum(jnp.dot(p, later, preferred_element_type=F32) for p in _split3(lf)) + carry
        carry_sc[...] = carry + jnp.sum(lf, axis=1, keepdims=True)
        bias = jnp.concatenate(
            [jnp.broadcast_to(suffix[h:h + 1, :], (T, suffix.shape[1])) for h in range(n_heads)], axis=0)
        s = _dot_nt(q_exp, kp) + (bias + f_new)
        m_old = m_sc[...]
        m_new = jnp.maximum(m_old, jnp.max(s, axis=1, keepdims=True))
        p = jnp.exp(s - m_new)
        alpha = jnp.exp(m_old - m_new)
        l_sc[...] = alpha * l_sc[...] + jnp.sum(p, axis=1, keepdims=True)
        acc_sc[...] = alpha * acc_sc[...] + _dot(p, vp)
        m_sc[...] = m_new

    @pl.when(j == pl.num_programs(1) - 1)
    def _():
        o = acc_sc[...] / l_sc[...]
        o_ref[...] = sum(jnp.where(_head_mask(T, W, h), o[h * T:(h + 1) * T, :], 0.0) for h in range(n_heads))


def fox_decode(z, gc, cache_k, cache_v, cache_lf_t, page_table, *, layer, valid, col0, n_heads, g0,
               pages_per_step):
    M = z.shape[0]
    T = SUBLANES
    n_seq = M // T
    W = n_heads * HEAD_DIM
    n_pages = page_table.shape[1]
    page = cache_k.shape[2]
    steps = n_pages // pages_per_step
    cb = col0 // W

    def page_idx(i):
        return lambda b, j, pt: (layer, pt[b * n_pages + (n_pages - 1 - (j * pages_per_step + i))], 0, 0)

    kv_specs = [pl.BlockSpec((None, None, page, W), page_idx(i)) for i in range(pages_per_step)]
    lf_specs = [pl.BlockSpec((None, None, n_heads, page), page_idx(i)) for i in range(pages_per_step)]
    zspec = lambda c: pl.BlockSpec((T, W), lambda b, j, pt: (b, c))
    R = n_heads * T
    grid_spec = pltpu.PrefetchScalarGridSpec(
        num_scalar_prefetch=1,
        grid=(n_seq, steps),
        in_specs=[zspec(cb), zspec(cb + 1), zspec(cb + 2),
                  pl.BlockSpec((T, LANES), lambda b, j, pt: (b, 0))] + kv_specs + kv_specs + lf_specs,
        out_specs=pl.BlockSpec((T, W), lambda b, j, pt: (b, 0)),
        scratch_shapes=[pltpu.VMEM((R, 1), F32), pltpu.VMEM((R, 1), F32), pltpu.VMEM((R, W), F32),
                        pltpu.VMEM((n_heads, 1), F32)],
    )
    return pl.pallas_call(
        functools.partial(_fox_decode_kernel, valid=valid, n_heads=n_heads, g0=g0,
                          pages_per_step=pages_per_step),
        out_shape=jax.ShapeDtypeStruct((M, W), F32),
        grid_spec=grid_spec,
        compiler_params=_cparams("parallel", "arbitrary"),
        name="fox_decode",
    )(page_table.reshape(-1), z, z, z, gc,
      *([cache_k] * pages_per_step), *([cache_v] * pages_per_step), *([cache_lf_t] * pages_per_step))


def _cross_prompt_kernel(x_ref, g_ref, wq_ref, mk_ref, mv_ref, wo_ref, o_ref, *, n_heads):
    x = x_ref[...]
    q = jnp.dot(_rms(x, g_ref[...]).astype(BF16), wq_ref[...], preferred_element_type=F32)
    mk = mk_ref[...]
    mv = mv_ref[...]
    outs = []
    for h in range(n_heads):
        sl = slice(h * HEAD_DIM, (h + 1) * HEAD_DIM)
        s = _dot_nt(q[:, sl], mk[:, sl]) * SCALE
        p = jnp.exp(s - jnp.max(s, axis=1, keepdims=True))
        p = p / jnp.sum(p, axis=1, keepdims=True)
        outs.append(_dot(p, mv[:, sl]))
    o_ref[...] = x + jnp.dot(jnp.concatenate(outs, axis=-1).astype(BF16), wo_ref[...],
                             preferred_element_type=F32)


def cross_attn_prompt(x, g, w_q, mkv, w_o, *, n_seq, n_heads, tm):
    M, D = x.shape
    W = n_heads * HEAD_DIM
    mem_len = mkv.shape[0] // n_seq
    per_seq = M // n_seq // tm
    return pl.pallas_call(
        functools.partial(_cross_prompt_kernel, n_heads=n_heads),
        out_shape=jax.ShapeDtypeStruct((M, D), F32),
        grid=(M // tm,),
        in_specs=[pl.BlockSpec((tm, D), lambda i: (i, 0)),
                  pl.BlockSpec((1, D), lambda i: (0, 0)),
                  pl.BlockSpec((D, W), lambda i: (0, 0)),
                  pl.BlockSpec((mem_len, W), lambda i: (i // per_seq, 0)),
                  pl.BlockSpec((mem_len, W), lambda i: (i // per_seq, 1)),
                  pl.BlockSpec((W, D), lambda i: (0, 0))],
        out_specs=pl.BlockSpec((tm, D), lambda i: (i, 0)),
        compiler_params=_cparams("parallel"),
        name="cross_attn_prompt",
    )(x, g.reshape(1, D), w_q, mkv, mkv, w_o)


def _cross_sample_kernel(x_ref, g_ref, wq_ref, mk_ref, mv_ref, wo_ref, o_ref, *, n_heads, seqs):
    T = SUBLANES
    W = n_heads * HEAD_DIM
    x = x_ref[...]
    q = jnp.dot(_rms(x, g_ref[...]).astype(BF16), wq_ref[...], preferred_element_type=F32) * SCALE
    masks = [_head_mask(T, W, h) for h in range(n_heads)]
    outs = []
    for b in range(seqs):
        qb = q[b * T:(b + 1) * T, :]
        q_exp = jnp.concatenate([jnp.where(masks[h], qb, 0.0) for h in range(n_heads)], axis=0)
        s = _dot_nt(q_exp, mk_ref[b])
        p = jnp.exp(s - jnp.max(s, axis=1, keepdims=True))
        p = p / jnp.sum(p, axis=1, keepdims=True)
        o = _dot(p, mv_ref[b])
        outs.append(sum(jnp.where(masks[h], o[h * T:(h + 1) * T, :], 0.0) for h in range(n_heads)))
    o_ref[...] = x + jnp.dot(jnp.concatenate(outs, axis=0).astype(BF16), wo_ref[...],
                             preferred_element_type=F32)


def cross_attn_sample(x, g, w_q, cache_k, cache_v, w_o, *, layer, n_heads, seqs):
    M, D = x.shape
    T = SUBLANES
    W = n_heads * HEAD_DIM
    mem_len = cache_k.shape[2]
    kv = pl.BlockSpec((None, seqs, mem_len, W), lambda i: (layer, i, 0, 0))
    return pl.pallas_call(
        functools.partial(_cross_sample_kernel, n_heads=n_heads, seqs=seqs),
        out_shape=jax.ShapeDtypeStruct((M, D), F32),
        grid=(M // (seqs * T),),
        in_specs=[pl.BlockSpec((seqs * T, D), lambda i: (i, 0)),
                  pl.BlockSpec((1, D), lambda i: (0, 0)),
                  pl.BlockSpec((D, W), lambda i: (0, 0)),
                  kv, kv,
                  pl.BlockSpec((W, D), lambda i: (0, 0))],
        out_specs=pl.BlockSpec((seqs * T, D), lambda i: (i, 0)),
        compiler_params=_cparams("parallel"),
        name="cross_attn_sample",
    )(x, g.reshape(1, D), w_q, cache_k, cache_v, w_o)


def _rope_tables(pos, n_heads):
    inv = ROPE_BASE ** (-jnp.arange(HALF, dtype=F32) / HALF)
    ang = pos.astype(F32)[:, None] * inv[None, :]
    cos, sin = jnp.cos(ang), jnp.sin(ang)
    return (jnp.tile(jnp.concatenate([cos, cos], axis=-1), (1, n_heads)),
            jnp.tile(jnp.concatenate([-sin, sin], axis=-1), (1, n_heads)))


def _pick(n, candidates):
    for c in candidates:
        if n % c == 0:
            return c
    return n


def kernel(x_prompt, x_sample, mem_prompt, state_ret, state_mlstm_c, state_mlstm_n, state_mlstm_m,
           cache_fox_k, cache_fox_v, cache_fox_logf, cache_mem_k, cache_mem_v, page_table,
           g_mix, w_in, b_in, g_ret, g_mlstm, w_out, g_cross, g_mem, w_cq, w_ckv, w_co,
           g_mlp, w_up, w_down, g_final):
    B, S, D = x_prompt.shape
    NB, T, _ = x_sample.shape
    depth = w_in.shape[0]
    RH = state_ret.shape[2]
    MH = state_mlstm_c.shape[2]
    FH = cache_fox_k.shape[3]
    MEMH = cache_mem_k.shape[3]
    RW, MW, FW, MEMW = RH * HEAD_DIM, MH * HEAD_DIM, FH * HEAD_DIM, MEMH * HEAD_DIM
    n_pool, page = cache_fox_k.shape[1], cache_fox_k.shape[2]
    n_pages = page_table.shape[1]
    mem_len = mem_prompt.shape[1]
    TP = SUBLANES
    assert T <= TP and S % CHUNK == 0 and 2 * MH + FH <= GATE_ROWS
    assert RW % LANES == 0 and MW % LANES == 0 and FW % LANES == 0

    sizes = (RW,) * 4 + (MW,) * 4 + (MH, MH) + (FW,) * 3 + (FH,)
    starts = np.concatenate([[0], np.cumsum(sizes)[:-1]])
    seg = lambda i: np.arange(starts[i], starts[i] + sizes[i])
    order = np.concatenate([seg(i) for i in (0, 1, 2, 3, 4, 5, 6, 7, 10, 11, 12, 8, 9, 13)])
    ret_col0, mls_col0 = 0, 4 * RW
    fox_col0 = mls_col0 + 4 * MW
    gate_col0 = fox_col0 + 3 * FW
    assert fox_col0 % FW == 0 and mls_col0 % MW == 0 and gate_col0 % LANES == 0
    n_cols = gate_col0 + 2 * LANES
    pad = n_cols - len(order)
    w_in_r = jnp.pad(w_in[:, :, order], ((0, 0), (0, 0), (0, pad))).astype(BF16)
    b_in_r = jnp.pad(b_in[:, order], ((0, 0), (0, pad)))
    fox_g0 = 2 * MH

    w_out_b, w_cq_b, w_ckv_b, w_co_b = (w.astype(BF16) for w in (w_out, w_cq, w_ckv, w_co))
    w_up_b, w_down_b = w_up.astype(BF16), w_down.astype(BF16)

    past = n_pages * page
    cos_p, sin_p = _rope_tables(jnp.arange(S, dtype=jnp.int32), RH)
    pos_s = jnp.minimum(past + jnp.arange(TP, dtype=jnp.int32), past + T - 1)
    cos_s, sin_s = _rope_tables(pos_s, RH)
    log_gamma = tuple(float(np.log1p(-np.exp2(-5.0 - h))) for h in range(RH))

    cache_k = cache_fox_k.reshape(depth, n_pool, page, FW)
    cache_v = cache_fox_v.reshape(depth, n_pool, page, FW)
    cache_lf_t = jnp.swapaxes(cache_fox_logf, 2, 3)
    mem_k = cache_mem_k.reshape(depth, NB, mem_len, MEMW)
    mem_v = cache_mem_v.reshape(depth, NB, mem_len, MEMW)

    Mp, Ms = B * S, NB * TP
    xp = x_prompt.reshape(Mp, D)
    xs = jnp.pad(x_sample, ((0, 0), (0, TP - T), (0, 0))).reshape(Ms, D)
    mem = mem_prompt.reshape(B * mem_len, D)

    tm_p = _pick(Mp, (1024, 512, 256, 128))
    tm_s = _pick(Ms, (1024, 512, 256, 128, 64, 32, 16, 8))
    tn_in = _pick(n_cols, (1280, 768, 640, 512, 256, 128))
    tf = _pick(w_up.shape[2], (512, 256, 128))
    tq = _pick(S, (256, 128))
    pps = _pick(n_pages, (4, 2, 1))
    seqs = _pick(NB, (8, 4, 2, 1))
    zeros = lambda *s: jnp.zeros(s, F32)

    def gates_of(z, n_seq, rows, tm):
        gc = gate_activations(z, col_block=gate_col0 // LANES, n_raw=MH, tm=tm)
        gr = jnp.swapaxes(gc[:, :GATE_ROWS].reshape(n_seq, rows, GATE_ROWS), 1, 2)
        return gc, gr

    outs_p = [[] for _ in range(9)]
    outs_s = [[] for _ in range(7)]
    for l in range(depth):
        zp = norm_matmul(xp, g_mix[l], w_in_r[l], b_in_r[l], tm=tm_p, tn=tn_in)
        gcp, grp = gates_of(zp, B, S, tm_p)
        r_p, S_p = retention(zp, cos_p, sin_p, zeros(B, RH, HEAD_DIM, HEAD_DIM), g_ret[l],
                             n_seq=B, L=CHUNK, valid=CHUNK, col0=ret_col0, log_gamma=log_gamma)
        m_p, C_p, n_p, mm_p = mlstm(zp, gcp, grp, zeros(B, MH, HEAD_DIM, HEAD_DIM), zeros(B, MH, HEAD_DIM),
                                    zeros(B, MH), g_mlstm[l], n_seq=B, L=CHUNK, valid=CHUNK,
                                    col0=mls_col0, n_heads=MH)
        fcp, frp = gate_cumsum(gcp, grp, n_seq=B, L=CHUNK)
        f_p = fox_prompt(zp, fcp, frp, n_seq=B, col0=fox_col0, n_heads=FH, g0=fox_g0, tq=tq)
        xp = out_proj(xp, r_p, m_p, f_p, w_out_b[l], tm=tm_p)
        mkv = norm_matmul(mem, g_mem[l], w_ckv_b[l], zeros(2 * MEMW), tm=_pick(B * mem_len, (512, 256)),
                          tn=2 * MEMW)
        xp = cross_attn_prompt(xp, g_cross[l], w_cq_b[l], mkv, w_co_b[l], n_seq=B, n_heads=MEMH,
                               tm=_pick(S, (256, 128)))
        xp = sq_relu_mlp(xp, g_mlp[l], w_up_b[l], w_down_b[l], tm=tm_p, tf=tf)

        zs = norm_matmul(xs, g_mix[l], w_in_r[l], b_in_r[l], tm=tm_s, tn=tn_in)
        gcs, grs = gates_of(zs, NB, TP, tm_s)
        r_s, S_s = retention(zs, cos_s, sin_s, state_ret[l], g_ret[l], n_seq=NB, L=TP, valid=T,
                             col0=ret_col0, log_gamma=log_gamma)
        m_s, C_s, n_s, mm_s = mlstm(zs, gcs, grs, state_mlstm_c[l], state_mlstm_n[l], state_mlstm_m[l],
                                    g_mlstm[l], n_seq=NB, L=TP, valid=T, col0=mls_col0, n_heads=MH)
        f_s = fox_decode(zs, gcs, cache_k, cache_v, cache_lf_t, page_table, layer=l, valid=T,
                         col0=fox_col0, n_heads=FH, g0=fox_g0, pages_per_step=pps)
        xs = out_proj(xs, r_s, m_s, f_s, w_out_b[l], tm=tm_s)
        xs = cross_attn_sample(xs, g_cross[l], w_cq_b[l], mem_k, mem_v, w_co_b[l], layer=l,
                               n_heads=MEMH, seqs=seqs)
        xs = sq_relu_mlp(xs, g_mlp[l], w_up_b[l], w_down_b[l], tm=tm_s, tf=tf)

        fk = lambda z, c: z[:, fox_col0 + c * FW:fox_col0 + (c + 1) * FW]
        new_p = (S_p, C_p, n_p, mm_p.reshape(B, MH),
                 fk(zp, 1).reshape(B, S, FH, HEAD_DIM), fk(zp, 2).reshape(B, S, FH, HEAD_DIM),
                 gcp[:, fox_g0:fox_g0 + FH].reshape(B, S, FH),
                 mkv[:, :MEMW].reshape(B, mem_len, MEMH, HEAD_DIM),
                 mkv[:, MEMW:].reshape(B, mem_len, MEMH, HEAD_DIM))
        new_s = (S_s, C_s, n_s, mm_s.reshape(NB, MH),
                 fk(zs, 1).reshape(NB, TP, FH, HEAD_DIM)[:, :T], fk(zs, 2).reshape(NB, TP, FH, HEAD_DIM)[:, :T],
                 gcs[:, fox_g0:fox_g0 + FH].reshape(NB, TP, FH)[:, :T])
        for lst, a in zip(outs_p, new_p):
            lst.append(a)
        for lst, a in zip(outs_s, new_s):
            lst.append(a)

    y_prompt = final_norm(xp, g_final, tm=tm_p).reshape(B, S, D)
    y_sample = final_norm(xs, g_final, tm=tm_s).reshape(NB, TP, D)[:, :T]
    ret_p, c_p, n_p, m_p, fk_p, fv_p, flf_p, memk_p, memv_p = [jnp.stack(a, axis=0) for a in outs_p]
    ret_s, c_s, n_s, m_s, fk_s, fv_s, flf_s = [jnp.stack(a, axis=0) for a in outs_s]
    return (y_prompt, y_sample, ret_p, ret_s, c_p, c_s, n_p, n_s, m_p, m_s,
            fk_p, fk_s, fv_p, fv_s, flf_p, flf_s, memk_p, memv_p)
```

```python
import functools
import math

import numpy as np
import jax
import jax.numpy as jnp
from jax import lax
from jax.experimental import pallas as pl
from jax.experimental.pallas import tpu as pltpu

F32 = jnp.float32
BF16 = jnp.bfloat16

HEAD_DIM = 64
HALF = HEAD_DIM // 2
EPS = 1e-6
ROPE_BASE = 10000.0
CHUNK = 128
SUBLANES = 8
LANES = 128
GATE_ROWS = 16
AUG = 2 * HEAD_DIM
SCALE = HEAD_DIM ** -0.5
VMEM_LIMIT = 56 * 1024 * 1024
NEG_INF = float("-inf")


def _cparams(*sem):
    return pltpu.CompilerParams(dimension_semantics=sem, vmem_limit_bytes=VMEM_LIMIT)


def _dot(a, b):
    return jnp.dot(a.astype(BF16), b.astype(BF16), preferred_element_type=F32)


def _dot_nt(a, b):
    return lax.dot_general(a.astype(BF16), b.astype(BF16), (((1,), (1,)), ((), ())),
                           preferred_element_type=F32)


def _dot_tn(a, b):
    return lax.dot_general(a.astype(BF16), b.astype(BF16), (((0,), (0,)), ((), ())),
                           preferred_element_type=F32)


def _split3(x):
    hi = x.astype(BF16)
    r1 = x - hi.astype(F32)
    mid = r1.astype(BF16)
    lo = (r1 - mid.astype(F32)).astype(BF16)
    return hi, mid, lo


def _cumsum_rows(x, n):
    t = lax.broadcasted_iota(jnp.int32, (n, n), 0)
    s = lax.broadcasted_iota(jnp.int32, (n, n), 1)
    tri = jnp.where(t >= s, 1.0, 0.0).astype(BF16)
    return sum(jnp.dot(tri, p, preferred_element_type=F32) for p in _split3(x))


def _cumsum_lanes(x, n):
    t = lax.broadcasted_iota(jnp.int32, (n, n), 0)
    s = lax.broadcasted_iota(jnp.int32, (n, n), 1)
    tri = jnp.where(t <= s, 1.0, 0.0).astype(BF16)
    return sum(jnp.dot(p, tri, preferred_element_type=F32) for p in _split3(x))


def _rms(x, g):
    return x * lax.rsqrt(jnp.mean(x * x, axis=-1, keepdims=True) + EPS) * g


def _sigmoid(x):
    return 1.0 / (1.0 + jnp.exp(-x))


def _log_sigmoid(x):
    return jnp.minimum(x, 0.0) - jnp.log1p(jnp.exp(-jnp.abs(x)))


def _head_norm(o):
    return o * lax.rsqrt(jnp.mean(o * o, axis=-1, keepdims=True) + EPS)


def _norm_matmul_kernel(x_ref, g_ref, w_ref, b_ref, o_ref, xn_ref):
    @pl.when(pl.program_id(1) == 0)
    def _():
        xn_ref[...] = _rms(x_ref[...], g_ref[...]).astype(BF16)

    o_ref[...] = jnp.dot(xn_ref[...], w_ref[...], preferred_element_type=F32) + b_ref[...]


def norm_matmul(x, g, w, b, *, tm, tn):
    M, D = x.shape
    N = w.shape[1]
    return pl.pallas_call(
        _norm_matmul_kernel,
        out_shape=jax.ShapeDtypeStruct((M, N), F32),
        grid=(M // tm, N // tn),
        in_specs=[pl.BlockSpec((tm, D), lambda i, j: (i, 0)),
                  pl.BlockSpec((1, D), lambda i, j: (0, 0)),
                  pl.BlockSpec((D, tn), lambda i, j: (0, j)),
                  pl.BlockSpec((1, tn), lambda i, j: (0, j))],
        out_specs=pl.BlockSpec((tm, tn), lambda i, j: (i, j)),
        scratch_shapes=[pltpu.VMEM((tm, D), BF16)],
        compiler_params=_cparams("parallel", "arbitrary"),
        name="norm_matmul",
    )(x, g.reshape(1, D), w, b.reshape(1, N))


def _in_proj_prompt_kernel(x_ref, g_ref, w_ref, b_ref, wt_ref, bt_ref, z_ref, qt_ref, kt_ref, vt_ref,
                           xn_ref, *, fw):
    @pl.when(pl.program_id(2) == 0)
    def _():
        xn_ref[...] = _rms(x_ref[...], g_ref[...]).astype(BF16)

    z_ref[...] = jnp.dot(xn_ref[...], w_ref[...], preferred_element_type=F32) + b_ref[...]

    @pl.when(pl.program_id(2) == pl.num_programs(2) - 1)
    def _():
        zt = lax.dot_general(wt_ref[...], xn_ref[...], (((1,), (1,)), ((), ())),
                             preferred_element_type=F32) + bt_ref[...]
        qt_ref[...] = zt[:fw]
        kt_ref[...] = zt[fw:2 * fw]
        vt_ref[...] = zt[2 * fw:]


def in_proj_prompt(x, g, w, b, wt, bt, *, n_seq, tm, tn):
    M, D = x.shape
    N = w.shape[1]
    S = M // n_seq
    fw = wt.shape[0] // 3
    nt = S // tm
    t_out = jax.ShapeDtypeStruct((n_seq, fw, S), F32)
    t_spec = pl.BlockSpec((None, fw, tm), lambda s, i, j: (s, 0, i))
    return pl.pallas_call(
        functools.partial(_in_proj_prompt_kernel, fw=fw),
        out_shape=(jax.ShapeDtypeStruct((M, N), F32), t_out, t_out, t_out),
        grid=(n_seq, nt, N // tn),
        in_specs=[pl.BlockSpec((tm, D), lambda s, i, j: (s * nt + i, 0)),
                  pl.BlockSpec((1, D), lambda s, i, j: (0, 0)),
                  pl.BlockSpec((D, tn), lambda s, i, j: (0, j)),
                  pl.BlockSpec((1, tn), lambda s, i, j: (0, j)),
                  pl.BlockSpec((3 * fw, D), lambda s, i, j: (0, 0)),
                  pl.BlockSpec((3 * fw, 1), lambda s, i, j: (0, 0))],
        out_specs=(pl.BlockSpec((tm, tn), lambda s, i, j: (s * nt + i, j)), t_spec, t_spec, t_spec),
        scratch_shapes=[pltpu.VMEM((tm, D), BF16)],
        compiler_params=_cparams("parallel", "parallel", "arbitrary"),
        name="in_proj_prompt",
    )(x, g.reshape(1, D), w, b.reshape(1, N), wt, bt.reshape(3 * fw, 1))


def _norm_matmul_t_kernel(x_ref, g_ref, wt_ref, o_ref):
    xn = _rms(x_ref[...], g_ref[...]).astype(BF16)
    o_ref[...] = lax.dot_general(wt_ref[...], xn, (((1,), (1,)), ((), ())), preferred_element_type=F32)


def norm_matmul_t(x, g, wt, *, n_seq):
    M, D = x.shape
    N = wt.shape[0]
    R = M // n_seq
    return pl.pallas_call(
        _norm_matmul_t_kernel,
        out_shape=jax.ShapeDtypeStruct((n_seq, N, R), F32),
        grid=(n_seq,),
        in_specs=[pl.BlockSpec((R, D), lambda s: (s, 0)),
                  pl.BlockSpec((1, D), lambda s: (0, 0)),
                  pl.BlockSpec((N, D), lambda s: (0, 0))],
        out_specs=pl.BlockSpec((None, N, R), lambda s: (s, 0, 0)),
        compiler_params=_cparams("parallel"),
        name="norm_matmul_t",
    )(x, g.reshape(1, D), wt)


def _gates_kernel(z_ref, o_ref, *, n_raw):
    z = z_ref[...]
    lane = lax.broadcasted_iota(jnp.int32, z.shape, 1)
    o_ref[...] = jnp.where(lane >= n_raw, _log_sigmoid(z), z)


def gate_activations(z, *, col_block, n_raw, tm):
    M = z.shape[0]
    return pl.pallas_call(
        functools.partial(_gates_kernel, n_raw=n_raw),
        out_shape=jax.ShapeDtypeStruct((M, LANES), F32),
        grid=(M // tm,),
        in_specs=[pl.BlockSpec((tm, LANES), lambda i: (i, col_block))],
        out_specs=pl.BlockSpec((tm, LANES), lambda i: (i, 0)),
        compiler_params=_cparams("parallel"),
        name="gate_activations",
    )(z)


def _out_proj_kernel(x_ref, r_ref, m_ref, f_ref, wr_ref, wm_ref, wf_ref, o_ref, *, f_transposed):
    f_term = _dot_tn(f_ref[...], wf_ref[...]) if f_transposed else _dot(f_ref[...], wf_ref[...])
    o_ref[...] = x_ref[...] + _dot(r_ref[...], wr_ref[...]) + _dot(m_ref[...], wm_ref[...]) + f_term


def out_proj(x, r, m, f, w_out, *, tm, n_seq=None):
    M, D = x.shape
    wr, wm = r.shape[1], m.shape[1]
    w_r, w_m, w_f = w_out[:wr], w_out[wr:wr + wm], w_out[wr + wm:]
    row = lambda w: pl.BlockSpec((tm, w), lambda i: (i, 0))
    full = lambda a: pl.BlockSpec(a.shape, lambda i: (0, 0))
    if n_seq is None:
        f_spec = row(f.shape[1])
    else:
        nt = M // n_seq // tm
        f_spec = pl.BlockSpec((None, f.shape[1], tm), lambda i: (i // nt, 0, i % nt))
    return pl.pallas_call(
        functools.partial(_out_proj_kernel, f_transposed=n_seq is not None),
        out_shape=jax.ShapeDtypeStruct((M, D), F32),
        grid=(M // tm,),
        in_specs=[row(D), row(wr), row(wm), f_spec, full(w_r), full(w_m), full(w_f)],
        out_specs=row(D),
        compiler_params=_cparams("parallel"),
        name="out_proj",
    )(x, r, m, f, w_r, w_m, w_f)


def _mlp_kernel(x_ref, g_ref, wu_ref, wd_ref, o_ref, xn_ref, acc_ref):
    j = pl.program_id(1)

    @pl.when(j == 0)
    def _():
        xn_ref[...] = _rms(x_ref[...], g_ref[...]).astype(BF16)
        acc_ref[...] = x_ref[...]

    h = jnp.maximum(jnp.dot(xn_ref[...], wu_ref[...], preferred_element_type=F32), 0.0)
    acc_ref[...] += jnp.dot((h * h).astype(BF16), wd_ref[...], preferred_element_type=F32)

    @pl.when(j == pl.num_programs(1) - 1)
    def _():
        o_ref[...] = acc_ref[...]


def sq_relu_mlp(x, g, w_up, w_down, *, tm, tf):
    M, D = x.shape
    FF = w_up.shape[1]
    return pl.pallas_call(
        _mlp_kernel,
        out_shape=jax.ShapeDtypeStruct((M, D), F32),
        grid=(M // tm, FF // tf),
        in_specs=[pl.BlockSpec((tm, D), lambda i, j: (i, 0)),
                  pl.BlockSpec((1, D), lambda i, j: (0, 0)),
                  pl.BlockSpec((D, tf), lambda i, j: (0, j)),
                  pl.BlockSpec((tf, D), lambda i, j: (j, 0))],
        out_specs=pl.BlockSpec((tm, D), lambda i, j: (i, 0)),
        scratch_shapes=[pltpu.VMEM((tm, D), BF16), pltpu.VMEM((tm, D), F32)],
        compiler_params=_cparams("parallel", "arbitrary"),
        name="sq_relu_mlp",
    )(x, g.reshape(1, D), w_up, w_down)


def _final_norm_kernel(x_ref, g_ref, o_ref):
    o_ref[...] = _rms(x_ref[...], g_ref[...])


def final_norm(x, g, *, tm):
    M, D = x.shape
    return pl.pallas_call(
        _final_norm_kernel,
        out_shape=jax.ShapeDtypeStruct((M, D), F32),
        grid=(M // tm,),
        in_specs=[pl.BlockSpec((tm, D), lambda i: (i, 0)), pl.BlockSpec((1, D), lambda i: (0, 0))],
        out_specs=pl.BlockSpec((tm, D), lambda i: (i, 0)),
        compiler_params=_cparams("parallel"),
        name="final_norm",
    )(x, g.reshape(1, D))


def _retention_kernel(q_ref, k_ref, v_ref, g_ref, cos_ref, sin_ref, s0_ref, gain_ref,
                      y_ref, sout_ref, s_sc, *, L, valid, log_gamma):
    c = pl.program_id(1)
    n_heads = len(log_gamma)
    W = n_heads * HEAD_DIM

    @pl.when(c == 0)
    def _():
        s_sc[...] = s0_ref[0]

    cos = cos_ref[...]
    sin = sin_ref[...]
    lane = lax.broadcasted_iota(jnp.int32, (L, W), 1)
    first_half = (lane % HEAD_DIM) < HALF

    def rope(x):
        swapped = jnp.where(first_half, pltpu.roll(x, W - HALF, 1), pltpu.roll(x, HALF, 1))
        return x * cos + swapped * sin

    q = rope(q_ref[...])
    k = rope(k_ref[...]) * SCALE
    v = v_ref[...]
    t_i = lax.broadcasted_iota(jnp.int32, (L, L), 0)
    s_i = lax.broadcasted_iota(jnp.int32, (L, L), 1)
    causal = t_i >= s_i
    diff = jnp.where(causal, (t_i - s_i).astype(F32), 0.0)
    tcol = lax.broadcasted_iota(jnp.int32, (L, 1), 0).astype(F32)
    row_valid = tcol < valid
    outs = []
    for h in range(n_heads):
        lg = log_gamma[h]
        sl = slice(h * HEAD_DIM, (h + 1) * HEAD_DIM)
        qh, kh, vh = q[:, sl], k[:, sl], v[:, sl]
        decay = jnp.where(causal, jnp.exp(lg * diff), 0.0)
        inner = _dot(_dot_nt(qh, kh) * decay, vh)
        S = s_sc[h]
        cross = _dot(qh, S) * jnp.exp(lg * (tcol + 1.0))
        k_dec = jnp.where(row_valid, kh * jnp.exp(lg * (valid - 1.0 - tcol)), 0.0)
        s_sc[h] = math.exp(lg * valid) * S + _dot_tn(k_dec, vh)
        outs.append(_head_norm(inner + cross))
    y_ref[...] = jnp.concatenate(outs, axis=-1) * gain_ref[...] * (g_ref[...] * _sigmoid(g_ref[...]))

    @pl.when(c == pl.num_programs(1) - 1)
    def _():
        sout_ref[0] = s_sc[...]


def retention(z, cos, sin, s0, gain, *, n_seq, L, valid, col0, log_gamma):
    M = z.shape[0]
    n_heads = len(log_gamma)
    W = n_heads * HEAD_DIM
    nc = M // (n_seq * L)
    cb = col0 // W
    zspec = lambda j: pl.BlockSpec((L, W), lambda b, c: (b * nc + c, cb + j))
    tab = pl.BlockSpec((L, W), lambda b, c: (c, 0))
    st = pl.BlockSpec((1, n_heads, HEAD_DIM, HEAD_DIM), lambda b, c: (b, 0, 0, 0))
    return pl.pallas_call(
        functools.partial(_retention_kernel, L=L, valid=valid, log_gamma=log_gamma),
        out_shape=(jax.ShapeDtypeStruct((M, W), F32),
                   jax.ShapeDtypeStruct((n_seq, n_heads, HEAD_DIM, HEAD_DIM), F32)),
        grid=(n_seq, nc),
        in_specs=[zspec(0), zspec(1), zspec(2), zspec(3), tab, tab, st,
                  pl.BlockSpec((1, W), lambda b, c: (0, 0))],
        out_specs=(pl.BlockSpec((L, W), lambda b, c: (b * nc + c, 0)), st),
        scratch_shapes=[pltpu.VMEM((n_heads, HEAD_DIM, HEAD_DIM), F32)],
        compiler_params=_cparams("parallel", "arbitrary"),
        name="retention",
    )(z, z, z, z, cos, sin, s0, gain.reshape(1, W))


def _mlstm_kernel(q_ref, k_ref, v_ref, og_ref, gc_ref, gr_ref, c0_ref, n0_ref, m0_ref, gain_ref,
                  y_ref, cout_ref, nout_ref, mout_ref, c_sc, n_sc, m_sc, *, L, valid, n_heads):
    c = pl.program_id(1)

    @pl.when(c == 0)
    def _():
        c_sc[...] = c0_ref[0]
        n_sc[...] = n0_ref[0]
        m_sc[...] = m0_ref[0]

    q = q_ref[...]
    k = k_ref[...] * SCALE
    v = v_ref[...]
    og = _sigmoid(og_ref[...])
    gc = gc_ref[...]
    gr = gr_ref[0]
    f_col = _cumsum_rows(gc, L)
    f_row = _cumsum_lanes(gr, L)
    t_i = lax.broadcasted_iota(jnp.int32, (L, L), 0)
    s_i = lax.broadcasted_iota(jnp.int32, (L, L), 1)
    causal = t_i >= s_i
    row_valid = lax.broadcasted_iota(jnp.int32, (L, 1), 0) < valid
    outs = []
    for h in range(n_heads):
        sl = slice(h * HEAD_DIM, (h + 1) * HEAD_DIM)
        qh, kh, vh = q[:, sl], k[:, sl], v[:, sl]
        fc = f_col[:, n_heads + h:n_heads + h + 1]
        fr = f_row[n_heads + h:n_heads + h + 1, :]
        ic = gc[:, h:h + 1]
        ir = gr[h:h + 1, :]
        m_prev = m_sc[:, h:h + 1]
        C = c_sc[h]
        n = n_sc[h:h + 1, :]
        log_d = jnp.where(causal, fc - fr + ir, NEG_INF)
        inter = fc + m_prev
        m_t = jnp.maximum(inter, jnp.max(log_d, axis=1, keepdims=True))
        d = jnp.exp(log_d - m_t)
        w_inter = jnp.exp(inter - m_t)
        qk = _dot_nt(qh, kh) * d
        num = _dot(qk, vh) + w_inter * _dot_nt(qh, C)
        den = jnp.sum(qk, axis=1, keepdims=True) + w_inter * jnp.sum(qh * n, axis=1, keepdims=True)
        hh = num / jnp.maximum(jnp.abs(den), jnp.exp(-m_t))
        outs.append(_head_norm(og[:, sl] * hh))
        m_last = m_t[valid - 1:valid, :]
        f_last = fc[valid - 1:valid, :]
        w_s = jnp.where(row_valid, jnp.exp(f_last - fc + ic - m_last), 0.0)
        w_c = jnp.exp(f_last + m_prev - m_last)
        c_sc[h] = w_c * C + _dot_tn(vh * w_s, kh)
        n_sc[h:h + 1, :] = w_c * n + jnp.sum(kh * w_s, axis=0, keepdims=True)
        m_sc[:, h:h + 1] = m_last
    y_ref[...] = jnp.concatenate(outs, axis=-1) * gain_ref[...]

    @pl.when(c == pl.num_programs(1) - 1)
    def _():
        cout_ref[0] = c_sc[...]
        nout_ref[0] = n_sc[...]
        mout_ref[0] = m_sc[...]


def mlstm(z, gc, gr, c0, n0, m0, gain, *, n_seq, L, valid, col0, n_heads):
    M = z.shape[0]
    W = n_heads * HEAD_DIM
    nc = M // (n_seq * L)
    cb = col0 // W
    zspec = lambda j: pl.BlockSpec((L, W), lambda b, c: (b * nc + c, cb + j))
    cst = pl.BlockSpec((1, n_heads, HEAD_DIM, HEAD_DIM), lambda b, c: (b, 0, 0, 0))
    nst = pl.BlockSpec((1, n_heads, HEAD_DIM), lambda b, c: (b, 0, 0))
    mst = pl.BlockSpec((1, 1, n_heads), lambda b, c: (b, 0, 0))
    return pl.pallas_call(
        functools.partial(_mlstm_kernel, L=L, valid=valid, n_heads=n_heads),
        out_shape=(jax.ShapeDtypeStruct((M, W), F32),
                   jax.ShapeDtypeStruct((n_seq, n_heads, HEAD_DIM, HEAD_DIM), F32),
                   jax.ShapeDtypeStruct((n_seq, n_heads, HEAD_DIM), F32),
                   jax.ShapeDtypeStruct((n_seq, 1, n_heads), F32)),
        grid=(n_seq, nc),
        in_specs=[zspec(0), zspec(1), zspec(2), zspec(3),
                  pl.BlockSpec((L, LANES), lambda b, c: (b * nc + c, 0)),
                  pl.BlockSpec((1, GATE_ROWS, L), lambda b, c: (b, 0, c)),
                  cst, nst, mst,
                  pl.BlockSpec((1, W), lambda b, c: (0, 0))],
        out_specs=(pl.BlockSpec((L, W), lambda b, c: (b * nc + c, 0)), cst, nst, mst),
        scratch_shapes=[pltpu.VMEM((n_heads, HEAD_DIM, HEAD_DIM), F32),
                        pltpu.VMEM((n_heads, HEAD_DIM), F32),
                        pltpu.VMEM((1, n_heads), F32)],
        compiler_params=_cparams("parallel", "arbitrary"),
        name="mlstm",
    )(z, z, z, z, gc, gr, c0, n0, m0.reshape(n_seq, 1, n_heads), gain.reshape(1, W))


def _fox_prep_kernel(gc_ref, gr_ref, k_ref, qt_ref, vt_ref, ka_ref, qa_ref, vb_ref, cc_sc, cr_sc,
                     *, L, n_heads, g0):
    @pl.when(pl.program_id(1) == 0)
    def _():
        cc_sc[...] = jnp.zeros_like(cc_sc)
        cr_sc[...] = jnp.zeros_like(cr_sc)

    fc = _cumsum_rows(gc_ref[...], L) + cc_sc[...]
    fr = _cumsum_lanes(gr_ref[...], L) + cr_sc[...]
    cc_sc[...] = fc[L - 1:L, :]
    cr_sc[...] = fr[:, L - 1:L]
    k = k_ref[...]
    qt = qt_ref[...] * SCALE
    lane = lax.broadcasted_iota(jnp.int32, (L, HEAD_DIM), 1)
    row = lax.broadcasted_iota(jnp.int32, (HEAD_DIM, L), 0)
    for h in range(n_heads):
        sl = slice(h * HEAD_DIM, (h + 1) * HEAD_DIM)
        kh, km, kl = (p.astype(F32) for p in _split3(fc[:, g0 + h:g0 + h + 1]))
        k_aug = jnp.where(lane < 3, 1.0,
                          jnp.where(lane == 3, -kh, jnp.where(lane == 4, -km, jnp.where(lane == 5, -kl, 0.0))))
        ka_ref[:, h * AUG:(h + 1) * AUG] = jnp.concatenate([k[:, sl], k_aug], axis=1).astype(BF16)
        qh, qm, ql = (p.astype(F32) for p in _split3(fr[g0 + h:g0 + h + 1, :]))
        q_aug = jnp.where(row == 0, qh,
                          jnp.where(row == 1, qm, jnp.where(row == 2, ql, jnp.where(row < 6, 1.0, 0.0))))
        qa_ref[h * AUG:(h + 1) * AUG, :] = jnp.concatenate([qt[sl, :], q_aug], axis=0).astype(BF16)
    vb_ref[...] = vt_ref[...].astype(BF16)


def fox_prep(gc, gr, z, qt, vt, *, n_seq, L, k_col0, n_heads, g0):
    M = gc.shape[0]
    S = M // n_seq
    nc = S // L
    W = n_heads * HEAD_DIM
    t_spec = pl.BlockSpec((None, W, L), lambda b, c: (b, 0, c))
    return pl.pallas_call(
        functools.partial(_fox_prep_kernel, L=L, n_heads=n_heads, g0=g0),
        out_shape=(jax.ShapeDtypeStruct((M, n_heads * AUG), BF16),
                   jax.ShapeDtypeStruct((n_seq, n_heads * AUG, S), BF16),
                   jax.ShapeDtypeStruct((n_seq, W, S), BF16)),
        grid=(n_seq, nc),
        in_specs=[pl.BlockSpec((L, LANES), lambda b, c: (b * nc + c, 0)),
                  pl.BlockSpec((None, GATE_ROWS, L), lambda b, c: (b, 0, c)),
                  pl.BlockSpec((L, W), lambda b, c: (b * nc + c, k_col0 // W)),
                  t_spec, t_spec],
        out_specs=(pl.BlockSpec((L, n_heads * AUG), lambda b, c: (b * nc + c, 0)),
                   pl.BlockSpec((None, n_heads * AUG, L), lambda b, c: (b, 0, c)),
                   t_spec),
        scratch_shapes=[pltpu.VMEM((1, LANES), F32), pltpu.VMEM((GATE_ROWS, 1), F32)],
        compiler_params=_cparams("parallel", "arbitrary"),
        name="fox_prep",
    )(gc, gr, z, qt, vt)


def _fox_flash_kernel(qi_ref, kj_ref, ka_ref, qa_ref, vt_ref, o_ref, m_sc, l_sc, acc_sc, *, tq, n_heads):
    step = pl.program_id(1)
    qi = qi_ref[step]
    kj = kj_ref[step]

    @pl.when(kj == 0)
    def _():
        m_sc[...] = jnp.full_like(m_sc, NEG_INF)
        l_sc[...] = jnp.zeros_like(l_sc)
        acc_sc[...] = jnp.zeros_like(acc_sc)

    def update(diagonal):
        if diagonal:
            key = lax.broadcasted_iota(jnp.int32, (tq, tq), 0)
            qry = lax.broadcasted_iota(jnp.int32, (tq, tq), 1)
            visible = key <= qry
        for h in range(n_heads):
            s = jnp.dot(ka_ref[:, h * AUG:(h + 1) * AUG], qa_ref[h * AUG:(h + 1) * AUG, :],
                        preferred_element_type=F32)
            if diagonal:
                s = jnp.where(visible, s, NEG_INF)
            hs = slice(h * HEAD_DIM, (h + 1) * HEAD_DIM)
            m_old = m_sc[h:h + 1, :]
            m_new = jnp.maximum(m_old, jnp.max(s, axis=0, keepdims=True))
            p = jnp.exp(s - m_new)
            alpha = jnp.exp(m_old - m_new)
            l_sc[h:h + 1, :] = alpha * l_sc[h:h + 1, :] + jnp.sum(p, axis=0, keepdims=True)
            acc_sc[hs, :] = alpha * acc_sc[hs, :] + jnp.dot(vt_ref[hs, :], p.astype(BF16),
                                                            preferred_element_type=F32)
            m_sc[h:h + 1, :] = m_new

    @pl.when(kj < qi)
    def _():
        update(False)

    @pl.when(kj == qi)
    def _():
        update(True)
        o_ref[...] = jnp.concatenate(
            [acc_sc[h * HEAD_DIM:(h + 1) * HEAD_DIM, :] / l_sc[h:h + 1, :] for h in range(n_heads)], axis=0)


def fox_flash(ka, qa, vt, *, tq, n_heads):
    n_seq, W, S = vt.shape
    nq = S // tq
    pairs = [(i, j) for i in range(nq) for j in range(i + 1)]
    qi_tab = jnp.asarray([p[0] for p in pairs], jnp.int32)
    kj_tab = jnp.asarray([p[1] for p in pairs], jnp.int32)
    grid_spec = pltpu.PrefetchScalarGridSpec(
        num_scalar_prefetch=2,
        grid=(n_seq, len(pairs)),
        in_specs=[pl.BlockSpec((tq, n_heads * AUG), lambda b, s, qi, kj: (b * nq + kj[s], 0)),
                  pl.BlockSpec((None, n_heads * AUG, tq), lambda b, s, qi, kj: (b, 0, qi[s])),
                  pl.BlockSpec((None, W, tq), lambda b, s, qi, kj: (b, 0, kj[s]))],
        out_specs=pl.BlockSpec((None, W, tq), lambda b, s, qi, kj: (b, 0, qi[s])),
        scratch_shapes=[pltpu.VMEM((n_heads, tq), F32), pltpu.VMEM((n_heads, tq), F32),
                        pltpu.VMEM((W, tq), F32)],
    )
    return pl.pallas_call(
        functools.partial(_fox_flash_kernel, tq=tq, n_heads=n_heads),
        out_shape=jax.ShapeDtypeStruct((n_seq, W, S), F32),
        grid_spec=grid_spec,
        compiler_params=_cparams("parallel", "arbitrary"),
        name="fox_flash",
    )(qi_tab, kj_tab, ka, qa, vt)


def _fox_decode_kernel(pt_ref, q_ref, kn_ref, vn_ref, gc_ref, gr_ref, *rest, valid, n_heads, g0, n_pages):
    k_refs = rest[:n_pages]
    v_refs = rest[n_pages:2 * n_pages]
    lf_refs = rest[2 * n_pages:3 * n_pages]
    o_ref = rest[3 * n_pages]
    del pt_ref
    T = SUBLANES
    H = n_heads
    page = lf_refs[0].shape[-1]

    q = q_ref[...] * SCALE
    kn = kn_ref[...]
    vn = vn_ref[...]
    gc = gc_ref[...]
    gr = gr_ref[...]
    t_i = lax.broadcasted_iota(jnp.int32, (T, LANES), 0)
    cum_col = sum(jnp.where(t_i >= u, gc[u:u + 1, :], 0.0) for u in range(valid))
    u_i = lax.broadcasted_iota(jnp.int32, (GATE_ROWS, T), 1)
    cum_row = sum(jnp.where(u_i >= u, gr[:, u:u + 1], 0.0) for u in range(valid))
    t_q = lax.broadcasted_iota(jnp.int32, (T, T), 0)
    u_k = lax.broadcasted_iota(jnp.int32, (T, T), 1)
    new_visible = (u_k <= t_q) & (u_k < valid)

    s_a = lax.broadcasted_iota(jnp.int32, (page, page), 0)
    s_b = lax.broadcasted_iota(jnp.int32, (page, page), 1)
    later = jnp.where(s_a > s_b, 1.0, 0.0).astype(BF16)
    lf = jnp.concatenate([r[...] for r in lf_refs], axis=0)
    within = sum(jnp.dot(p, later, preferred_element_type=F32) for p in _split3(lf))
    totals = within[:, 0:1] + lf[:, 0:1]
    carry = jnp.zeros((H, 1), F32)
    suffix = [None] * n_pages
    for p in reversed(range(n_pages)):
        suffix[p] = within[p * H:(p + 1) * H, :] + carry
        carry = carry + totals[p * H:(p + 1) * H, :]

    outs = []
    for h in range(H):
        sl = slice(h * HEAD_DIM, (h + 1) * HEAD_DIM)
        qh = q[:, sl]
        f_t = cum_col[:, g0 + h:g0 + h + 1]
        kt = jnp.concatenate([r[h] for r in k_refs], axis=1)
        bias = jnp.concatenate([sp[h:h + 1, :] for sp in suffix], axis=1)
        s_past = _dot(qh, kt) + (f_t + bias)
        s_new = _dot_nt(qh, kn[:, sl]) + (f_t - cum_row[g0 + h:g0 + h + 1, :])
        s_new = jnp.where(new_visible, s_new, NEG_INF)
        m = jnp.maximum(jnp.max(s_past, axis=1, keepdims=True), jnp.max(s_new, axis=1, keepdims=True))
        p_past = jnp.exp(s_past - m)
        p_new = jnp.exp(s_new - m)
        denom = jnp.sum(p_past, axis=1, keepdims=True) + jnp.sum(p_new, axis=1, keepdims=True)
        vt = jnp.concatenate([r[h] for r in v_refs], axis=1)
        outs.append((_dot_nt(p_past, vt) + _dot(p_new, vn[:, sl])) / denom)
    o_ref[...] = jnp.concatenate(outs, axis=1)


def fox_decode(z, gc, gr, cache_kt, cache_vt, cache_lf_t, page_table, *, layer, valid, col0, n_heads, g0):
    M = z.shape[0]
    T = SUBLANES
    n_seq = M // T
    W = n_heads * HEAD_DIM
    n_pages = page_table.shape[1]
    page = cache_kt.shape[-1]
    cb = col0 // W

    def page_idx(i, nd):
        return lambda b, pt: (layer, pt[b * n_pages + i]) + (0,) * nd

    kv_specs = [pl.BlockSpec((None, None, n_heads, HEAD_DIM, page), page_idx(i, 3)) for i in range(n_pages)]
    lf_specs = [pl.BlockSpec((None, None, n_heads, page), page_idx(i, 2)) for i in range(n_pages)]
    zspec = lambda c: pl.BlockSpec((T, W), lambda b, pt: (b, c))
    grid_spec = pltpu.PrefetchScalarGridSpec(
        num_scalar_prefetch=1,
        grid=(n_seq,),
        in_specs=[zspec(cb), zspec(cb + 1), zspec(cb + 2),
                  pl.BlockSpec((T, LANES), lambda b, pt: (b, 0)),
                  pl.BlockSpec((None, GATE_ROWS, T), lambda b, pt: (b, 0, 0))] + kv_specs + kv_specs + lf_specs,
        out_specs=pl.BlockSpec((T, W), lambda b, pt: (b, 0)),
    )
    return pl.pallas_call(
        functools.partial(_fox_decode_kernel, valid=valid, n_heads=n_heads, g0=g0, n_pages=n_pages),
        out_shape=jax.ShapeDtypeStruct((M, W), F32),
        grid_spec=grid_spec,
        compiler_params=_cparams("parallel"),
        name="fox_decode",
    )(page_table.reshape(-1), z, z, z, gc, gr,
      *([cache_kt] * n_pages), *([cache_vt] * n_pages), *([cache_lf_t] * n_pages))


def _softmax_rows(s):
    p = jnp.exp(s - jnp.max(s, axis=1, keepdims=True))
    return p / jnp.sum(p, axis=1, keepdims=True)


def _cross_prompt_kernel(x_ref, g_ref, wq_ref, kt_ref, vt_ref, wo_ref, o_ref, *, n_heads):
    x = x_ref[...]
    q = jnp.dot(_rms(x, g_ref[...]).astype(BF16), wq_ref[...], preferred_element_type=F32) * SCALE
    outs = []
    for h in range(n_heads):
        sl = slice(h * HEAD_DIM, (h + 1) * HEAD_DIM)
        p = _softmax_rows(_dot(q[:, sl], kt_ref[sl, :]))
        outs.append(_dot_nt(p, vt_ref[sl, :]))
    o_ref[...] = x + jnp.dot(jnp.concatenate(outs, axis=-1).astype(BF16), wo_ref[...],
                             preferred_element_type=F32)


def cross_attn_prompt(x, g, w_q, mkv_t, w_o, *, n_heads, tm):
    M, D = x.shape
    W = n_heads * HEAD_DIM
    n_seq, _, mem_len = mkv_t.shape
    per_seq = M // n_seq // tm
    return pl.pallas_call(
        functools.partial(_cross_prompt_kernel, n_heads=n_heads),
        out_shape=jax.ShapeDtypeStruct((M, D), F32),
        grid=(M // tm,),
        in_specs=[pl.BlockSpec((tm, D), lambda i: (i, 0)),
                  pl.BlockSpec((1, D), lambda i: (0, 0)),
                  pl.BlockSpec((D, W), lambda i: (0, 0)),
                  pl.BlockSpec((None, W, mem_len), lambda i: (i // per_seq, 0, 0)),
                  pl.BlockSpec((None, W, mem_len), lambda i: (i // per_seq, 1, 0)),
                  pl.BlockSpec((W, D), lambda i: (0, 0))],
        out_specs=pl.BlockSpec((tm, D), lambda i: (i, 0)),
        compiler_params=_cparams("parallel"),
        name="cross_attn_prompt",
    )(x, g.reshape(1, D), w_q, mkv_t, mkv_t, w_o)


def _cross_sample_kernel(x_ref, g_ref, wq_ref, kt_ref, vt_ref, wo_ref, o_ref, *, n_heads, seqs):
    T = SUBLANES
    x = x_ref[...]
    q = jnp.dot(_rms(x, g_ref[...]).astype(BF16), wq_ref[...], preferred_element_type=F32) * SCALE
    rows = []
    for b in range(seqs):
        qb = q[b * T:(b + 1) * T, :]
        outs = []
        for h in range(n_heads):
            sl = slice(h * HEAD_DIM, (h + 1) * HEAD_DIM)
            p = _softmax_rows(_dot(qb[:, sl], kt_ref[b, h]))
            outs.append(_dot_nt(p, vt_ref[b, h]))
        rows.append(jnp.concatenate(outs, axis=1))
    o_ref[...] = x + jnp.dot(jnp.concatenate(rows, axis=0).astype(BF16), wo_ref[...],
                             preferred_element_type=F32)


def cross_attn_sample(x, g, w_q, cache_kt, cache_vt, w_o, *, layer, n_heads, seqs):
    M, D = x.shape
    T = SUBLANES
    W = n_heads * HEAD_DIM
    mem_len = cache_kt.shape[-1]
    kv = pl.BlockSpec((None, seqs, n_heads, HEAD_DIM, mem_len), lambda i: (layer, i, 0, 0, 0))
    return pl.pallas_call(
        functools.partial(_cross_sample_kernel, n_heads=n_heads, seqs=seqs),
        out_shape=jax.ShapeDtypeStruct((M, D), F32),
        grid=(M // (seqs * T),),
        in_specs=[pl.BlockSpec((seqs * T, D), lambda i: (i, 0)),
                  pl.BlockSpec((1, D), lambda i: (0, 0)),
                  pl.BlockSpec((D, W), lambda i: (0, 0)),
                  kv, kv,
                  pl.BlockSpec((W, D), lambda i: (0, 0))],
        out_specs=pl.BlockSpec((seqs * T, D), lambda i: (i, 0)),
        compiler_params=_cparams("parallel"),
        name="cross_attn_sample",
    )(x, g.reshape(1, D), w_q, cache_kt, cache_vt, w_o)


def _rope_tables(pos, n_heads):
    inv = ROPE_BASE ** (-jnp.arange(HALF, dtype=F32) / HALF)
    ang = pos.astype(F32)[:, None] * inv[None, :]
    cos, sin = jnp.cos(ang), jnp.sin(ang)
    return (jnp.tile(jnp.concatenate([cos, cos], axis=-1), (1, n_heads)),
            jnp.tile(jnp.concatenate([-sin, sin], axis=-1), (1, n_heads)))


def _pick(n, candidates):
    for c in candidates:
        if n % c == 0:
            return c
    return n


def kernel(x_prompt, x_sample, mem_prompt, state_ret, state_mlstm_c, state_mlstm_n, state_mlstm_m,
           cache_fox_k, cache_fox_v, cache_fox_logf, cache_mem_k, cache_mem_v, page_table,
           g_mix, w_in, b_in, g_ret, g_mlstm, w_out, g_cross, g_mem, w_cq, w_ckv, w_co,
           g_mlp, w_up, w_down, g_final):
    B, S, D = x_prompt.shape
    NB, T, _ = x_sample.shape
    depth = w_in.shape[0]
    RH = state_ret.shape[2]
    MH = state_mlstm_c.shape[2]
    FH = cache_fox_k.shape[3]
    MEMH = cache_mem_k.shape[3]
    RW, MW, FW, MEMW = RH * HEAD_DIM, MH * HEAD_DIM, FH * HEAD_DIM, MEMH * HEAD_DIM
    n_pages = page_table.shape[1]
    page = cache_fox_k.shape[2]
    mem_len = mem_prompt.shape[1]
    TP = SUBLANES
    assert T <= TP and S % CHUNK == 0 and 2 * MH + FH <= GATE_ROWS
    assert RW % LANES == 0 and MW % LANES == 0 and FW % LANES == 0

    sizes = (RW,) * 4 + (MW,) * 4 + (MH, MH) + (FW,) * 3 + (FH,)
    starts = np.concatenate([[0], np.cumsum(sizes)[:-1]])
    seg = lambda *ids: np.concatenate([np.arange(starts[i], starts[i] + sizes[i]) for i in ids])
    ret_col0, mls_col0 = 0, 4 * RW
    fox_col0 = mls_col0 + 4 * MW
    assert fox_col0 % FW == 0 and mls_col0 % MW == 0
    fox_g0 = 2 * MH

    def regroup(ids, gate_col0):
        order = seg(*ids, 8, 9, 13)
        n_cols = gate_col0 + 2 * LANES
        pad = n_cols - len(order)
        return (jnp.pad(w_in[:, :, order], ((0, 0), (0, 0), (0, pad))).astype(BF16),
                jnp.pad(b_in[:, order], ((0, 0), (0, pad))), n_cols)

    gate_col0_s = fox_col0 + 3 * FW
    w_in_s, b_in_s, n_cols_s = regroup((0, 1, 2, 3, 4, 5, 6, 7, 10, 11, 12), gate_col0_s)
    gate_col0_p = fox_col0 + FW
    w_in_p, b_in_p, n_cols_p = regroup((0, 1, 2, 3, 4, 5, 6, 7, 11), gate_col0_p)
    fox_cols = seg(10, 11, 12)
    w_fox_t = jnp.swapaxes(w_in[:, :, fox_cols], 1, 2).astype(BF16)
    b_fox = b_in[:, fox_cols]

    w_out_b, w_cq_b, w_co_b = (w.astype(BF16) for w in (w_out, w_cq, w_co))
    w_ckv_t = jnp.swapaxes(w_ckv, 1, 2).astype(BF16)
    w_up_b, w_down_b = w_up.astype(BF16), w_down.astype(BF16)

    past = n_pages * page
    cos_p, sin_p = _rope_tables(jnp.arange(S, dtype=jnp.int32), RH)
    pos_s = jnp.minimum(past + jnp.arange(TP, dtype=jnp.int32), past + T - 1)
    cos_s, sin_s = _rope_tables(pos_s, RH)
    log_gamma = tuple(float(np.log1p(-np.exp2(-5.0 - h))) for h in range(RH))

    cache_kt = jnp.transpose(cache_fox_k, (0, 1, 3, 4, 2))
    cache_vt = jnp.transpose(cache_fox_v, (0, 1, 3, 4, 2))
    cache_lf_t = jnp.swapaxes(cache_fox_logf, 2, 3)
    mem_kt = jnp.transpose(cache_mem_k, (0, 1, 3, 4, 2))
    mem_vt = jnp.transpose(cache_mem_v, (0, 1, 3, 4, 2))

    Mp, Ms = B * S, NB * TP
    xp = x_prompt.reshape(Mp, D)
    xs = jnp.pad(x_sample, ((0, 0), (0, TP - T), (0, 0))).reshape(Ms, D)
    mem = mem_prompt.reshape(B * mem_len, D)

    tm_p = _pick(S, (1024, 512, 256, 128))
    tm_in = _pick(S, (512, 256, 128))
    tm_s = _pick(Ms, (1024, 512, 256, 128, 64, 32, 16, 8))
    tn_s = _pick(n_cols_s, (1280, 768, 640, 512, 256, 128))
    tn_p = _pick(n_cols_p, (1408, 1024, 768, 640, 512, 256, 128))
    tf = _pick(w_up.shape[2], (512, 256, 128))
    tq = _pick(S, (256, 128))
    l_prep = _pick(S, (512, 256, 128))
    seqs = _pick(NB, (8, 4, 2, 1))
    zeros = lambda *s: jnp.zeros(s, F32)

    def gates_of(z, gate_col0, n_seq, rows, tm):
        gc = gate_activations(z, col_block=gate_col0 // LANES, n_raw=MH, tm=tm)
        gr = jnp.swapaxes(gc[:, :GATE_ROWS].reshape(n_seq, rows, GATE_ROWS), 1, 2)
        return gc, gr

    outs_p = [[] for _ in range(9)]
    outs_s = [[] for _ in range(7)]
    for l in range(depth):
        zp, qt_p, kt_p, vt_p = in_proj_prompt(xp, g_mix[l], w_in_p[l], b_in_p[l], w_fox_t[l], b_fox[l],
                                              n_seq=B, tm=tm_in, tn=tn_p)
        gcp, grp = gates_of(zp, gate_col0_p, B, S, tm_p)
        r_p, S_p = retention(zp, cos_p, sin_p, zeros(B, RH, HEAD_DIM, HEAD_DIM), g_ret[l],
                             n_seq=B, L=CHUNK, valid=CHUNK, col0=ret_col0, log_gamma=log_gamma)
        m_p, C_p, n_p, mm_p = mlstm(zp, gcp, grp, zeros(B, MH, HEAD_DIM, HEAD_DIM), zeros(B, MH, HEAD_DIM),
                                    zeros(B, MH), g_mlstm[l], n_seq=B, L=CHUNK, valid=CHUNK,
                                    col0=mls_col0, n_heads=MH)
        ka, qa, vb = fox_prep(gcp, grp, zp, qt_p, vt_p, n_seq=B, L=l_prep, k_col0=fox_col0,
                              n_heads=FH, g0=fox_g0)
        f_p = fox_flash(ka, qa, vb, tq=tq, n_heads=FH)
        xp = out_proj(xp, r_p, m_p, f_p, w_out_b[l], tm=tm_p, n_seq=B)
        mkv_t = norm_matmul_t(mem, g_mem[l], w_ckv_t[l], n_seq=B)
        xp = cross_attn_prompt(xp, g_cross[l], w_cq_b[l], mkv_t, w_co_b[l], n_heads=MEMH,
                               tm=_pick(S, (256, 128)))
        xp = sq_relu_mlp(xp, g_mlp[l], w_up_b[l], w_down_b[l], tm=tm_p, tf=tf)

        zs = norm_matmul(xs, g_mix[l], w_in_s[l], b_in_s[l], tm=tm_s, tn=tn_s)
        gcs, grs = gates_of(zs, gate_col0_s, NB, TP, tm_s)
        r_s, S_s = retention(zs, cos_s, sin_s, state_ret[l], g_ret[l], n_seq=NB, L=TP, valid=T,
                             col0=ret_col0, log_gamma=log_gamma)
        m_s, C_s, n_s, mm_s = mlstm(zs, gcs, grs, state_mlstm_c[l], state_mlstm_n[l], state_mlstm_m[l],
                                    g_mlstm[l], n_seq=NB, L=TP, valid=T, col0=mls_col0, n_heads=MH)
        f_s = fox_decode(zs, gcs, grs, cache_kt, cache_vt, cache_lf_t, page_table, layer=l, valid=T,
                         col0=fox_col0, n_heads=FH, g0=fox_g0)
        xs = out_proj(xs, r_s, m_s, f_s, w_out_b[l], tm=tm_s)
        xs = cross_attn_sample(xs, g_cross[l], w_cq_b[l], mem_kt, mem_vt, w_co_b[l], layer=l,
                               n_heads=MEMH, seqs=seqs)
        xs = sq_relu_mlp(xs, g_mlp[l], w_up_b[l], w_down_b[l], tm=tm_s, tf=tf)

        heads_last = lambda a, n: jnp.transpose(a.reshape(a.shape[0], n, HEAD_DIM, a.shape[-1]), (0, 3, 1, 2))
        fks = lambda c: zs[:, fox_col0 + c * FW:fox_col0 + (c + 1) * FW].reshape(NB, TP, FH, HEAD_DIM)[:, :T]
        new_p = (S_p, C_p, n_p, mm_p.reshape(B, MH),
                 heads_last(kt_p, FH), heads_last(vt_p, FH),
                 gcp[:, fox_g0:fox_g0 + FH].reshape(B, S, FH),
                 heads_last(mkv_t[:, :MEMW], MEMH), heads_last(mkv_t[:, MEMW:], MEMH))
        new_s = (S_s, C_s, n_s, mm_s.reshape(NB, MH), fks(1), fks(2),
                 gcs[:, fox_g0:fox_g0 + FH].reshape(NB, TP, FH)[:, :T])
        for lst, a in zip(outs_p, new_p):
            lst.append(a)
        for lst, a in zip(outs_s, new_s):
            lst.append(a)

    y_prompt = final_norm(xp, g_final, tm=tm_p).reshape(B, S, D)
    y_sample = final_norm(xs, g_final, tm=tm_s).reshape(NB, TP, D)[:, :T]
    ret_p, c_p, n_p, m_p, fk_p, fv_p, flf_p, memk_p, memv_p = [jnp.stack(a, axis=0) for a in outs_p]
    ret_s, c_s, n_s, m_s, fk_s, fv_s, flf_s = [jnp.stack(a, axis=0) for a in outs_s]
    return (y_prompt, y_sample, ret_p, ret_s, c_p, c_s, n_p, n_s, m_p, m_s,
            fk_p, fk_s, fv_p, fv_s, flf_p, flf_s, memk_p, memv_p)
```

```python
import functools
import math

import numpy as np
import jax
import jax.numpy as jnp
from jax import lax
from jax.experimental import pallas as pl
from jax.experimental.pallas import tpu as pltpu

F32 = jnp.float32
BF16 = jnp.bfloat16

HEAD_DIM = 64
HALF = HEAD_DIM // 2
EPS = 1e-6
ROPE_BASE = 10000.0
CHUNK = 128
SUBLANES = 8
LANES = 128
GATE_ROWS = 16
AUG = 2 * HEAD_DIM
SCALE = HEAD_DIM ** -0.5
VMEM_LIMIT = 56 * 1024 * 1024
NEG_INF = float("-inf")


def _cparams(*sem):
    return pltpu.CompilerParams(dimension_semantics=sem, vmem_limit_bytes=VMEM_LIMIT)


def _dot(a, b):
    return jnp.dot(a.astype(BF16), b.astype(BF16), preferred_element_type=F32)


def _dot_nt(a, b):
    return lax.dot_general(a.astype(BF16), b.astype(BF16), (((1,), (1,)), ((), ())),
                           preferred_element_type=F32)


def _dot_tn(a, b):
    return lax.dot_general(a.astype(BF16), b.astype(BF16), (((0,), (0,)), ((), ())),
                           preferred_element_type=F32)


def _split3(x):
    hi = x.astype(BF16)
    r1 = x - hi.astype(F32)
    mid = r1.astype(BF16)
    lo = (r1 - mid.astype(F32)).astype(BF16)
    return hi, mid, lo


def _cumsum_rows(x, n):
    t = lax.broadcasted_iota(jnp.int32, (n, n), 0)
    s = lax.broadcasted_iota(jnp.int32, (n, n), 1)
    tri = jnp.where(t >= s, 1.0, 0.0).astype(BF16)
    return sum(jnp.dot(tri, p, preferred_element_type=F32) for p in _split3(x))


def _cumsum_lanes(x, n):
    t = lax.broadcasted_iota(jnp.int32, (n, n), 0)
    s = lax.broadcasted_iota(jnp.int32, (n, n), 1)
    tri = jnp.where(t <= s, 1.0, 0.0).astype(BF16)
    return sum(jnp.dot(p, tri, preferred_element_type=F32) for p in _split3(x))


def _rms(x, g):
    return x * lax.rsqrt(jnp.mean(x * x, axis=-1, keepdims=True) + EPS) * g


def _sigmoid(x):
    return 1.0 / (1.0 + jnp.exp(-x))


def _log_sigmoid(x):
    return jnp.minimum(x, 0.0) - jnp.log1p(jnp.exp(-jnp.abs(x)))


def _head_norm(o):
    return o * lax.rsqrt(jnp.mean(o * o, axis=-1, keepdims=True) + EPS)


def _softmax_rows(s):
    p = jnp.exp(s - jnp.max(s, axis=1, keepdims=True))
    return p / jnp.sum(p, axis=1, keepdims=True)


def _head_slice(h):
    return slice(h * HEAD_DIM, (h + 1) * HEAD_DIM)


def _norm_matmul_kernel(x_ref, g_ref, w_ref, b_ref, o_ref, xn_ref):
    @pl.when(pl.program_id(1) == 0)
    def _():
        xn_ref[...] = _rms(x_ref[...], g_ref[...]).astype(BF16)

    o_ref[...] = jnp.dot(xn_ref[...], w_ref[...], preferred_element_type=F32) + b_ref[...]


def norm_matmul(x, g, w, b, *, tm, tn):
    M, D = x.shape
    N = w.shape[1]
    return pl.pallas_call(
        _norm_matmul_kernel,
        out_shape=jax.ShapeDtypeStruct((M, N), F32),
        grid=(M // tm, N // tn),
        in_specs=[pl.BlockSpec((tm, D), lambda i, j: (i, 0)),
                  pl.BlockSpec((1, D), lambda i, j: (0, 0)),
                  pl.BlockSpec((D, tn), lambda i, j: (0, j)),
                  pl.BlockSpec((1, tn), lambda i, j: (0, j))],
        out_specs=pl.BlockSpec((tm, tn), lambda i, j: (i, j)),
        scratch_shapes=[pltpu.VMEM((tm, D), BF16)],
        compiler_params=_cparams("parallel", "arbitrary"),
        name="norm_matmul",
    )(x, g.reshape(1, D), w, b.reshape(1, N))


def _in_proj_prompt_kernel(x_ref, g_ref, w_ref, b_ref, wt_ref, bt_ref, z_ref, qt_ref, kt_ref, vt_ref,
                           xn_ref, *, fw):
    @pl.when(pl.program_id(2) == 0)
    def _():
        xn_ref[...] = _rms(x_ref[...], g_ref[...]).astype(BF16)

    z_ref[...] = jnp.dot(xn_ref[...], w_ref[...], preferred_element_type=F32) + b_ref[...]

    @pl.when(pl.program_id(2) == pl.num_programs(2) - 1)
    def _():
        zt = lax.dot_general(wt_ref[...], xn_ref[...], (((1,), (1,)), ((), ())),
                             preferred_element_type=F32) + bt_ref[...]
        qt_ref[...] = zt[:fw]
        kt_ref[...] = zt[fw:2 * fw]
        vt_ref[...] = zt[2 * fw:]


def in_proj_prompt(x, g, w, b, wt, bt, *, n_seq, tm, tn):
    M, D = x.shape
    N = w.shape[1]
    S = M // n_seq
    fw = wt.shape[0] // 3
    nt = S // tm
    t_out = jax.ShapeDtypeStruct((n_seq, fw, S), F32)
    t_spec = pl.BlockSpec((None, fw, tm), lambda s, i, j: (s, 0, i))
    return pl.pallas_call(
        functools.partial(_in_proj_prompt_kernel, fw=fw),
        out_shape=(jax.ShapeDtypeStruct((M, N), F32), t_out, t_out, t_out),
        grid=(n_seq, nt, N // tn),
        in_specs=[pl.BlockSpec((tm, D), lambda s, i, j: (s * nt + i, 0)),
                  pl.BlockSpec((1, D), lambda s, i, j: (0, 0)),
                  pl.BlockSpec((D, tn), lambda s, i, j: (0, j)),
                  pl.BlockSpec((1, tn), lambda s, i, j: (0, j)),
                  pl.BlockSpec((3 * fw, D), lambda s, i, j: (0, 0)),
                  pl.BlockSpec((3 * fw, 1), lambda s, i, j: (0, 0))],
        out_specs=(pl.BlockSpec((tm, tn), lambda s, i, j: (s * nt + i, j)), t_spec, t_spec, t_spec),
        scratch_shapes=[pltpu.VMEM((tm, D), BF16)],
        compiler_params=_cparams("parallel", "parallel", "arbitrary"),
        name="in_proj_prompt",
    )(x, g.reshape(1, D), w, b.reshape(1, N), wt, bt.reshape(3 * fw, 1))


def _norm_matmul_t_kernel(x_ref, g_ref, wt_ref, o_ref):
    xn = _rms(x_ref[...], g_ref[...]).astype(BF16)
    o_ref[...] = lax.dot_general(wt_ref[...], xn, (((1,), (1,)), ((), ())), preferred_element_type=F32)


def norm_matmul_t(x, g, wt, *, n_seq):
    M, D = x.shape
    N = wt.shape[0]
    R = M // n_seq
    return pl.pallas_call(
        _norm_matmul_t_kernel,
        out_shape=jax.ShapeDtypeStruct((n_seq, N, R), F32),
        grid=(n_seq,),
        in_specs=[pl.BlockSpec((R, D), lambda s: (s, 0)),
                  pl.BlockSpec((1, D), lambda s: (0, 0)),
                  pl.BlockSpec((N, D), lambda s: (0, 0))],
        out_specs=pl.BlockSpec((None, N, R), lambda s: (s, 0, 0)),
        compiler_params=_cparams("parallel"),
        name="norm_matmul_t",
    )(x, g.reshape(1, D), wt)


def _gates_kernel(z_ref, o_ref, *, n_raw):
    z = z_ref[...]
    lane = lax.broadcasted_iota(jnp.int32, z.shape, 1)
    o_ref[...] = jnp.where(lane >= n_raw, _log_sigmoid(z), z)


def gate_activations(z, *, col_block, n_raw, tm):
    M = z.shape[0]
    return pl.pallas_call(
        functools.partial(_gates_kernel, n_raw=n_raw),
        out_shape=jax.ShapeDtypeStruct((M, LANES), F32),
        grid=(M // tm,),
        in_specs=[pl.BlockSpec((tm, LANES), lambda i: (i, col_block))],
        out_specs=pl.BlockSpec((tm, LANES), lambda i: (i, 0)),
        compiler_params=_cparams("parallel"),
        name="gate_activations",
    )(z)


def _out_proj_kernel(x_ref, r_ref, m_ref, f_ref, wr_ref, wm_ref, wf_ref, o_ref, *, f_transposed):
    f_term = _dot_tn(f_ref[...], wf_ref[...]) if f_transposed else _dot(f_ref[...], wf_ref[...])
    o_ref[...] = x_ref[...] + _dot(r_ref[...], wr_ref[...]) + _dot(m_ref[...], wm_ref[...]) + f_term


def out_proj(x, r, m, f, w_out, *, tm, n_seq=None):
    M, D = x.shape
    wr, wm = r.shape[1], m.shape[1]
    w_r, w_m, w_f = w_out[:wr], w_out[wr:wr + wm], w_out[wr + wm:]
    row = lambda w: pl.BlockSpec((tm, w), lambda i: (i, 0))
    full = lambda a: pl.BlockSpec(a.shape, lambda i: (0, 0))
    if n_seq is None:
        f_spec = row(f.shape[1])
    else:
        nt = M // n_seq // tm
        f_spec = pl.BlockSpec((None, f.shape[1], tm), lambda i: (i // nt, 0, i % nt))
    return pl.pallas_call(
        functools.partial(_out_proj_kernel, f_transposed=n_seq is not None),
        out_shape=jax.ShapeDtypeStruct((M, D), F32),
        grid=(M // tm,),
        in_specs=[row(D), row(wr), row(wm), f_spec, full(w_r), full(w_m), full(w_f)],
        out_specs=row(D),
        compiler_params=_cparams("parallel"),
        name="out_proj",
    )(x, r, m, f, w_r, w_m, w_f)


def _mlp_kernel(x_ref, g_ref, wu_ref, wd_ref, o_ref, xn_ref, acc_ref):
    j = pl.program_id(1)

    @pl.when(j == 0)
    def _():
        xn_ref[...] = _rms(x_ref[...], g_ref[...]).astype(BF16)
        acc_ref[...] = x_ref[...]

    h = jnp.maximum(jnp.dot(xn_ref[...], wu_ref[...], preferred_element_type=F32), 0.0)
    acc_ref[...] += jnp.dot((h * h).astype(BF16), wd_ref[...], preferred_element_type=F32)

    @pl.when(j == pl.num_programs(1) - 1)
    def _():
        o_ref[...] = acc_ref[...]


def sq_relu_mlp(x, g, w_up, w_down, *, tm, tf):
    M, D = x.shape
    FF = w_up.shape[1]
    return pl.pallas_call(
        _mlp_kernel,
        out_shape=jax.ShapeDtypeStruct((M, D), F32),
        grid=(M // tm, FF // tf),
        in_specs=[pl.BlockSpec((tm, D), lambda i, j: (i, 0)),
                  pl.BlockSpec((1, D), lambda i, j: (0, 0)),
                  pl.BlockSpec((D, tf), lambda i, j: (0, j)),
                  pl.BlockSpec((tf, D), lambda i, j: (j, 0))],
        out_specs=pl.BlockSpec((tm, D), lambda i, j: (i, 0)),
        scratch_shapes=[pltpu.VMEM((tm, D), BF16), pltpu.VMEM((tm, D), F32)],
        compiler_params=_cparams("parallel", "arbitrary"),
        name="sq_relu_mlp",
    )(x, g.reshape(1, D), w_up, w_down)


def _final_norm_kernel(x_ref, g_ref, o_ref):
    o_ref[...] = _rms(x_ref[...], g_ref[...])


def final_norm(x, g, *, tm):
    M, D = x.shape
    return pl.pallas_call(
        _final_norm_kernel,
        out_shape=jax.ShapeDtypeStruct((M, D), F32),
        grid=(M // tm,),
        in_specs=[pl.BlockSpec((tm, D), lambda i: (i, 0)), pl.BlockSpec((1, D), lambda i: (0, 0))],
        out_specs=pl.BlockSpec((tm, D), lambda i: (i, 0)),
        compiler_params=_cparams("parallel"),
        name="final_norm",
    )(x, g.reshape(1, D))


def _retention_kernel(q_ref, k_ref, v_ref, g_ref, cos_ref, sin_ref, s0_ref, gain_ref,
                      y_ref, sout_ref, s_sc, *, L, valid, log_gamma, seqs):
    c = pl.program_id(1)
    n_heads = len(log_gamma)
    W = n_heads * HEAD_DIM

    @pl.when(c == 0)
    def _():
        s_sc[...] = s0_ref[...]

    cos = cos_ref[...]
    sin = sin_ref[...]
    lane = lax.broadcasted_iota(jnp.int32, (L, W), 1)
    first_half = (lane % HEAD_DIM) < HALF

    def rope(x):
        swapped = jnp.where(first_half, pltpu.roll(x, W - HALF, 1), pltpu.roll(x, HALF, 1))
        return x * cos + swapped * sin

    t_i = lax.broadcasted_iota(jnp.int32, (L, L), 0)
    s_i = lax.broadcasted_iota(jnp.int32, (L, L), 1)
    causal = t_i >= s_i
    diff = jnp.where(causal, (t_i - s_i).astype(F32), 0.0)
    tcol = lax.broadcasted_iota(jnp.int32, (L, 1), 0).astype(F32)
    decay = [jnp.where(causal, jnp.exp(lg * diff), 0.0) for lg in log_gamma]
    q_decay = [jnp.exp(lg * (tcol + 1.0)) for lg in log_gamma]
    k_decay = [jnp.where(tcol < valid, jnp.exp(lg * (valid - 1.0 - tcol)), 0.0) for lg in log_gamma]

    items = [(j, h) for j in range(seqs) for h in range(n_heads)]
    qs, ks, vs = {}, {}, {}
    for j in range(seqs):
        q = rope(q_ref[j])
        k = rope(k_ref[j]) * SCALE
        v = v_ref[j]
        for h in range(n_heads):
            qs[j, h], ks[j, h], vs[j, h] = q[:, _head_slice(h)], k[:, _head_slice(h)], v[:, _head_slice(h)]
    scores = {i: _dot_nt(qs[i], ks[i]) for i in items}
    cross = {(j, h): _dot(qs[j, h], s_sc[j, h]) for j, h in items}
    outs = {(j, h): _head_norm(_dot(scores[j, h] * decay[h], vs[j, h]) + cross[j, h] * q_decay[h])
            for j, h in items}
    for j, h in items:
        s_sc[j, h] = math.exp(log_gamma[h] * valid) * s_sc[j, h] + _dot_tn(ks[j, h] * k_decay[h], vs[j, h])
    for j in range(seqs):
        g = g_ref[j]
        y_ref[j] = jnp.concatenate([outs[j, h] for h in range(n_heads)], axis=-1) * gain_ref[...] * (
            g * _sigmoid(g))

    @pl.when(c == pl.num_programs(1) - 1)
    def _():
        sout_ref[...] = s_sc[...]


def retention(z, cos, sin, s0, gain, *, n_seq, L, valid, col0, log_gamma, seqs):
    M, N = z.shape
    n_heads = len(log_gamma)
    W = n_heads * HEAD_DIM
    rows = M // n_seq
    cb = col0 // W
    z3 = z.reshape(n_seq, rows, N)
    zspec = lambda j: pl.BlockSpec((seqs, L, W), lambda b, c: (b, c, cb + j))
    tab = pl.BlockSpec((L, W), lambda b, c: (c, 0))
    st = pl.BlockSpec((seqs, n_heads, HEAD_DIM, HEAD_DIM), lambda b, c: (b, 0, 0, 0))
    y, s_out = pl.pallas_call(
        functools.partial(_retention_kernel, L=L, valid=valid, log_gamma=log_gamma, seqs=seqs),
        out_shape=(jax.ShapeDtypeStruct((n_seq, rows, W), F32),
                   jax.ShapeDtypeStruct((n_seq, n_heads, HEAD_DIM, HEAD_DIM), F32)),
        grid=(n_seq // seqs, rows // L),
        in_specs=[zspec(0), zspec(1), zspec(2), zspec(3), tab, tab, st,
                  pl.BlockSpec((1, W), lambda b, c: (0, 0))],
        out_specs=(pl.BlockSpec((seqs, L, W), lambda b, c: (b, c, 0)), st),
        scratch_shapes=[pltpu.VMEM((seqs, n_heads, HEAD_DIM, HEAD_DIM), F32)],
        compiler_params=_cparams("parallel", "arbitrary"),
        name="retention",
    )(z3, z3, z3, z3, cos, sin, s0, gain.reshape(1, W))
    return y.reshape(M, W), s_out


def _mlstm_kernel(q_ref, k_ref, v_ref, og_ref, gc_ref, gr_ref, c0_ref, n0_ref, m0_ref, gain_ref,
                  y_ref, cout_ref, nout_ref, mout_ref, c_sc, n_sc, m_sc, *, L, valid, n_heads, seqs):
    c = pl.program_id(1)

    @pl.when(c == 0)
    def _():
        c_sc[...] = c0_ref[...]
        n_sc[...] = n0_ref[...]
        m_sc[...] = m0_ref[...]

    t_i = lax.broadcasted_iota(jnp.int32, (L, L), 0)
    s_i = lax.broadcasted_iota(jnp.int32, (L, L), 1)
    causal = t_i >= s_i
    row_valid = lax.broadcasted_iota(jnp.int32, (L, 1), 0) < valid
    ones = jnp.ones((L, HEAD_DIM), F32)
    n_pad = jnp.zeros((SUBLANES - 1, HEAD_DIM), F32)

    items = [(j, h) for j in range(seqs) for h in range(n_heads)]
    qs, ks, vs, ogs, gate = {}, {}, {}, {}, {}
    for j in range(seqs):
        q = q_ref[j]
        k = k_ref[j] * SCALE
        v = v_ref[j]
        og = _sigmoid(og_ref[j])
        gc = gc_ref[j]
        gr = gr_ref[j]
        f_col = _cumsum_rows(gc, L)
        f_row = _cumsum_lanes(gr, L)
        for h in range(n_heads):
            sl = _head_slice(h)
            qs[j, h], ks[j, h], vs[j, h], ogs[j, h] = q[:, sl], k[:, sl], v[:, sl], og[:, sl]
            gate[j, h] = (f_col[:, n_heads + h:n_heads + h + 1], f_row[n_heads + h:n_heads + h + 1, :],
                          gc[:, h:h + 1], gr[h:h + 1, :])

    qk_raw, cq = {}, {}
    for j, h in items:
        qk_raw[j, h] = _dot_nt(qs[j, h], ks[j, h])
        c_aug = jnp.concatenate([c_sc[j, h], n_sc[j, h:h + 1, :], n_pad], axis=0)
        cq[j, h] = _dot_nt(qs[j, h], c_aug)

    m_t, d, w_inter = {}, {}, {}
    for j, h in items:
        fc, fr, ic, ir = gate[j, h]
        m_prev = m_sc[j, :, h:h + 1]
        log_d = jnp.where(causal, fc - fr + ir, NEG_INF)
        inter = fc + m_prev
        m_t[j, h] = jnp.maximum(inter, jnp.max(log_d, axis=1, keepdims=True))
        d[j, h] = jnp.exp(log_d - m_t[j, h])
        w_inter[j, h] = jnp.exp(inter - m_t[j, h])

    outs = {}
    for j, h in items:
        qk = qk_raw[j, h] * d[j, h]
        mix = _dot(qk, jnp.concatenate([vs[j, h], ones], axis=1)) + w_inter[j, h] * jnp.concatenate(
            [cq[j, h], jnp.zeros((L, HEAD_DIM - SUBLANES), F32)], axis=1)
        num = mix[:, :HEAD_DIM]
        den = mix[:, HEAD_DIM:HEAD_DIM + 1]
        hh = num / jnp.maximum(jnp.abs(den), jnp.exp(-m_t[j, h]))
        outs[j, h] = _head_norm(ogs[j, h] * hh)

    for j, h in items:
        fc, _, ic, _ = gate[j, h]
        m_prev = m_sc[j, :, h:h + 1]
        m_last = m_t[j, h][valid - 1:valid, :]
        f_last = fc[valid - 1:valid, :]
        w_s = jnp.where(row_valid, jnp.exp(f_last - fc + ic - m_last), 0.0)
        w_c = jnp.exp(f_last + m_prev - m_last)
        c_sc[j, h] = w_c * c_sc[j, h] + _dot_tn(vs[j, h] * w_s, ks[j, h])
        n_sc[j, h:h + 1, :] = w_c * n_sc[j, h:h + 1, :] + jnp.sum(ks[j, h] * w_s, axis=0, keepdims=True)
        m_sc[j, :, h:h + 1] = m_last
    for j in range(seqs):
        y_ref[j] = jnp.concatenate([outs[j, h] for h in range(n_heads)], axis=-1) * gain_ref[...]

    @pl.when(c == pl.num_programs(1) - 1)
    def _():
        cout_ref[...] = c_sc[...]
        nout_ref[...] = n_sc[...]
        mout_ref[...] = m_sc[...]


def mlstm(z, gc, gr, c0, n0, m0, gain, *, n_seq, L, valid, col0, n_heads, seqs):
    M, N = z.shape
    W = n_heads * HEAD_DIM
    rows = M // n_seq
    cb = col0 // W
    z3 = z.reshape(n_seq, rows, N)
    zspec = lambda j: pl.BlockSpec((seqs, L, W), lambda b, c: (b, c, cb + j))
    cst = pl.BlockSpec((seqs, n_heads, HEAD_DIM, HEAD_DIM), lambda b, c: (b, 0, 0, 0))
    nst = pl.BlockSpec((seqs, n_heads, HEAD_DIM), lambda b, c: (b, 0, 0))
    mst = pl.BlockSpec((seqs, 1, n_heads), lambda b, c: (b, 0, 0))
    y, c_out, n_out, m_out = pl.pallas_call(
        functools.partial(_mlstm_kernel, L=L, valid=valid, n_heads=n_heads, seqs=seqs),
        out_shape=(jax.ShapeDtypeStruct((n_seq, rows, W), F32),
                   jax.ShapeDtypeStruct((n_seq, n_heads, HEAD_DIM, HEAD_DIM), F32),
                   jax.ShapeDtypeStruct((n_seq, n_heads, HEAD_DIM), F32),
                   jax.ShapeDtypeStruct((n_seq, 1, n_heads), F32)),
        grid=(n_seq // seqs, rows // L),
        in_specs=[zspec(0), zspec(1), zspec(2), zspec(3),
                  pl.BlockSpec((seqs, L, LANES), lambda b, c: (b, c, 0)),
                  pl.BlockSpec((seqs, GATE_ROWS, L), lambda b, c: (b, 0, c)),
                  cst, nst, mst,
                  pl.BlockSpec((1, W), lambda b, c: (0, 0))],
        out_specs=(pl.BlockSpec((seqs, L, W), lambda b, c: (b, c, 0)), cst, nst, mst),
        scratch_shapes=[pltpu.VMEM((seqs, n_heads, HEAD_DIM, HEAD_DIM), F32),
                        pltpu.VMEM((seqs, n_heads, HEAD_DIM), F32),
                        pltpu.VMEM((seqs, 1, n_heads), F32)],
        compiler_params=_cparams("parallel", "arbitrary"),
        name="mlstm",
    )(z3, z3, z3, z3, gc.reshape(n_seq, rows, LANES), gr, c0, n0, m0.reshape(n_seq, 1, n_heads),
      gain.reshape(1, W))
    return y.reshape(M, W), c_out, n_out, m_out.reshape(n_seq, n_heads)


def _fox_prep_kernel(gc_ref, gr_ref, k_ref, qt_ref, vt_ref, ka_ref, qa_ref, vb_ref, cc_sc, cr_sc,
                     *, L, n_heads, g0):
    @pl.when(pl.program_id(1) == 0)
    def _():
        cc_sc[...] = jnp.zeros_like(cc_sc)
        cr_sc[...] = jnp.zeros_like(cr_sc)

    fc = _cumsum_rows(gc_ref[...], L) + cc_sc[...]
    fr = _cumsum_lanes(gr_ref[...], L) + cr_sc[...]
    cc_sc[...] = fc[L - 1:L, :]
    cr_sc[...] = fr[:, L - 1:L]
    k = k_ref[...]
    qt = qt_ref[...] * SCALE
    lane = lax.broadcasted_iota(jnp.int32, (L, HEAD_DIM), 1)
    row = lax.broadcasted_iota(jnp.int32, (HEAD_DIM, L), 0)
    for h in range(n_heads):
        sl = _head_slice(h)
        kh, km, kl = (p.astype(F32) for p in _split3(fc[:, g0 + h:g0 + h + 1]))
        k_aug = jnp.where(lane < 3, 1.0,
                          jnp.where(lane == 3, -kh, jnp.where(lane == 4, -km, jnp.where(lane == 5, -kl, 0.0))))
        ka_ref[:, h * AUG:(h + 1) * AUG] = jnp.concatenate([k[:, sl], k_aug], axis=1).astype(BF16)
        qh, qm, ql = (p.astype(F32) for p in _split3(fr[g0 + h:g0 + h + 1, :]))
        q_aug = jnp.where(row == 0, qh,
                          jnp.where(row == 1, qm, jnp.where(row == 2, ql, jnp.where(row < 6, 1.0, 0.0))))
        qa_ref[h * AUG:(h + 1) * AUG, :] = jnp.concatenate([qt[sl, :], q_aug], axis=0).astype(BF16)
    vb_ref[...] = vt_ref[...].astype(BF16)


def fox_prep(gc, gr, z, qt, vt, *, n_seq, L, k_col0, n_heads, g0):
    M = gc.shape[0]
    S = M // n_seq
    nc = S // L
    W = n_heads * HEAD_DIM
    t_spec = pl.BlockSpec((None, W, L), lambda b, c: (b, 0, c))
    return pl.pallas_call(
        functools.partial(_fox_prep_kernel, L=L, n_heads=n_heads, g0=g0),
        out_shape=(jax.ShapeDtypeStruct((M, n_heads * AUG), BF16),
                   jax.ShapeDtypeStruct((n_seq, n_heads * AUG, S), BF16),
                   jax.ShapeDtypeStruct((n_seq, W, S), BF16)),
        grid=(n_seq, nc),
        in_specs=[pl.BlockSpec((L, LANES), lambda b, c: (b * nc + c, 0)),
                  pl.BlockSpec((None, GATE_ROWS, L), lambda b, c: (b, 0, c)),
                  pl.BlockSpec((L, W), lambda b, c: (b * nc + c, k_col0 // W)),
                  t_spec, t_spec],
        out_specs=(pl.BlockSpec((L, n_heads * AUG), lambda b, c: (b * nc + c, 0)),
                   pl.BlockSpec((None, n_heads * AUG, L), lambda b, c: (b, 0, c)),
                   t_spec),
        scratch_shapes=[pltpu.VMEM((1, LANES), F32), pltpu.VMEM((GATE_ROWS, 1), F32)],
        compiler_params=_cparams("parallel", "arbitrary"),
        name="fox_prep",
    )(gc, gr, z, qt, vt)


def _fox_flash_kernel(qi_ref, kj_ref, ka_ref, qa_ref, vt_ref, o_ref, m_sc, l_sc, acc_sc, s_sc, p_sc,
                      *, tq, n_heads):
    step = pl.program_id(1)
    qi = qi_ref[step]
    kj = kj_ref[step]

    @pl.when(kj == 0)
    def _():
        m_sc[...] = jnp.full_like(m_sc, NEG_INF)
        l_sc[...] = jnp.zeros_like(l_sc)
        acc_sc[...] = jnp.zeros_like(acc_sc)

    def update(diagonal):
        if diagonal:
            key = lax.broadcasted_iota(jnp.int32, (tq, tq), 0)
            qry = lax.broadcasted_iota(jnp.int32, (tq, tq), 1)
            visible = key <= qry
        block_max = []
        for h in range(n_heads):
            s = jnp.dot(ka_ref[:, h * AUG:(h + 1) * AUG], qa_ref[h * AUG:(h + 1) * AUG, :],
                        preferred_element_type=F32)
            if diagonal:
                s = jnp.where(visible, s, NEG_INF)
            s_sc[h] = s
            block_max.append(jnp.max(s, axis=0, keepdims=True))
        m_old = m_sc[...]
        m_new = jnp.maximum(m_old, jnp.concatenate(block_max, axis=0))
        alpha = jnp.exp(m_old - m_new)
        m_sc[...] = m_new
        sums = []
        for h in range(n_heads):
            p = jnp.exp(s_sc[h] - m_new[h:h + 1, :])
            sums.append(jnp.sum(p, axis=0, keepdims=True))
            p_sc[h] = p.astype(BF16)
        l_sc[...] = alpha * l_sc[...] + jnp.concatenate(sums, axis=0)
        for h in range(n_heads):
            hs = _head_slice(h)
            acc_sc[hs, :] = alpha[h:h + 1, :] * acc_sc[hs, :] + jnp.dot(
                vt_ref[hs, :], p_sc[h], preferred_element_type=F32)

    @pl.when(kj < qi)
    def _():
        update(False)

    @pl.when(kj == qi)
    def _():
        update(True)
        o_ref[...] = jnp.concatenate(
            [acc_sc[_head_slice(h), :] / l_sc[h:h + 1, :] for h in range(n_heads)], axis=0)


def fox_flash(ka, qa, vt, *, tq, n_heads):
    n_seq, W, S = vt.shape
    nq = S // tq
    pairs = [(i, j) for i in range(nq) for j in range(i + 1)]
    qi_tab = jnp.asarray([p[0] for p in pairs], jnp.int32)
    kj_tab = jnp.asarray([p[1] for p in pairs], jnp.int32)
    grid_spec = pltpu.PrefetchScalarGridSpec(
        num_scalar_prefetch=2,
        grid=(n_seq, len(pairs)),
        in_specs=[pl.BlockSpec((tq, n_heads * AUG), lambda b, s, qi, kj: (b * nq + kj[s], 0)),
                  pl.BlockSpec((None, n_heads * AUG, tq), lambda b, s, qi, kj: (b, 0, qi[s])),
                  pl.BlockSpec((None, W, tq), lambda b, s, qi, kj: (b, 0, kj[s]))],
        out_specs=pl.BlockSpec((None, W, tq), lambda b, s, qi, kj: (b, 0, qi[s])),
        scratch_shapes=[pltpu.VMEM((n_heads, tq), F32), pltpu.VMEM((n_heads, tq), F32),
                        pltpu.VMEM((W, tq), F32),
                        pltpu.VMEM((n_heads, tq, tq), F32), pltpu.VMEM((n_heads, tq, tq), BF16)],
    )
    return pl.pallas_call(
        functools.partial(_fox_flash_kernel, tq=tq, n_heads=n_heads),
        out_shape=jax.ShapeDtypeStruct((n_seq, W, S), F32),
        grid_spec=grid_spec,
        compiler_params=_cparams("parallel", "arbitrary"),
        name="fox_flash",
    )(qi_tab, kj_tab, ka, qa, vt)


def _fox_decode_kernel(pt_ref, q_ref, kn_ref, vn_ref, gc_ref, gr_ref, *rest, valid, n_heads, g0, n_pages):
    k_refs = rest[:n_pages]
    v_refs = rest[n_pages:2 * n_pages]
    lf_refs = rest[2 * n_pages:3 * n_pages]
    o_ref = rest[3 * n_pages]
    del pt_ref
    T = SUBLANES
    H = n_heads
    page = lf_refs[0].shape[-1]

    q = q_ref[...] * SCALE
    kn = kn_ref[...]
    vn = vn_ref[...]
    gc = gc_ref[...]
    gr = gr_ref[...]
    t_i = lax.broadcasted_iota(jnp.int32, (T, LANES), 0)
    cum_col = sum(jnp.where(t_i >= u, gc[u:u + 1, :], 0.0) for u in range(valid))
    u_i = lax.broadcasted_iota(jnp.int32, (GATE_ROWS, T), 1)
    cum_row = sum(jnp.where(u_i >= u, gr[:, u:u + 1], 0.0) for u in range(valid))
    t_q = lax.broadcasted_iota(jnp.int32, (T, T), 0)
    u_k = lax.broadcasted_iota(jnp.int32, (T, T), 1)
    new_visible = (u_k <= t_q) & (u_k < valid)

    s_a = lax.broadcasted_iota(jnp.int32, (page, page), 0)
    s_b = lax.broadcasted_iota(jnp.int32, (page, page), 1)
    later = jnp.where(s_a > s_b, 1.0, 0.0).astype(BF16)
    lf = jnp.concatenate([r[...] for r in lf_refs], axis=0)
    within = sum(jnp.dot(p, later, preferred_element_type=F32) for p in _split3(lf))
    totals = within[:, 0:1] + lf[:, 0:1]
    carry = jnp.zeros((H, 1), F32)
    suffix = [None] * n_pages
    for p in reversed(range(n_pages)):
        suffix[p] = within[p * H:(p + 1) * H, :] + carry
        carry = carry + totals[p * H:(p + 1) * H, :]

    s_past, s_new = [], []
    for h in range(H):
        qh = q[:, _head_slice(h)]
        f_t = cum_col[:, g0 + h:g0 + h + 1]
        kt = jnp.concatenate([r[h] for r in k_refs], axis=1)
        bias = jnp.concatenate([sp[h:h + 1, :] for sp in suffix], axis=1)
        s_past.append(_dot(qh, kt) + (f_t + bias))
        sn = _dot_nt(qh, kn[:, _head_slice(h)]) + (f_t - cum_row[g0 + h:g0 + h + 1, :])
        s_new.append(jnp.where(new_visible, sn, NEG_INF))
    p_past, p_new, denom = [], [], []
    for h in range(H):
        m = jnp.maximum(jnp.max(s_past[h], axis=1, keepdims=True), jnp.max(s_new[h], axis=1, keepdims=True))
        p_past.append(jnp.exp(s_past[h] - m))
        p_new.append(jnp.exp(s_new[h] - m))
        denom.append(jnp.sum(p_past[h], axis=1, keepdims=True) + jnp.sum(p_new[h], axis=1, keepdims=True))
    outs = []
    for h in range(H):
        vt = jnp.concatenate([r[h] for r in v_refs], axis=1)
        outs.append((_dot_nt(p_past[h], vt) + _dot(p_new[h], vn[:, _head_slice(h)])) / denom[h])
    o_ref[...] = jnp.concatenate(outs, axis=1)


def fox_decode(z, gc, gr, cache_kt, cache_vt, cache_lf_t, page_table, *, layer, valid, col0, n_heads, g0):
    M = z.shape[0]
    T = SUBLANES
    n_seq = M // T
    W = n_heads * HEAD_DIM
    n_pages = page_table.shape[1]
    page = cache_kt.shape[-1]
    cb = col0 // W

    def page_idx(i, nd):
        return lambda b, pt: (layer, pt[b * n_pages + i]) + (0,) * nd

    kv_specs = [pl.BlockSpec((None, None, n_heads, HEAD_DIM, page), page_idx(i, 3)) for i in range(n_pages)]
    lf_specs = [pl.BlockSpec((None, None, n_heads, page), page_idx(i, 2)) for i in range(n_pages)]
    zspec = lambda c: pl.BlockSpec((T, W), lambda b, pt: (b, c))
    grid_spec = pltpu.PrefetchScalarGridSpec(
        num_scalar_prefetch=1,
        grid=(n_seq,),
        in_specs=[zspec(cb), zspec(cb + 1), zspec(cb + 2),
                  pl.BlockSpec((T, LANES), lambda b, pt: (b, 0)),
                  pl.BlockSpec((None, GATE_ROWS, T), lambda b, pt: (b, 0, 0))] + kv_specs + kv_specs + lf_specs,
        out_specs=pl.BlockSpec((T, W), lambda b, pt: (b, 0)),
    )
    return pl.pallas_call(
        functools.partial(_fox_decode_kernel, valid=valid, n_heads=n_heads, g0=g0, n_pages=n_pages),
        out_shape=jax.ShapeDtypeStruct((M, W), F32),
        grid_spec=grid_spec,
        compiler_params=_cparams("parallel"),
        name="fox_decode",
    )(page_table.reshape(-1), z, z, z, gc, gr,
      *([cache_kt] * n_pages), *([cache_vt] * n_pages), *([cache_lf_t] * n_pages))


def _cross_prompt_kernel(x_ref, g_ref, wq_ref, kt_ref, vt_ref, wo_ref, o_ref, *, n_heads):
    x = x_ref[...]
    q = jnp.dot(_rms(x, g_ref[...]).astype(BF16), wq_ref[...], preferred_element_type=F32) * SCALE
    scores = [_dot(q[:, _head_slice(h)], kt_ref[_head_slice(h), :]) for h in range(n_heads)]
    probs = [_softmax_rows(s) for s in scores]
    outs = [_dot_nt(probs[h], vt_ref[_head_slice(h), :]) for h in range(n_heads)]
    o_ref[...] = x + jnp.dot(jnp.concatenate(outs, axis=-1).astype(BF16), wo_ref[...],
                             preferred_element_type=F32)


def cross_attn_prompt(x, g, w_q, mkv_t, w_o, *, n_heads, tm):
    M, D = x.shape
    W = n_heads * HEAD_DIM
    n_seq, _, mem_len = mkv_t.shape
    per_seq = M // n_seq // tm
    return pl.pallas_call(
        functools.partial(_cross_prompt_kernel, n_heads=n_heads),
        out_shape=jax.ShapeDtypeStruct((M, D), F32),
        grid=(M // tm,),
        in_specs=[pl.BlockSpec((tm, D), lambda i: (i, 0)),
                  pl.BlockSpec((1, D), lambda i: (0, 0)),
                  pl.BlockSpec((D, W), lambda i: (0, 0)),
                  pl.BlockSpec((None, W, mem_len), lambda i: (i // per_seq, 0, 0)),
                  pl.BlockSpec((None, W, mem_len), lambda i: (i // per_seq, 1, 0)),
                  pl.BlockSpec((W, D), lambda i: (0, 0))],
        out_specs=pl.BlockSpec((tm, D), lambda i: (i, 0)),
        compiler_params=_cparams("parallel"),
        name="cross_attn_prompt",
    )(x, g.reshape(1, D), w_q, mkv_t, mkv_t, w_o)


def _cross_sample_kernel(x_ref, g_ref, wq_ref, kt_ref, vt_ref, wo_ref, o_ref, *, n_heads, seqs):
    T = SUBLANES
    x = x_ref[...]
    q = jnp.dot(_rms(x, g_ref[...]).astype(BF16), wq_ref[...], preferred_element_type=F32) * SCALE
    items = [(b, h) for b in range(seqs) for h in range(n_heads)]
    scores = {(b, h): _dot(q[b * T:(b + 1) * T, _head_slice(h)], kt_ref[b, h]) for b, h in items}
    probs = {i: _softmax_rows(scores[i]) for i in items}
    outs = {(b, h): _dot_nt(probs[b, h], vt_ref[b, h]) for b, h in items}
    rows = [jnp.concatenate([outs[b, h] for h in range(n_heads)], axis=1) for b in range(seqs)]
    o_ref[...] = x + jnp.dot(jnp.concatenate(rows, axis=0).astype(BF16), wo_ref[...],
                             preferred_element_type=F32)


def cross_attn_sample(x, g, w_q, cache_kt, cache_vt, w_o, *, layer, n_heads, seqs):
    M, D = x.shape
    T = SUBLANES
    W = n_heads * HEAD_DIM
    mem_len = cache_kt.shape[-1]
    kv = pl.BlockSpec((None, seqs, n_heads, HEAD_DIM, mem_len), lambda i: (layer, i, 0, 0, 0))
    return pl.pallas_call(
        functools.partial(_cross_sample_kernel, n_heads=n_heads, seqs=seqs),
        out_shape=jax.ShapeDtypeStruct((M, D), F32),
        grid=(M // (seqs * T),),
        in_specs=[pl.BlockSpec((seqs * T, D), lambda i: (i, 0)),
                  pl.BlockSpec((1, D), lambda i: (0, 0)),
                  pl.BlockSpec((D, W), lambda i: (0, 0)),
                  kv, kv,
                  pl.BlockSpec((W, D), lambda i: (0, 0))],
        out_specs=pl.BlockSpec((seqs * T, D), lambda i: (i, 0)),
        compiler_params=_cparams("parallel"),
        name="cross_attn_sample",
    )(x, g.reshape(1, D), w_q, cache_kt, cache_vt, w_o)


def _rope_tables(pos, n_heads):
    inv = ROPE_BASE ** (-jnp.arange(HALF, dtype=F32) / HALF)
    ang = pos.astype(F32)[:, None] * inv[None, :]
    cos, sin = jnp.cos(ang), jnp.sin(ang)
    return (jnp.tile(jnp.concatenate([cos, cos], axis=-1), (1, n_heads)),
            jnp.tile(jnp.concatenate([-sin, sin], axis=-1), (1, n_heads)))


def _pick(n, candidates):
    for c in candidates:
        if n % c == 0:
            return c
    return n


def kernel(x_prompt, x_sample, mem_prompt, state_ret, state_mlstm_c, state_mlstm_n, state_mlstm_m,
           cache_fox_k, cache_fox_v, cache_fox_logf, cache_mem_k, cache_mem_v, page_table,
           g_mix, w_in, b_in, g_ret, g_mlstm, w_out, g_cross, g_mem, w_cq, w_ckv, w_co,
           g_mlp, w_up, w_down, g_final):
    B, S, D = x_prompt.shape
    NB, T, _ = x_sample.shape
    depth = w_in.shape[0]
    RH = state_ret.shape[2]
    MH = state_mlstm_c.shape[2]
    FH = cache_fox_k.shape[3]
    MEMH = cache_mem_k.shape[3]
    RW, MW, FW, MEMW = RH * HEAD_DIM, MH * HEAD_DIM, FH * HEAD_DIM, MEMH * HEAD_DIM
    n_pages = page_table.shape[1]
    page = cache_fox_k.shape[2]
    mem_len = mem_prompt.shape[1]
    TP = SUBLANES
    assert T <= TP and S % CHUNK == 0 and 2 * MH + FH <= GATE_ROWS
    assert RW % LANES == 0 and MW % LANES == 0 and FW % LANES == 0

    sizes = (RW,) * 4 + (MW,) * 4 + (MH, MH) + (FW,) * 3 + (FH,)
    starts = np.concatenate([[0], np.cumsum(sizes)[:-1]])
    cols = lambda a, *ids: jnp.concatenate([a[..., starts[i]:starts[i] + sizes[i]] for i in ids], axis=-1)
    ret_col0, mls_col0 = 0, 4 * RW
    fox_col0 = mls_col0 + 4 * MW
    assert fox_col0 % FW == 0 and mls_col0 % MW == 0
    fox_g0 = 2 * MH

    def regroup(ids, gate_col0):
        ids = ids + (8, 9, 13)
        n_cols = gate_col0 + 2 * LANES
        pad = n_cols - sum(sizes[i] for i in ids)
        return (jnp.pad(cols(w_in, *ids), ((0, 0), (0, 0), (0, pad))).astype(BF16),
                jnp.pad(cols(b_in, *ids), ((0, 0), (0, pad))), n_cols)

    gate_col0_s = fox_col0 + 3 * FW
    w_in_s, b_in_s, n_cols_s = regroup((0, 1, 2, 3, 4, 5, 6, 7, 10, 11, 12), gate_col0_s)
    gate_col0_p = fox_col0 + FW
    w_in_p, b_in_p, n_cols_p = regroup((0, 1, 2, 3, 4, 5, 6, 7, 11), gate_col0_p)
    w_fox_t = jnp.swapaxes(cols(w_in, 10, 11, 12), 1, 2).astype(BF16)
    b_fox = cols(b_in, 10, 11, 12)

    w_out_b, w_cq_b, w_co_b = (w.astype(BF16) for w in (w_out, w_cq, w_co))
    w_ckv_t = jnp.swapaxes(w_ckv, 1, 2).astype(BF16)
    w_up_b, w_down_b = w_up.astype(BF16), w_down.astype(BF16)

    past = n_pages * page
    cos_p, sin_p = _rope_tables(jnp.arange(S, dtype=jnp.int32), RH)
    pos_s = jnp.minimum(past + jnp.arange(TP, dtype=jnp.int32), past + T - 1)
    cos_s, sin_s = _rope_tables(pos_s, RH)
    log_gamma = tuple(float(np.log1p(-np.exp2(-5.0 - h))) for h in range(RH))

    cache_kt = jnp.transpose(cache_fox_k, (0, 1, 3, 4, 2))
    cache_vt = jnp.transpose(cache_fox_v, (0, 1, 3, 4, 2))
    cache_lf_t = jnp.swapaxes(cache_fox_logf, 2, 3)
    mem_kt = jnp.transpose(cache_mem_k, (0, 1, 3, 4, 2))
    mem_vt = jnp.transpose(cache_mem_v, (0, 1, 3, 4, 2))

    Mp, Ms = B * S, NB * TP
    xp = x_prompt.reshape(Mp, D)
    xs = jnp.pad(x_sample, ((0, 0), (0, TP - T), (0, 0))).reshape(Ms, D)
    mem = mem_prompt.reshape(B * mem_len, D)

    tm_p = _pick(S, (1024, 512, 256, 128))
    tm_in = _pick(S, (512, 256, 128))
    tm_s = _pick(Ms, (1024, 512, 256, 128, 64, 32, 16, 8))
    tn_s = _pick(n_cols_s, (1280, 768, 640, 512, 256, 128))
    tn_p = n_cols_p
    tf = _pick(w_up.shape[2], (512, 256, 128))
    tq = _pick(S, (512, 256, 128))
    l_prep = _pick(S, (512, 256, 128))
    seqs = _pick(NB, (8, 4, 2, 1))
    seqs_p = _pick(B, (2, 1))
    zeros = lambda *s: jnp.zeros(s, F32)

    def gates_of(z, gate_col0, n_seq, rows, tm):
        gc = gate_activations(z, col_block=gate_col0 // LANES, n_raw=MH, tm=tm)
        gr = jnp.swapaxes(gc[:, :GATE_ROWS].reshape(n_seq, rows, GATE_ROWS), 1, 2)
        return gc, gr

    outs_p = [[] for _ in range(9)]
    outs_s = [[] for _ in range(7)]
    for l in range(depth):
        zp, qt_p, kt_p, vt_p = in_proj_prompt(xp, g_mix[l], w_in_p[l], b_in_p[l], w_fox_t[l], b_fox[l],
                                              n_seq=B, tm=tm_in, tn=tn_p)
        gcp, grp = gates_of(zp, gate_col0_p, B, S, tm_p)
        r_p, S_p = retention(zp, cos_p, sin_p, zeros(B, RH, HEAD_DIM, HEAD_DIM), g_ret[l],
                             n_seq=B, L=CHUNK, valid=CHUNK, col0=ret_col0, log_gamma=log_gamma,
                             seqs=seqs_p)
        m_p, C_p, n_p, mm_p = mlstm(zp, gcp, grp, zeros(B, MH, HEAD_DIM, HEAD_DIM), zeros(B, MH, HEAD_DIM),
                                    zeros(B, MH), g_mlstm[l], n_seq=B, L=CHUNK, valid=CHUNK,
                                    col0=mls_col0, n_heads=MH, seqs=1)
        ka, qa, vb = fox_prep(gcp, grp, zp, qt_p, vt_p, n_seq=B, L=l_prep, k_col0=fox_col0,
                              n_heads=FH, g0=fox_g0)
        f_p = fox_flash(ka, qa, vb, tq=tq, n_heads=FH)
        xp = out_proj(xp, r_p, m_p, f_p, w_out_b[l], tm=tm_p, n_seq=B)
        mkv_t = norm_matmul_t(mem, g_mem[l], w_ckv_t[l], n_seq=B)
        xp = cross_attn_prompt(xp, g_cross[l], w_cq_b[l], mkv_t, w_co_b[l], n_heads=MEMH,
                               tm=_pick(S, (512, 256, 128)))
        xp = sq_relu_mlp(xp, g_mlp[l], w_up_b[l], w_down_b[l], tm=tm_p, tf=tf)

        zs = norm_matmul(xs, g_mix[l], w_in_s[l], b_in_s[l], tm=tm_s, tn=tn_s)
        gcs, grs = gates_of(zs, gate_col0_s, NB, TP, tm_s)
        r_s, S_s = retention(zs, cos_s, sin_s, state_ret[l], g_ret[l], n_seq=NB, L=TP, valid=T,
                             col0=ret_col0, log_gamma=log_gamma, seqs=seqs)
        m_s, C_s, n_s, mm_s = mlstm(zs, gcs, grs, state_mlstm_c[l], state_mlstm_n[l], state_mlstm_m[l],
                                    g_mlstm[l], n_seq=NB, L=TP, valid=T, col0=mls_col0, n_heads=MH,
                                    seqs=seqs)
        f_s = fox_decode(zs, gcs, grs, cache_kt, cache_vt, cache_lf_t, page_table, layer=l, valid=T,
                         col0=fox_col0, n_heads=FH, g0=fox_g0)
        xs = out_proj(xs, r_s, m_s, f_s, w_out_b[l], tm=tm_s)
        xs = cross_attn_sample(xs, g_cross[l], w_cq_b[l], mem_kt, mem_vt, w_co_b[l], layer=l,
                               n_heads=MEMH, seqs=seqs)
        xs = sq_relu_mlp(xs, g_mlp[l], w_up_b[l], w_down_b[l], tm=tm_s, tf=tf)

        heads_last = lambda a, n: jnp.transpose(a.reshape(a.shape[0], n, HEAD_DIM, a.shape[-1]), (0, 3, 1, 2))
        fks = lambda c: zs[:, fox_col0 + c * FW:fox_col0 + (c + 1) * FW].reshape(NB, TP, FH, HEAD_DIM)[:, :T]
        new_p = (S_p, C_p, n_p, mm_p,
                 heads_last(kt_p, FH), heads_last(vt_p, FH),
                 gcp[:, fox_g0:fox_g0 + FH].reshape(B, S, FH),
                 heads_last(mkv_t[:, :MEMW], MEMH), heads_last(mkv_t[:, MEMW:], MEMH))
        new_s = (S_s, C_s, n_s, mm_s, fks(1), fks(2),
                 gcs[:, fox_g0:fox_g0 + FH].reshape(NB, TP, FH)[:, :T])
        for lst, a in zip(outs_p, new_p):
            lst.append(a)
        for lst, a in zip(outs_s, new_s):
            lst.append(a)

    y_prompt = final_norm(xp, g_final, tm=tm_p).reshape(B, S, D)
    y_sample = final_norm(xs, g_final, tm=tm_s).reshape(NB, TP, D)[:, :T]
    ret_p, c_p, n_p, m_p, fk_p, fv_p, flf_p, memk_p, memv_p = [jnp.stack(a, axis=0) for a in outs_p]
    ret_s, c_s, n_s, m_s, fk_s, fv_s, flf_s = [jnp.stack(a, axis=0) for a in outs_s]
    return (y_prompt, y_sample, ret_p, ret_s, c_p, c_s, n_p, n_s, m_p, m_s,
            fk_p, fk_s, fv_p, fv_s, flf_p, flf_s, memk_p, memv_p)
```

```python
import functools
import math

import numpy as np
import jax
import jax.numpy as jnp
from jax import lax
from jax.experimental import pallas as pl
from jax.experimental.pallas import tpu as pltpu

F32 = jnp.float32
BF16 = jnp.bfloat16

HEAD_DIM = 64
HALF = HEAD_DIM // 2
EPS = 1e-6
ROPE_BASE = 10000.0
CHUNK = 128
SUBLANES = 8
LANES = 128
GATE_ROWS = 16
AUG = 2 * HEAD_DIM
SCALE = HEAD_DIM ** -0.5
VMEM_LIMIT = 56 * 1024 * 1024
NEG_INF = float("-inf")


def _cparams(*sem):
    return pltpu.CompilerParams(dimension_semantics=sem, vmem_limit_bytes=VMEM_LIMIT)


def _dot(a, b):
    return jnp.dot(a.astype(BF16), b.astype(BF16), preferred_element_type=F32)


def _dot_nt(a, b):
    return lax.dot_general(a.astype(BF16), b.astype(BF16), (((1,), (1,)), ((), ())),
                           preferred_element_type=F32)


def _dot_tn(a, b):
    return lax.dot_general(a.astype(BF16), b.astype(BF16), (((0,), (0,)), ((), ())),
                           preferred_element_type=F32)


def _split3(x):
    hi = x.astype(BF16)
    r1 = x - hi.astype(F32)
    mid = r1.astype(BF16)
    lo = (r1 - mid.astype(F32)).astype(BF16)
    return hi, mid, lo


def _cumsum_rows(x, n):
    t = lax.broadcasted_iota(jnp.int32, (n, n), 0)
    s = lax.broadcasted_iota(jnp.int32, (n, n), 1)
    tri = jnp.where(t >= s, 1.0, 0.0).astype(BF16)
    return sum(jnp.dot(tri, p, preferred_element_type=F32) for p in _split3(x))


def _cumsum_lanes(x, n):
    t = lax.broadcasted_iota(jnp.int32, (n, n), 0)
    s = lax.broadcasted_iota(jnp.int32, (n, n), 1)
    tri = jnp.where(t <= s, 1.0, 0.0).astype(BF16)
    return sum(jnp.dot(p, tri, preferred_element_type=F32) for p in _split3(x))


def _rms(x, g):
    return x * lax.rsqrt(jnp.mean(x * x, axis=-1, keepdims=True) + EPS) * g


def _sigmoid(x):
    return 1.0 / (1.0 + jnp.exp(-x))


def _log_sigmoid(x):
    return jnp.minimum(x, 0.0) - jnp.log1p(jnp.exp(-jnp.abs(x)))


def _head_norm(o):
    return o * lax.rsqrt(jnp.mean(o * o, axis=-1, keepdims=True) + EPS)


def _softmax_rows(s):
    p = jnp.exp(s - jnp.max(s, axis=1, keepdims=True))
    return p / jnp.sum(p, axis=1, keepdims=True)


def _head_slice(h):
    return slice(h * HEAD_DIM, (h + 1) * HEAD_DIM)


def _norm_matmul_kernel(x_ref, g_ref, w_ref, b_ref, o_ref, xn_ref):
    @pl.when(pl.program_id(1) == 0)
    def _():
        xn_ref[...] = _rms(x_ref[...], g_ref[...]).astype(BF16)

    o_ref[...] = jnp.dot(xn_ref[...], w_ref[...], preferred_element_type=F32) + b_ref[...]


def norm_matmul(x, g, w, b, *, tm, tn):
    M, D = x.shape
    N = w.shape[1]
    return pl.pallas_call(
        _norm_matmul_kernel,
        out_shape=jax.ShapeDtypeStruct((M, N), F32),
        grid=(M // tm, N // tn),
        in_specs=[pl.BlockSpec((tm, D), lambda i, j: (i, 0)),
                  pl.BlockSpec((1, D), lambda i, j: (0, 0)),
                  pl.BlockSpec((D, tn), lambda i, j: (0, j)),
                  pl.BlockSpec((1, tn), lambda i, j: (0, j))],
        out_specs=pl.BlockSpec((tm, tn), lambda i, j: (i, j)),
        scratch_shapes=[pltpu.VMEM((tm, D), BF16)],
        compiler_params=_cparams("parallel", "arbitrary"),
        name="norm_matmul",
    )(x, g.reshape(1, D), w, b.reshape(1, N))


def _in_proj_prompt_kernel(x_ref, g_ref, w_ref, b_ref, wt_ref, bt_ref, z_ref, qt_ref, kt_ref, vt_ref,
                           xn_ref, *, fw):
    @pl.when(pl.program_id(2) == 0)
    def _():
        xn_ref[...] = _rms(x_ref[...], g_ref[...]).astype(BF16)

    z_ref[...] = jnp.dot(xn_ref[...], w_ref[...], preferred_element_type=F32) + b_ref[...]

    @pl.when(pl.program_id(2) == pl.num_programs(2) - 1)
    def _():
        zt = lax.dot_general(wt_ref[...], xn_ref[...], (((1,), (1,)), ((), ())),
                             preferred_element_type=F32) + bt_ref[...]
        qt_ref[...] = zt[:fw]
        kt_ref[...] = zt[fw:2 * fw]
        vt_ref[...] = zt[2 * fw:]


def in_proj_prompt(x, g, w, b, wt, bt, *, n_seq, tm, tn):
    M, D = x.shape
    N = w.shape[1]
    S = M // n_seq
    fw = wt.shape[0] // 3
    nt = S // tm
    t_out = jax.ShapeDtypeStruct((n_seq, fw, S), F32)
    t_spec = pl.BlockSpec((None, fw, tm), lambda s, i, j: (s, 0, i))
    return pl.pallas_call(
        functools.partial(_in_proj_prompt_kernel, fw=fw),
        out_shape=(jax.ShapeDtypeStruct((M, N), F32), t_out, t_out, t_out),
        grid=(n_seq, nt, N // tn),
        in_specs=[pl.BlockSpec((tm, D), lambda s, i, j: (s * nt + i, 0)),
                  pl.BlockSpec((1, D), lambda s, i, j: (0, 0)),
                  pl.BlockSpec((D, tn), lambda s, i, j: (0, j)),
                  pl.BlockSpec((1, tn), lambda s, i, j: (0, j)),
                  pl.BlockSpec((3 * fw, D), lambda s, i, j: (0, 0)),
                  pl.BlockSpec((3 * fw, 1), lambda s, i, j: (0, 0))],
        out_specs=(pl.BlockSpec((tm, tn), lambda s, i, j: (s * nt + i, j)), t_spec, t_spec, t_spec),
        scratch_shapes=[pltpu.VMEM((tm, D), BF16)],
        compiler_params=_cparams("parallel", "parallel", "arbitrary"),
        name="in_proj_prompt",
    )(x, g.reshape(1, D), w, b.reshape(1, N), wt, bt.reshape(3 * fw, 1))


def _norm_matmul_t_kernel(x_ref, g_ref, wt_ref, o_ref):
    xn = _rms(x_ref[...], g_ref[...]).astype(BF16)
    o_ref[...] = lax.dot_general(wt_ref[...], xn, (((1,), (1,)), ((), ())), preferred_element_type=F32)


def norm_matmul_t(x, g, wt, *, n_seq):
    M, D = x.shape
    N = wt.shape[0]
    R = M // n_seq
    return pl.pallas_call(
        _norm_matmul_t_kernel,
        out_shape=jax.ShapeDtypeStruct((n_seq, N, R), F32),
        grid=(n_seq,),
        in_specs=[pl.BlockSpec((R, D), lambda s: (s, 0)),
                  pl.BlockSpec((1, D), lambda s: (0, 0)),
                  pl.BlockSpec((N, D), lambda s: (0, 0))],
        out_specs=pl.BlockSpec((None, N, R), lambda s: (s, 0, 0)),
        compiler_params=_cparams("parallel"),
        name="norm_matmul_t",
    )(x, g.reshape(1, D), wt)


def _gates_kernel(z_ref, o_ref, *, n_raw):
    z = z_ref[...]
    lane = lax.broadcasted_iota(jnp.int32, z.shape, 1)
    o_ref[...] = jnp.where(lane >= n_raw, _log_sigmoid(z), z)


def gate_activations(z, *, col_block, n_raw, tm):
    M = z.shape[0]
    return pl.pallas_call(
        functools.partial(_gates_kernel, n_raw=n_raw),
        out_shape=jax.ShapeDtypeStruct((M, LANES), F32),
        grid=(M // tm,),
        in_specs=[pl.BlockSpec((tm, LANES), lambda i: (i, col_block))],
        out_specs=pl.BlockSpec((tm, LANES), lambda i: (i, 0)),
        compiler_params=_cparams("parallel"),
        name="gate_activations",
    )(z)


def _out_proj_kernel(x_ref, r_ref, m_ref, f_ref, wr_ref, wm_ref, wf_ref, o_ref, *, f_transposed):
    f_term = _dot_tn(f_ref[...], wf_ref[...]) if f_transposed else _dot(f_ref[...], wf_ref[...])
    o_ref[...] = x_ref[...] + _dot(r_ref[...], wr_ref[...]) + _dot(m_ref[...], wm_ref[...]) + f_term


def out_proj(x, r, m, f, w_out, *, tm, n_seq=None):
    M, D = x.shape
    wr, wm = r.shape[1], m.shape[1]
    w_r, w_m, w_f = w_out[:wr], w_out[wr:wr + wm], w_out[wr + wm:]
    row = lambda w: pl.BlockSpec((tm, w), lambda i: (i, 0))
    full = lambda a: pl.BlockSpec(a.shape, lambda i: (0, 0))
    if n_seq is None:
        f_spec = row(f.shape[1])
    else:
        nt = M // n_seq // tm
        f_spec = pl.BlockSpec((None, f.shape[1], tm), lambda i: (i // nt, 0, i % nt))
    return pl.pallas_call(
        functools.partial(_out_proj_kernel, f_transposed=n_seq is not None),
        out_shape=jax.ShapeDtypeStruct((M, D), F32),
        grid=(M // tm,),
        in_specs=[row(D), row(wr), row(wm), f_spec, full(w_r), full(w_m), full(w_f)],
        out_specs=row(D),
        compiler_params=_cparams("parallel"),
        name="out_proj",
    )(x, r, m, f, w_r, w_m, w_f)


def _mlp_kernel(x_ref, g_ref, wu_ref, wd_ref, o_ref, xn_ref, acc_ref):
    j = pl.program_id(1)

    @pl.when(j == 0)
    def _():
        xn_ref[...] = _rms(x_ref[...], g_ref[...]).astype(BF16)
        acc_ref[...] = x_ref[...]

    h = jnp.maximum(jnp.dot(xn_ref[...], wu_ref[...], preferred_element_type=F32), 0.0)
    acc_ref[...] += jnp.dot((h * h).astype(BF16), wd_ref[...], preferred_element_type=F32)

    @pl.when(j == pl.num_programs(1) - 1)
    def _():
        o_ref[...] = acc_ref[...]


def sq_relu_mlp(x, g, w_up, w_down, *, tm, tf):
    M, D = x.shape
    FF = w_up.shape[1]
    return pl.pallas_call(
        _mlp_kernel,
        out_shape=jax.ShapeDtypeStruct((M, D), F32),
        grid=(M // tm, FF // tf),
        in_specs=[pl.BlockSpec((tm, D), lambda i, j: (i, 0)),
                  pl.BlockSpec((1, D), lambda i, j: (0, 0)),
                  pl.BlockSpec((D, tf), lambda i, j: (0, j)),
                  pl.BlockSpec((tf, D), lambda i, j: (j, 0))],
        out_specs=pl.BlockSpec((tm, D), lambda i, j: (i, 0)),
        scratch_shapes=[pltpu.VMEM((tm, D), BF16), pltpu.VMEM((tm, D), F32)],
        compiler_params=_cparams("parallel", "arbitrary"),
        name="sq_relu_mlp",
    )(x, g.reshape(1, D), w_up, w_down)


def _final_norm_kernel(x_ref, g_ref, o_ref):
    o_ref[...] = _rms(x_ref[...], g_ref[...])


def final_norm(x, g, *, tm):
    M, D = x.shape
    return pl.pallas_call(
        _final_norm_kernel,
        out_shape=jax.ShapeDtypeStruct((M, D), F32),
        grid=(M // tm,),
        in_specs=[pl.BlockSpec((tm, D), lambda i: (i, 0)), pl.BlockSpec((1, D), lambda i: (0, 0))],
        out_specs=pl.BlockSpec((tm, D), lambda i: (i, 0)),
        compiler_params=_cparams("parallel"),
        name="final_norm",
    )(x, g.reshape(1, D))


def _retention_kernel(q_ref, k_ref, v_ref, g_ref, cos_ref, sin_ref, s0_ref, gain_ref,
                      y_ref, sout_ref, s_sc, *, L, valid, log_gamma, seqs):
    c = pl.program_id(1)
    n_heads = len(log_gamma)
    W = n_heads * HEAD_DIM
    P = n_heads // 2
    HD = HEAD_DIM
    upper = lax.broadcasted_iota(jnp.int32, (1, LANES), 1) >= HD
    r_i = lax.broadcasted_iota(jnp.int32, (LANES, LANES), 0)
    c_i = lax.broadcasted_iota(jnp.int32, (LANES, LANES), 1)
    same_half = (r_i >= HD) == (c_i >= HD)
    half_ones = jnp.where(same_half, 1.0, 0.0).astype(BF16)

    @pl.when(c == 0)
    def _():
        zero = jnp.zeros((HD, HD), F32)
        for j in range(seqs):
            for p in range(P):
                s_sc[j, p] = jnp.concatenate(
                    [jnp.concatenate([s0_ref[j, 2 * p], zero], axis=1),
                     jnp.concatenate([zero, s0_ref[j, 2 * p + 1]], axis=1)], axis=0)

    cos = cos_ref[...]
    sin = sin_ref[...]
    lane = lax.broadcasted_iota(jnp.int32, (L, W), 1)
    first_half = (lane % HD) < HALF

    def rope(x):
        swapped = jnp.where(first_half, pltpu.roll(x, W - HALF, 1), pltpu.roll(x, HALF, 1))
        return x * cos + swapped * sin

    t_i = lax.broadcasted_iota(jnp.int32, (L, L), 0)
    s_i = lax.broadcasted_iota(jnp.int32, (L, L), 1)
    causal = t_i >= s_i
    diff = jnp.where(causal, (t_i - s_i).astype(F32), 0.0)
    decay = [jnp.where(causal, jnp.exp(lg * diff), 0.0) for lg in log_gamma]
    t_rep = lax.broadcasted_iota(jnp.int32, (L, LANES), 0).astype(F32)
    pair_lg = [jnp.where(upper, log_gamma[2 * p + 1], log_gamma[2 * p]) for p in range(P)]
    q_decay = [jnp.exp(lg * (t_rep + 1.0)) for lg in pair_lg]
    k_decay = [jnp.where(t_rep < valid, jnp.exp(lg * (valid - 1.0 - t_rep)), 0.0) for lg in pair_lg]
    s_decay = [jnp.exp(lg * float(valid)) for lg in pair_lg]

    pairs = [(j, p) for j in range(seqs) for p in range(P)]
    q, k, v = {}, {}, {}
    for j in range(seqs):
        qj = rope(q_ref[j])
        kj = rope(k_ref[j]) * SCALE
        for p in range(P):
            tile = slice(p * LANES, (p + 1) * LANES)
            q[j, p], k[j, p], v[j, p] = qj[:, tile], kj[:, tile], v_ref[j, :, tile]
    scores, cross = {}, {}
    for j, p in pairs:
        qb = q[j, p].astype(BF16)
        for e in range(2):
            scores[j, 2 * p + e] = _dot_nt(qb, jnp.where(upper if e else ~upper, k[j, p], 0.0))
        cross[j, p] = _dot(qb, s_sc[j, p])
    outs = {}
    for j, p in pairs:
        lhs = jnp.concatenate([scores[j, 2 * p] * decay[2 * p], scores[j, 2 * p + 1] * decay[2 * p + 1]],
                              axis=1)
        v_sel = jnp.concatenate([jnp.where(upper, 0.0, v[j, p]), jnp.where(upper, v[j, p], 0.0)], axis=0)
        o = _dot(lhs, v_sel) + cross[j, p] * q_decay[p]
        o2 = o * o
        hi = o2.astype(BF16)
        lo = (o2 - hi.astype(F32)).astype(BF16)
        mean_sq = (jnp.dot(hi, half_ones, preferred_element_type=F32)
                   + jnp.dot(lo, half_ones, preferred_element_type=F32)) * (1.0 / HD)
        outs[j, p] = o * lax.rsqrt(mean_sq + EPS)
    for j, p in pairs:
        update = jnp.where(same_half, _dot_tn(k[j, p] * k_decay[p], v[j, p]), 0.0)
        s_sc[j, p] = s_decay[p] * s_sc[j, p] + update
    for j in range(seqs):
        g = g_ref[j]
        y_ref[j] = jnp.concatenate([outs[j, p] for p in range(P)], axis=-1) * gain_ref[...] * (
            g * _sigmoid(g))

    @pl.when(c == pl.num_programs(1) - 1)
    def _():
        for j in range(seqs):
            for p in range(P):
                blk = s_sc[j, p]
                sout_ref[j, 2 * p] = blk[:HD, :HD]
                sout_ref[j, 2 * p + 1] = blk[HD:, HD:]


def retention(z, cos, sin, s0, gain, *, n_seq, L, valid, col0, log_gamma, seqs):
    M, N = z.shape
    n_heads = len(log_gamma)
    W = n_heads * HEAD_DIM
    rows = M // n_seq
    cb = col0 // W
    z3 = z.reshape(n_seq, rows, N)
    zspec = lambda j: pl.BlockSpec((seqs, L, W), lambda b, c: (b, c, cb + j))
    tab = pl.BlockSpec((L, W), lambda b, c: (c, 0))
    st = pl.BlockSpec((seqs, n_heads, HEAD_DIM, HEAD_DIM), lambda b, c: (b, 0, 0, 0))
    y, s_out = pl.pallas_call(
        functools.partial(_retention_kernel, L=L, valid=valid, log_gamma=log_gamma, seqs=seqs),
        out_shape=(jax.ShapeDtypeStruct((n_seq, rows, W), F32),
                   jax.ShapeDtypeStruct((n_seq, n_heads, HEAD_DIM, HEAD_DIM), F32)),
        grid=(n_seq // seqs, rows // L),
        in_specs=[zspec(0), zspec(1), zspec(2), zspec(3), tab, tab, st,
                  pl.BlockSpec((1, W), lambda b, c: (0, 0))],
        out_specs=(pl.BlockSpec((seqs, L, W), lambda b, c: (b, c, 0)), st),
        scratch_shapes=[pltpu.VMEM((seqs, n_heads // 2, LANES, LANES), F32)],
        compiler_params=_cparams("parallel", "arbitrary"),
        name="retention",
    )(z3, z3, z3, z3, cos, sin, s0, gain.reshape(1, W))
    return y.reshape(M, W), s_out


def _cumsum_rows_any(x, n):
    if n > SUBLANES:
        return _cumsum_rows(x, n)
    t = lax.broadcasted_iota(jnp.int32, x.shape, 0)
    return sum(jnp.where(t >= u, x[u:u + 1, :], 0.0) for u in range(n))


def _cumsum_lanes_any(x, n):
    if n > SUBLANES:
        return _cumsum_lanes(x, n)
    s = lax.broadcasted_iota(jnp.int32, x.shape, 1)
    return sum(jnp.where(s >= u, x[:, u:u + 1], 0.0) for u in range(n))


def _cummax_rows(x, n):
    row = lax.broadcasted_iota(jnp.int32, x.shape, 0)
    shift = 1
    while shift < n:
        x = jnp.maximum(x, jnp.where(row >= shift, pltpu.roll(x, shift, 0), NEG_INF))
        shift *= 2
    return x


def _mlstm_kernel(q_ref, k_ref, v_ref, og_ref, gc_ref, gr_ref, c0_ref, n0_ref, m0_ref, gain_ref,
                  y_ref, cout_ref, nout_ref, mout_ref, c_sc, n_sc, m_sc, *, L, valid, n_heads, seqs):
    c = pl.program_id(1)
    P = n_heads // 2
    HD = HEAD_DIM
    lane = lax.broadcasted_iota(jnp.int32, (1, LANES), 1)
    upper = lane >= HD
    r_i = lax.broadcasted_iota(jnp.int32, (LANES, LANES), 0)
    c_i = lax.broadcasted_iota(jnp.int32, (LANES, LANES), 1)
    same_half = (r_i >= HD) == (c_i >= HD)
    half_ones = jnp.where(same_half, 1.0, 0.0).astype(BF16)

    @pl.when(c == 0)
    def _():
        zero = jnp.zeros((HD, HD), F32)
        for j in range(seqs):
            for p in range(P):
                c_sc[j, p] = jnp.concatenate(
                    [jnp.concatenate([c0_ref[j, 2 * p], zero], axis=1),
                     jnp.concatenate([zero, c0_ref[j, 2 * p + 1]], axis=1)], axis=0)
                n_sc[j, p] = jnp.concatenate([n0_ref[j, 2 * p:2 * p + 1, :], n0_ref[j, 2 * p + 1:2 * p + 2, :]],
                                             axis=1)
        m_sc[...] = m0_ref[...]

    t_i = lax.broadcasted_iota(jnp.int32, (L, L), 0)
    s_i = lax.broadcasted_iota(jnp.int32, (L, L), 1)
    causal = t_i >= s_i
    row_valid = lax.broadcasted_iota(jnp.int32, (L, LANES), 0) < valid
    rr = lax.broadcasted_iota(jnp.int32, (2 * L, LANES), 0)
    cc = lax.broadcasted_iota(jnp.int32, (2 * L, LANES), 1)
    sum_sel = jnp.where((rr >= L) == (cc >= HD), 1.0, 0.0)

    pairs = [(j, p) for j in range(seqs) for p in range(P)]
    heads = [(j, h) for j in range(seqs) for h in range(n_heads)]
    q, k, v, og = {}, {}, {}, {}
    f_rep, i_rep, w_row, m_prev = {}, {}, {}, {}
    for j in range(seqs):
        gc = gc_ref[j]
        gr = gr_ref[j]
        f_col = _cumsum_rows_any(gc, L)
        f_row = _cumsum_lanes_any(gr, L)
        for p in range(P):
            tile = slice(p * LANES, (p + 1) * LANES)
            q[j, p] = q_ref[j, :, tile]
            k[j, p] = k_ref[j, :, tile] * SCALE
            v[j, p] = v_ref[j, :, tile]
            og[j, p] = _sigmoid(og_ref[j, :, tile])
        for h in range(n_heads):
            f_rep[j, h] = jnp.broadcast_to(f_col[:, n_heads + h:n_heads + h + 1], (L, LANES))
            i_rep[j, h] = jnp.broadcast_to(gc[:, h:h + 1], (L, LANES))
            w_row[j, h] = f_row[n_heads + h:n_heads + h + 1, :] - gr[h:h + 1, :]
            m_prev[j, h] = m_sc[j, h:h + 1, :]

    qk_raw, cq, nq = {}, {}, {}
    for j, p in pairs:
        qb = q[j, p].astype(BF16)
        for e in range(2):
            k_half = jnp.where(upper if e else ~upper, k[j, p], 0.0)
            qk_raw[j, 2 * p + e] = _dot_nt(qb, k_half)
        cq[j, p] = _dot_nt(qb, c_sc[j, p])
        n_rep = jnp.where(same_half, jnp.broadcast_to(n_sc[j, p], (LANES, LANES)), 0.0)
        nq[j, p] = _dot_nt(qb, n_rep)

    m_t, u, w_inter, e_rep = {}, {}, {}, {}
    for j, h in heads:
        peak = jnp.maximum(m_prev[j, h], _cummax_rows(i_rep[j, h] - f_rep[j, h], L))
        m_t[j, h] = f_rep[j, h] + peak
        u[j, h] = f_rep[j, h] - m_t[j, h]
        w_inter[j, h] = jnp.exp(u[j, h] + m_prev[j, h])
        e_rep[j, h] = jnp.exp(-m_t[j, h])

    outs = {}
    for j, p in pairs:
        h0, h1 = 2 * p, 2 * p + 1
        gated = [qk_raw[j, h] * jnp.where(causal, jnp.exp(u[j, h][:, :L] - w_row[j, h]), 0.0)
                 for h in (h0, h1)]
        lhs = jnp.concatenate(gated, axis=1)
        v_sel = jnp.concatenate([jnp.where(upper, 0.0, v[j, p]), jnp.where(upper, v[j, p], 0.0)], axis=0)
        mix = _dot(lhs, jnp.concatenate([v_sel, sum_sel], axis=1))
        w_pair = jnp.where(upper, w_inter[j, h1], w_inter[j, h0])
        num = mix[:, :LANES] + w_pair * cq[j, p]
        den = mix[:, LANES:] + w_pair * nq[j, p]
        floor = jnp.where(upper, e_rep[j, h1], e_rep[j, h0])
        y = og[j, p] * (num / jnp.maximum(jnp.abs(den), floor))
        y2 = y * y
        hi = y2.astype(BF16)
        lo = (y2 - hi.astype(F32)).astype(BF16)
        mean_sq = (jnp.dot(hi, half_ones, preferred_element_type=F32)
                   + jnp.dot(lo, half_ones, preferred_element_type=F32)) * (1.0 / HD)
        outs[j, p] = y * lax.rsqrt(mean_sq + EPS)

    for j, p in pairs:
        h0, h1 = 2 * p, 2 * p + 1
        w_s, w_c, m_last = {}, {}, {}
        for h in (h0, h1):
            m_last[h] = m_t[j, h][valid - 1:valid, :]
            f_last = f_rep[j, h][valid - 1:valid, :]
            w_s[h] = jnp.where(row_valid, jnp.exp(f_last - f_rep[j, h] + i_rep[j, h] - m_last[h]), 0.0)
            w_c[h] = jnp.exp(f_last + m_prev[j, h] - m_last[h])
        ws_pair = jnp.where(upper, w_s[h1], w_s[h0])
        wc_pair = jnp.where(upper, w_c[h1], w_c[h0])
        update = jnp.where(same_half, _dot_tn(v[j, p] * ws_pair, k[j, p]), 0.0)
        c_sc[j, p] = wc_pair * c_sc[j, p] + update
        n_sc[j, p] = wc_pair * n_sc[j, p] + jnp.sum(k[j, p] * ws_pair, axis=0, keepdims=True)
        m_sc[j, h0:h0 + 1, :] = m_last[h0]
        m_sc[j, h1:h1 + 1, :] = m_last[h1]
    for j in range(seqs):
        y_ref[j] = jnp.concatenate([outs[j, p] for p in range(P)], axis=-1) * gain_ref[...]

    @pl.when(c == pl.num_programs(1) - 1)
    def _():
        for j in range(seqs):
            for p in range(P):
                blk = c_sc[j, p]
                row = n_sc[j, p]
                cout_ref[j, 2 * p] = blk[:HD, :HD]
                cout_ref[j, 2 * p + 1] = blk[HD:, HD:]
                nout_ref[j, 2 * p:2 * p + 1, :] = row[:, :HD]
                nout_ref[j, 2 * p + 1:2 * p + 2, :] = row[:, HD:]
        mout_ref[...] = m_sc[...]


def mlstm(z, gc, gr, c0, n0, m0, gain, *, n_seq, L, valid, col0, n_heads, seqs):
    M, N = z.shape
    W = n_heads * HEAD_DIM
    P = n_heads // 2
    assert n_heads % 2 == 0 and L <= LANES
    rows = M // n_seq
    cb = col0 // W
    z3 = z.reshape(n_seq, rows, N)
    zspec = lambda j: pl.BlockSpec((seqs, L, W), lambda b, c: (b, c, cb + j))
    cst = pl.BlockSpec((seqs, n_heads, HEAD_DIM, HEAD_DIM), lambda b, c: (b, 0, 0, 0))
    nst = pl.BlockSpec((seqs, n_heads, HEAD_DIM), lambda b, c: (b, 0, 0))
    mst = pl.BlockSpec((seqs, n_heads, LANES), lambda b, c: (b, 0, 0))
    m0_rep = jnp.broadcast_to(m0[:, :, None], (n_seq, n_heads, LANES))
    y, c_out, n_out, m_out = pl.pallas_call(
        functools.partial(_mlstm_kernel, L=L, valid=valid, n_heads=n_heads, seqs=seqs),
        out_shape=(jax.ShapeDtypeStruct((n_seq, rows, W), F32),
                   jax.ShapeDtypeStruct((n_seq, n_heads, HEAD_DIM, HEAD_DIM), F32),
                   jax.ShapeDtypeStruct((n_seq, n_heads, HEAD_DIM), F32),
                   jax.ShapeDtypeStruct((n_seq, n_heads, LANES), F32)),
        grid=(n_seq // seqs, rows // L),
        in_specs=[zspec(0), zspec(1), zspec(2), zspec(3),
                  pl.BlockSpec((seqs, L, LANES), lambda b, c: (b, c, 0)),
                  pl.BlockSpec((seqs, GATE_ROWS, L), lambda b, c: (b, 0, c)),
                  cst, nst, mst,
                  pl.BlockSpec((1, W), lambda b, c: (0, 0))],
        out_specs=(pl.BlockSpec((seqs, L, W), lambda b, c: (b, c, 0)), cst, nst, mst),
        scratch_shapes=[pltpu.VMEM((seqs, P, LANES, LANES), F32),
                        pltpu.VMEM((seqs, P, 1, LANES), F32),
                        pltpu.VMEM((seqs, n_heads, LANES), F32)],
        compiler_params=_cparams("parallel", "arbitrary"),
        name="mlstm",
    )(z3, z3, z3, z3, gc.reshape(n_seq, rows, LANES), gr, c0, n0, m0_rep, gain.reshape(1, W))
    return y.reshape(M, W), c_out, n_out, m_out[:, :, 0]


def _fox_prep_kernel(gc_ref, gr_ref, k_ref, qt_ref, vt_ref, ka_ref, qa_ref, vb_ref, cc_sc, cr_sc,
                     *, L, n_heads, g0):
    @pl.when(pl.program_id(1) == 0)
    def _():
        cc_sc[...] = jnp.zeros_like(cc_sc)
        cr_sc[...] = jnp.zeros_like(cr_sc)

    fc = _cumsum_rows(gc_ref[...], L) + cc_sc[...]
    fr = _cumsum_lanes(gr_ref[...], L) + cr_sc[...]
    cc_sc[...] = fc[L - 1:L, :]
    cr_sc[...] = fr[:, L - 1:L]
    k = k_ref[...]
    qt = qt_ref[...] * SCALE
    lane = lax.broadcasted_iota(jnp.int32, (L, HEAD_DIM), 1)
    row = lax.broadcasted_iota(jnp.int32, (HEAD_DIM, L), 0)
    for h in range(n_heads):
        sl = _head_slice(h)
        kh, km, kl = (p.astype(F32) for p in _split3(fc[:, g0 + h:g0 + h + 1]))
        k_aug = jnp.where(lane < 3, 1.0,
                          jnp.where(lane == 3, -kh, jnp.where(lane == 4, -km, jnp.where(lane == 5, -kl, 0.0))))
        ka_ref[:, h * AUG:(h + 1) * AUG] = jnp.concatenate([k[:, sl], k_aug], axis=1).astype(BF16)
        qh, qm, ql = (p.astype(F32) for p in _split3(fr[g0 + h:g0 + h + 1, :]))
        q_aug = jnp.where(row == 0, qh,
                          jnp.where(row == 1, qm, jnp.where(row == 2, ql, jnp.where(row < 6, 1.0, 0.0))))
        qa_ref[h * AUG:(h + 1) * AUG, :] = jnp.concatenate([qt[sl, :], q_aug], axis=0).astype(BF16)
    vb_ref[...] = vt_ref[...].astype(BF16)


def fox_prep(gc, gr, z, qt, vt, *, n_seq, L, k_col0, n_heads, g0):
    M = gc.shape[0]
    S = M // n_seq
    nc = S // L
    W = n_heads * HEAD_DIM
    t_spec = pl.BlockSpec((None, W, L), lambda b, c: (b, 0, c))
    return pl.pallas_call(
        functools.partial(_fox_prep_kernel, L=L, n_heads=n_heads, g0=g0),
        out_shape=(jax.ShapeDtypeStruct((M, n_heads * AUG), BF16),
                   jax.ShapeDtypeStruct((n_seq, n_heads * AUG, S), BF16),
                   jax.ShapeDtypeStruct((n_seq, W, S), BF16)),
        grid=(n_seq, nc),
        in_specs=[pl.BlockSpec((L, LANES), lambda b, c: (b * nc + c, 0)),
                  pl.BlockSpec((None, GATE_ROWS, L), lambda b, c: (b, 0, c)),
                  pl.BlockSpec((L, W), lambda b, c: (b * nc + c, k_col0 // W)),
                  t_spec, t_spec],
        out_specs=(pl.BlockSpec((L, n_heads * AUG), lambda b, c: (b * nc + c, 0)),
                   pl.BlockSpec((None, n_heads * AUG, L), lambda b, c: (b, 0, c)),
                   t_spec),
        scratch_shapes=[pltpu.VMEM((1, LANES), F32), pltpu.VMEM((GATE_ROWS, 1), F32)],
        compiler_params=_cparams("parallel", "arbitrary"),
        name="fox_prep",
    )(gc, gr, z, qt, vt)


def _fox_flash_kernel(qi_ref, kj_ref, ka_ref, qa_ref, vt_ref, o_ref, m_sc, l_sc, acc_sc, s_sc, p_sc,
                      *, tq, n_heads):
    step = pl.program_id(1)
    qi = qi_ref[step]
    kj = kj_ref[step]

    @pl.when(kj == 0)
    def _():
        m_sc[...] = jnp.full_like(m_sc, NEG_INF)
        l_sc[...] = jnp.zeros_like(l_sc)
        acc_sc[...] = jnp.zeros_like(acc_sc)

    def update(diagonal):
        if diagonal:
            key = lax.broadcasted_iota(jnp.int32, (tq, tq), 0)
            qry = lax.broadcasted_iota(jnp.int32, (tq, tq), 1)
            visible = key <= qry
        block_max = []
        for h in range(n_heads):
            s = jnp.dot(ka_ref[:, h * AUG:(h + 1) * AUG], qa_ref[h * AUG:(h + 1) * AUG, :],
                        preferred_element_type=F32)
            if diagonal:
                s = jnp.where(visible, s, NEG_INF)
            s_sc[h] = s
            block_max.append(jnp.max(s, axis=0, keepdims=True))
        m_old = m_sc[...]
        m_new = jnp.maximum(m_old, jnp.concatenate(block_max, axis=0))
        alpha = jnp.exp(m_old - m_new)
        m_sc[...] = m_new
        sums = []
        for h in range(n_heads):
            p = jnp.exp(s_sc[h] - m_new[h:h + 1, :])
            sums.append(jnp.sum(p, axis=0, keepdims=True))
            p_sc[h] = p.astype(BF16)
        l_sc[...] = alpha * l_sc[...] + jnp.concatenate(sums, axis=0)
        for h in range(n_heads):
            hs = _head_slice(h)
            acc_sc[hs, :] = alpha[h:h + 1, :] * acc_sc[hs, :] + jnp.dot(
                vt_ref[hs, :], p_sc[h], preferred_element_type=F32)

    @pl.when(kj < qi)
    def _():
        update(False)

    @pl.when(kj == qi)
    def _():
        update(True)
        o_ref[...] = jnp.concatenate(
            [acc_sc[_head_slice(h), :] / l_sc[h:h + 1, :] for h in range(n_heads)], axis=0)


def fox_flash(ka, qa, vt, *, tq, n_heads):
    n_seq, W, S = vt.shape
    nq = S // tq
    pairs = [(i, j) for i in range(nq) for j in range(i + 1)]
    qi_tab = jnp.asarray([p[0] for p in pairs], jnp.int32)
    kj_tab = jnp.asarray([p[1] for p in pairs], jnp.int32)
    grid_spec = pltpu.PrefetchScalarGridSpec(
        num_scalar_prefetch=2,
        grid=(n_seq, len(pairs)),
        in_specs=[pl.BlockSpec((tq, n_heads * AUG), lambda b, s, qi, kj: (b * nq + kj[s], 0)),
                  pl.BlockSpec((None, n_heads * AUG, tq), lambda b, s, qi, kj: (b, 0, qi[s])),
                  pl.BlockSpec((None, W, tq), lambda b, s, qi, kj: (b, 0, kj[s]))],
        out_specs=pl.BlockSpec((None, W, tq), lambda b, s, qi, kj: (b, 0, qi[s])),
        scratch_shapes=[pltpu.VMEM((n_heads, tq), F32), pltpu.VMEM((n_heads, tq), F32),
                        pltpu.VMEM((W, tq), F32),
                        pltpu.VMEM((n_heads, tq, tq), F32), pltpu.VMEM((n_heads, tq, tq), BF16)],
    )
    return pl.pallas_call(
        functools.partial(_fox_flash_kernel, tq=tq, n_heads=n_heads),
        out_shape=jax.ShapeDtypeStruct((n_seq, W, S), F32),
        grid_spec=grid_spec,
        compiler_params=_cparams("parallel", "arbitrary"),
        name="fox_flash",
    )(qi_tab, kj_tab, ka, qa, vt)


def _fox_decode_kernel(pt_ref, q_ref, kn_ref, vn_ref, gc_ref, gr_ref, *rest, valid, n_heads, g0, n_pages):
    k_refs = rest[:n_pages]
    v_refs = rest[n_pages:2 * n_pages]
    lf_refs = rest[2 * n_pages:3 * n_pages]
    o_ref = rest[3 * n_pages]
    del pt_ref
    T = SUBLANES
    H = n_heads
    page = lf_refs[0].shape[-1]

    q = q_ref[...] * SCALE
    kn = kn_ref[...]
    vn = vn_ref[...]
    gc = gc_ref[...]
    gr = gr_ref[...]
    t_i = lax.broadcasted_iota(jnp.int32, (T, LANES), 0)
    cum_col = sum(jnp.where(t_i >= u, gc[u:u + 1, :], 0.0) for u in range(valid))
    u_i = lax.broadcasted_iota(jnp.int32, (GATE_ROWS, T), 1)
    cum_row = sum(jnp.where(u_i >= u, gr[:, u:u + 1], 0.0) for u in range(valid))
    t_q = lax.broadcasted_iota(jnp.int32, (T, T), 0)
    u_k = lax.broadcasted_iota(jnp.int32, (T, T), 1)
    new_visible = (u_k <= t_q) & (u_k < valid)

    s_a = lax.broadcasted_iota(jnp.int32, (page, page), 0)
    s_b = lax.broadcasted_iota(jnp.int32, (page, page), 1)
    later = jnp.where(s_a > s_b, 1.0, 0.0).astype(BF16)
    lf = jnp.concatenate([r[...] for r in lf_refs], axis=0)
    within = sum(jnp.dot(p, later, preferred_element_type=F32) for p in _split3(lf))
    totals = within[:, 0:1] + lf[:, 0:1]
    carry = jnp.zeros((H, 1), F32)
    suffix = [None] * n_pages
    for p in reversed(range(n_pages)):
        suffix[p] = within[p * H:(p + 1) * H, :] + carry
        carry = carry + totals[p * H:(p + 1) * H, :]

    s_past, s_new = [], []
    for h in range(H):
        qh = q[:, _head_slice(h)]
        f_t = cum_col[:, g0 + h:g0 + h + 1]
        kt = jnp.concatenate([r[h] for r in k_refs], axis=1)
        bias = jnp.concatenate([sp[h:h + 1, :] for sp in suffix], axis=1)
        s_past.append(_dot(qh, kt) + (f_t + bias))
        sn = _dot_nt(qh, kn[:, _head_slice(h)]) + (f_t - cum_row[g0 + h:g0 + h + 1, :])
        s_new.append(jnp.where(new_visible, sn, NEG_INF))
    p_past, p_new, denom = [], [], []
    for h in range(H):
        m = jnp.maximum(jnp.max(s_past[h], axis=1, keepdims=True), jnp.max(s_new[h], axis=1, keepdims=True))
        p_past.append(jnp.exp(s_past[h] - m))
        p_new.append(jnp.exp(s_new[h] - m))
        denom.append(jnp.sum(p_past[h], axis=1, keepdims=True) + jnp.sum(p_new[h], axis=1, keepdims=True))
    outs = []
    for h in range(H):
        vt = jnp.concatenate([r[h] for r in v_refs], axis=1)
        outs.append((_dot_nt(p_past[h], vt) + _dot(p_new[h], vn[:, _head_slice(h)])) / denom[h])
    o_ref[...] = jnp.concatenate(outs, axis=1)


def fox_decode(z, gc, gr, cache_kt, cache_vt, cache_lf_t, page_table, *, layer, valid, col0, n_heads, g0):
    M = z.shape[0]
    T = SUBLANES
    n_seq = M // T
    W = n_heads * HEAD_DIM
    n_pages = page_table.shape[1]
    page = cache_kt.shape[-1]
    cb = col0 // W

    def page_idx(i, nd):
        return lambda b, pt: (layer, pt[b * n_pages + i]) + (0,) * nd

    kv_specs = [pl.BlockSpec((None, None, n_heads, HEAD_DIM, page), page_idx(i, 3)) for i in range(n_pages)]
    lf_specs = [pl.BlockSpec((None, None, n_heads, page), page_idx(i, 2)) for i in range(n_pages)]
    zspec = lambda c: pl.BlockSpec((T, W), lambda b, pt: (b, c))
    grid_spec = pltpu.PrefetchScalarGridSpec(
        num_scalar_prefetch=1,
        grid=(n_seq,),
        in_specs=[zspec(cb), zspec(cb + 1), zspec(cb + 2),
                  pl.BlockSpec((T, LANES), lambda b, pt: (b, 0)),
                  pl.BlockSpec((None, GATE_ROWS, T), lambda b, pt: (b, 0, 0))] + kv_specs + kv_specs + lf_specs,
        out_specs=pl.BlockSpec((T, W), lambda b, pt: (b, 0)),
    )
    return pl.pallas_call(
        functools.partial(_fox_decode_kernel, valid=valid, n_heads=n_heads, g0=g0, n_pages=n_pages),
        out_shape=jax.ShapeDtypeStruct((M, W), F32),
        grid_spec=grid_spec,
        compiler_params=_cparams("parallel"),
        name="fox_decode",
    )(page_table.reshape(-1), z, z, z, gc, gr,
      *([cache_kt] * n_pages), *([cache_vt] * n_pages), *([cache_lf_t] * n_pages))


def _cross_prompt_kernel(x_ref, g_ref, wq_ref, kt_ref, vt_ref, wo_ref, o_ref, *, n_heads):
    x = x_ref[...]
    q = jnp.dot(_rms(x, g_ref[...]).astype(BF16), wq_ref[...], preferred_element_type=F32) * SCALE
    scores = [_dot(q[:, _head_slice(h)], kt_ref[_head_slice(h), :]) for h in range(n_heads)]
    probs = [_softmax_rows(s) for s in scores]
    outs = [_dot_nt(probs[h], vt_ref[_head_slice(h), :]) for h in range(n_heads)]
    o_ref[...] = x + jnp.dot(jnp.concatenate(outs, axis=-1).astype(BF16), wo_ref[...],
                             preferred_element_type=F32)


def cross_attn_prompt(x, g, w_q, mkv_t, w_o, *, n_heads, tm):
    M, D = x.shape
    W = n_heads * HEAD_DIM
    n_seq, _, mem_len = mkv_t.shape
    per_seq = M // n_seq // tm
    return pl.pallas_call(
        functools.partial(_cross_prompt_kernel, n_heads=n_heads),
        out_shape=jax.ShapeDtypeStruct((M, D), F32),
        grid=(M // tm,),
        in_specs=[pl.BlockSpec((tm, D), lambda i: (i, 0)),
                  pl.BlockSpec((1, D), lambda i: (0, 0)),
                  pl.BlockSpec((D, W), lambda i: (0, 0)),
                  pl.BlockSpec((None, W, mem_len), lambda i: (i // per_seq, 0, 0)),
                  pl.BlockSpec((None, W, mem_len), lambda i: (i // per_seq, 1, 0)),
                  pl.BlockSpec((W, D), lambda i: (0, 0))],
        out_specs=pl.BlockSpec((tm, D), lambda i: (i, 0)),
        compiler_params=_cparams("parallel"),
        name="cross_attn_prompt",
    )(x, g.reshape(1, D), w_q, mkv_t, mkv_t, w_o)


def _cross_sample_kernel(x_ref, g_ref, wq_ref, kt_ref, vt_ref, wo_ref, o_ref, *, n_heads, seqs):
    T = SUBLANES
    x = x_ref[...]
    q = jnp.dot(_rms(x, g_ref[...]).astype(BF16), wq_ref[...], preferred_element_type=F32) * SCALE
    items = [(b, h) for b in range(seqs) for h in range(n_heads)]
    scores = {(b, h): _dot(q[b * T:(b + 1) * T, _head_slice(h)], kt_ref[b, h]) for b, h in items}
    probs = {i: _softmax_rows(scores[i]) for i in items}
    outs = {(b, h): _dot_nt(probs[b, h], vt_ref[b, h]) for b, h in items}
    rows = [jnp.concatenate([outs[b, h] for h in range(n_heads)], axis=1) for b in range(seqs)]
    o_ref[...] = x + jnp.dot(jnp.concatenate(rows, axis=0).astype(BF16), wo_ref[...],
                             preferred_element_type=F32)


def cross_attn_sample(x, g, w_q, cache_kt, cache_vt, w_o, *, layer, n_heads, seqs):
    M, D = x.shape
    T = SUBLANES
    W = n_heads * HEAD_DIM
    mem_len = cache_kt.shape[-1]
    kv = pl.BlockSpec((None, seqs, n_heads, HEAD_DIM, mem_len), lambda i: (layer, i, 0, 0, 0))
    return pl.pallas_call(
        functools.partial(_cross_sample_kernel, n_heads=n_heads, seqs=seqs),
        out_shape=jax.ShapeDtypeStruct((M, D), F32),
        grid=(M // (seqs * T),),
        in_specs=[pl.BlockSpec((seqs * T, D), lambda i: (i, 0)),
                  pl.BlockSpec((1, D), lambda i: (0, 0)),
                  pl.BlockSpec((D, W), lambda i: (0, 0)),
                  kv, kv,
                  pl.BlockSpec((W, D), lambda i: (0, 0))],
        out_specs=pl.BlockSpec((seqs * T, D), lambda i: (i, 0)),
        compiler_params=_cparams("parallel"),
        name="cross_attn_sample",
    )(x, g.reshape(1, D), w_q, cache_kt, cache_vt, w_o)


def _rope_tables(pos, n_heads):
    inv = ROPE_BASE ** (-jnp.arange(HALF, dtype=F32) / HALF)
    ang = pos.astype(F32)[:, None] * inv[None, :]
    cos, sin = jnp.cos(ang), jnp.sin(ang)
    return (jnp.tile(jnp.concatenate([cos, cos], axis=-1), (1, n_heads)),
            jnp.tile(jnp.concatenate([-sin, sin], axis=-1), (1, n_heads)))


def _pick(n, candidates):
    for c in candidates:
        if n % c == 0:
            return c
    return n


def kernel(x_prompt, x_sample, mem_prompt, state_ret, state_mlstm_c, state_mlstm_n, state_mlstm_m,
           cache_fox_k, cache_fox_v, cache_fox_logf, cache_mem_k, cache_mem_v, page_table,
           g_mix, w_in, b_in, g_ret, g_mlstm, w_out, g_cross, g_mem, w_cq, w_ckv, w_co,
           g_mlp, w_up, w_down, g_final):
    B, S, D = x_prompt.shape
    NB, T, _ = x_sample.shape
    depth = w_in.shape[0]
    RH = state_ret.shape[2]
    MH = state_mlstm_c.shape[2]
    FH = cache_fox_k.shape[3]
    MEMH = cache_mem_k.shape[3]
    RW, MW, FW, MEMW = RH * HEAD_DIM, MH * HEAD_DIM, FH * HEAD_DIM, MEMH * HEAD_DIM
    n_pages = page_table.shape[1]
    page = cache_fox_k.shape[2]
    mem_len = mem_prompt.shape[1]
    TP = SUBLANES
    assert T <= TP and S % CHUNK == 0 and 2 * MH + FH <= GATE_ROWS
    assert RW % LANES == 0 and MW % LANES == 0 and FW % LANES == 0

    sizes = (RW,) * 4 + (MW,) * 4 + (MH, MH) + (FW,) * 3 + (FH,)
    starts = np.concatenate([[0], np.cumsum(sizes)[:-1]])
    cols = lambda a, *ids: jnp.concatenate([a[..., starts[i]:starts[i] + sizes[i]] for i in ids], axis=-1)
    ret_col0, mls_col0 = 0, 4 * RW
    fox_col0 = mls_col0 + 4 * MW
    assert fox_col0 % FW == 0 and mls_col0 % MW == 0
    fox_g0 = 2 * MH

    def regroup(ids, gate_col0):
        ids = ids + (8, 9, 13)
        n_cols = gate_col0 + 2 * LANES
        pad = n_cols - sum(sizes[i] for i in ids)
        return (jnp.pad(cols(w_in, *ids), ((0, 0), (0, 0), (0, pad))).astype(BF16),
                jnp.pad(cols(b_in, *ids), ((0, 0), (0, pad))), n_cols)

    gate_col0_s = fox_col0 + 3 * FW
    w_in_s, b_in_s, n_cols_s = regroup((0, 1, 2, 3, 4, 5, 6, 7, 10, 11, 12), gate_col0_s)
    gate_col0_p = fox_col0 + FW
    w_in_p, b_in_p, n_cols_p = regroup((0, 1, 2, 3, 4, 5, 6, 7, 11), gate_col0_p)
    w_fox_t = jnp.swapaxes(cols(w_in, 10, 11, 12), 1, 2).astype(BF16)
    b_fox = cols(b_in, 10, 11, 12)

    w_out_b, w_cq_b, w_co_b = (w.astype(BF16) for w in (w_out, w_cq, w_co))
    w_ckv_t = jnp.swapaxes(w_ckv, 1, 2).astype(BF16)
    w_up_b, w_down_b = w_up.astype(BF16), w_down.astype(BF16)

    past = n_pages * page
    cos_p, sin_p = _rope_tables(jnp.arange(S, dtype=jnp.int32), RH)
    pos_s = jnp.minimum(past + jnp.arange(TP, dtype=jnp.int32), past + T - 1)
    cos_s, sin_s = _rope_tables(pos_s, RH)
    log_gamma = tuple(float(np.log1p(-np.exp2(-5.0 - h))) for h in range(RH))

    cache_kt = jnp.transpose(cache_fox_k, (0, 1, 3, 4, 2))
    cache_vt = jnp.transpose(cache_fox_v, (0, 1, 3, 4, 2))
    cache_lf_t = jnp.swapaxes(cache_fox_logf, 2, 3)
    mem_kt = jnp.transpose(cache_mem_k, (0, 1, 3, 4, 2))
    mem_vt = jnp.transpose(cache_mem_v, (0, 1, 3, 4, 2))

    Mp, Ms = B * S, NB * TP
    xp = x_prompt.reshape(Mp, D)
    xs = jnp.pad(x_sample, ((0, 0), (0, TP - T), (0, 0))).reshape(Ms, D)
    mem = mem_prompt.reshape(B * mem_len, D)

    tm_p = _pick(S, (1024, 512, 256, 128))
    tm_in = _pick(S, (512, 256, 128))
    tm_s = _pick(Ms, (1024, 512, 256, 128, 64, 32, 16, 8))
    tn_s = _pick(n_cols_s, (1280, 768, 640, 512, 256, 128))
    tn_p = n_cols_p
    tf = _pick(w_up.shape[2], (2048, 1024, 512, 256, 128))
    tq = _pick(S, (512, 256, 128))
    l_prep = _pick(S, (512, 256, 128))
    seqs = _pick(NB, (8, 4, 2, 1))
    seqs_p = _pick(B, (2, 1))
    zeros = lambda *s: jnp.zeros(s, F32)

    def gates_of(z, gate_col0, n_seq, rows, tm):
        gc = gate_activations(z, col_block=gate_col0 // LANES, n_raw=MH, tm=tm)
        gr = jnp.swapaxes(gc[:, :GATE_ROWS].reshape(n_seq, rows, GATE_ROWS), 1, 2)
        return gc, gr

    outs_p = [[] for _ in range(9)]
    outs_s = [[] for _ in range(7)]
    for l in range(depth):
        zp, qt_p, kt_p, vt_p = in_proj_prompt(xp, g_mix[l], w_in_p[l], b_in_p[l], w_fox_t[l], b_fox[l],
                                              n_seq=B, tm=tm_in, tn=tn_p)
        gcp, grp = gates_of(zp, gate_col0_p, B, S, tm_p)
        r_p, S_p = retention(zp, cos_p, sin_p, zeros(B, RH, HEAD_DIM, HEAD_DIM), g_ret[l],
                             n_seq=B, L=CHUNK, valid=CHUNK, col0=ret_col0, log_gamma=log_gamma,
                             seqs=seqs_p)
        m_p, C_p, n_p, mm_p = mlstm(zp, gcp, grp, zeros(B, MH, HEAD_DIM, HEAD_DIM), zeros(B, MH, HEAD_DIM),
                                    zeros(B, MH), g_mlstm[l], n_seq=B, L=CHUNK, valid=CHUNK,
                                    col0=mls_col0, n_heads=MH, seqs=1)
        ka, qa, vb = fox_prep(gcp, grp, zp, qt_p, vt_p, n_seq=B, L=l_prep, k_col0=fox_col0,
                              n_heads=FH, g0=fox_g0)
        f_p = fox_flash(ka, qa, vb, tq=tq, n_heads=FH)
        xp = out_proj(xp, r_p, m_p, f_p, w_out_b[l], tm=tm_p, n_seq=B)
        mkv_t = norm_matmul_t(mem, g_mem[l], w_ckv_t[l], n_seq=B)
        xp = cross_attn_prompt(xp, g_cross[l], w_cq_b[l], mkv_t, w_co_b[l], n_heads=MEMH,
                               tm=_pick(S, (512, 256, 128)))
        xp = sq_relu_mlp(xp, g_mlp[l], w_up_b[l], w_down_b[l], tm=tm_p, tf=tf)

        zs = norm_matmul(xs, g_mix[l], w_in_s[l], b_in_s[l], tm=tm_s, tn=tn_s)
        gcs, grs = gates_of(zs, gate_col0_s, NB, TP, tm_s)
        r_s, S_s = retention(zs, cos_s, sin_s, state_ret[l], g_ret[l], n_seq=NB, L=TP, valid=T,
                             col0=ret_col0, log_gamma=log_gamma, seqs=seqs)
        m_s, C_s, n_s, mm_s = mlstm(zs, gcs, grs, state_mlstm_c[l], state_mlstm_n[l], state_mlstm_m[l],
                                    g_mlstm[l], n_seq=NB, L=TP, valid=T, col0=mls_col0, n_heads=MH,
                                    seqs=seqs)
        f_s = fox_decode(zs, gcs, grs, cache_kt, cache_vt, cache_lf_t, page_table, layer=l, valid=T,
                         col0=fox_col0, n_heads=FH, g0=fox_g0)
        xs = out_proj(xs, r_s, m_s, f_s, w_out_b[l], tm=tm_s)
        xs = cross_attn_sample(xs, g_cross[l], w_cq_b[l], mem_kt, mem_vt, w_co_b[l], layer=l,
                               n_heads=MEMH, seqs=seqs)
        xs = sq_relu_mlp(xs, g_mlp[l], w_up_b[l], w_down_b[l], tm=tm_s, tf=tf)

        heads_last = lambda a, n: jnp.transpose(a.reshape(a.shape[0], n, HEAD_DIM, a.shape[-1]), (0, 3, 1, 2))
        fks = lambda c: zs[:, fox_col0 + c * FW:fox_col0 + (c + 1) * FW].reshape(NB, TP, FH, HEAD_DIM)[:, :T]
        new_p = (S_p, C_p, n_p, mm_p,
                 heads_last(kt_p, FH), heads_last(vt_p, FH),
                 gcp[:, fox_g0:fox_g0 + FH].reshape(B, S, FH),
                 heads_last(mkv_t[:, :MEMW], MEMH), heads_last(mkv_t[:, MEMW:], MEMH))
        new_s = (S_s, C_s, n_s, mm_s, fks(1), fks(2),
                 gcs[:, fox_g0:fox_g0 + FH].reshape(NB, TP, FH)[:, :T])
        for lst, a in zip(outs_p, new_p):
            lst.append(a)
        for lst, a in zip(outs_s, new_s):
            lst.append(a)

    y_prompt = final_norm(xp, g_final, tm=tm_p).reshape(B, S, D)
    y_sample = final_norm(xs, g_final, tm=tm_s).reshape(NB, TP, D)[:, :T]
    ret_p, c_p, n_p, m_p, fk_p, fv_p, flf_p, memk_p, memv_p = [jnp.stack(a, axis=0) for a in outs_p]
    ret_s, c_s, n_s, m_s, fk_s, fv_s, flf_s = [jnp.stack(a, axis=0) for a in outs_s]
    return (y_prompt, y_sample, ret_p, ret_s, c_p, c_s, n_p, n_s, m_p, m_s,
            fk_p, fk_s, fv_p, fv_s, flf_p, flf_s, memk_p, memv_p)
```

```python
import functools
import math

import numpy as np
import jax
import jax.numpy as jnp
from jax import lax
from jax.experimental import pallas as pl
from jax.experimental.pallas import tpu as pltpu

F32 = jnp.float32
BF16 = jnp.bfloat16

HEAD_DIM = 64
HALF = HEAD_DIM // 2
EPS = 1e-6
ROPE_BASE = 10000.0
CHUNK = 128
SUBLANES = 8
LANES = 128
GATE_ROWS = 16
AUG = 2 * HEAD_DIM
SCALE = HEAD_DIM ** -0.5
VMEM_LIMIT = 56 * 1024 * 1024
NEG_INF = float("-inf")


def _cparams(*sem):
    return pltpu.CompilerParams(dimension_semantics=sem, vmem_limit_bytes=VMEM_LIMIT)


def _dot(a, b):
    return jnp.dot(a.astype(BF16), b.astype(BF16), preferred_element_type=F32)


def _dot_nt(a, b):
    return lax.dot_general(a.astype(BF16), b.astype(BF16), (((1,), (1,)), ((), ())),
                           preferred_element_type=F32)


def _dot_tn(a, b):
    return lax.dot_general(a.astype(BF16), b.astype(BF16), (((0,), (0,)), ((), ())),
                           preferred_element_type=F32)


def _split3(x):
    hi = x.astype(BF16)
    r1 = x - hi.astype(F32)
    mid = r1.astype(BF16)
    lo = (r1 - mid.astype(F32)).astype(BF16)
    return hi, mid, lo


def _cumsum_rows(x, n):
    t = lax.broadcasted_iota(jnp.int32, (n, n), 0)
    s = lax.broadcasted_iota(jnp.int32, (n, n), 1)
    tri = jnp.where(t >= s, 1.0, 0.0).astype(BF16)
    return sum(jnp.dot(tri, p, preferred_element_type=F32) for p in _split3(x))


def _cumsum_lanes(x, n):
    t = lax.broadcasted_iota(jnp.int32, (n, n), 0)
    s = lax.broadcasted_iota(jnp.int32, (n, n), 1)
    tri = jnp.where(t <= s, 1.0, 0.0).astype(BF16)
    return sum(jnp.dot(p, tri, preferred_element_type=F32) for p in _split3(x))


def _rms(x, g):
    return x * lax.rsqrt(jnp.mean(x * x, axis=-1, keepdims=True) + EPS) * g


def _sigmoid(x):
    return 1.0 / (1.0 + jnp.exp(-x))


def _log_sigmoid(x):
    return jnp.minimum(x, 0.0) - jnp.log1p(jnp.exp(-jnp.abs(x)))


def _head_norm(o):
    return o * lax.rsqrt(jnp.mean(o * o, axis=-1, keepdims=True) + EPS)


def _softmax_rows(s):
    p = jnp.exp(s - jnp.max(s, axis=1, keepdims=True))
    return p / jnp.sum(p, axis=1, keepdims=True)


def _head_slice(h):
    return slice(h * HEAD_DIM, (h + 1) * HEAD_DIM)


def _norm_matmul_kernel(x_ref, g_ref, w_ref, b_ref, o_ref, xn_ref):
    @pl.when(pl.program_id(1) == 0)
    def _():
        xn_ref[...] = _rms(x_ref[...], g_ref[...]).astype(BF16)

    o_ref[...] = jnp.dot(xn_ref[...], w_ref[...], preferred_element_type=F32) + b_ref[...]


def norm_matmul(x, g, w, b, *, tm, tn):
    M, D = x.shape
    N = w.shape[1]
    return pl.pallas_call(
        _norm_matmul_kernel,
        out_shape=jax.ShapeDtypeStruct((M, N), F32),
        grid=(M // tm, N // tn),
        in_specs=[pl.BlockSpec((tm, D), lambda i, j: (i, 0)),
                  pl.BlockSpec((1, D), lambda i, j: (0, 0)),
                  pl.BlockSpec((D, tn), lambda i, j: (0, j)),
                  pl.BlockSpec((1, tn), lambda i, j: (0, j))],
        out_specs=pl.BlockSpec((tm, tn), lambda i, j: (i, j)),
        scratch_shapes=[pltpu.VMEM((tm, D), BF16)],
        compiler_params=_cparams("parallel", "arbitrary"),
        name="norm_matmul",
    )(x, g.reshape(1, D), w, b.reshape(1, N))


def _in_proj_prompt_kernel(x_ref, g_ref, w_ref, b_ref, wt_ref, bt_ref, z_ref, qt_ref, kt_ref, vt_ref,
                           xn_ref, *, fw):
    @pl.when(pl.program_id(2) == 0)
    def _():
        xn_ref[...] = _rms(x_ref[...], g_ref[...]).astype(BF16)

    z_ref[...] = jnp.dot(xn_ref[...], w_ref[...], preferred_element_type=F32) + b_ref[...]

    @pl.when(pl.program_id(2) == pl.num_programs(2) - 1)
    def _():
        zt = lax.dot_general(wt_ref[...], xn_ref[...], (((1,), (1,)), ((), ())),
                             preferred_element_type=F32) + bt_ref[...]
        qt_ref[...] = zt[:fw]
        kt_ref[...] = zt[fw:2 * fw]
        vt_ref[...] = zt[2 * fw:]


def in_proj_prompt(x, g, w, b, wt, bt, *, n_seq, tm, tn):
    M, D = x.shape
    N = w.shape[1]
    S = M // n_seq
    fw = wt.shape[0] // 3
    nt = S // tm
    t_out = jax.ShapeDtypeStruct((n_seq, fw, S), F32)
    t_spec = pl.BlockSpec((None, fw, tm), lambda s, i, j: (s, 0, i))
    return pl.pallas_call(
        functools.partial(_in_proj_prompt_kernel, fw=fw),
        out_shape=(jax.ShapeDtypeStruct((M, N), F32), t_out, t_out, t_out),
        grid=(n_seq, nt, N // tn),
        in_specs=[pl.BlockSpec((tm, D), lambda s, i, j: (s * nt + i, 0)),
                  pl.BlockSpec((1, D), lambda s, i, j: (0, 0)),
                  pl.BlockSpec((D, tn), lambda s, i, j: (0, j)),
                  pl.BlockSpec((1, tn), lambda s, i, j: (0, j)),
                  pl.BlockSpec((3 * fw, D), lambda s, i, j: (0, 0)),
                  pl.BlockSpec((3 * fw, 1), lambda s, i, j: (0, 0))],
        out_specs=(pl.BlockSpec((tm, tn), lambda s, i, j: (s * nt + i, j)), t_spec, t_spec, t_spec),
        scratch_shapes=[pltpu.VMEM((tm, D), BF16)],
        compiler_params=_cparams("parallel", "parallel", "arbitrary"),
        name="in_proj_prompt",
    )(x, g.reshape(1, D), w, b.reshape(1, N), wt, bt.reshape(3 * fw, 1))


def _norm_matmul_t_kernel(x_ref, g_ref, wt_ref, o_ref):
    xn = _rms(x_ref[...], g_ref[...]).astype(BF16)
    o_ref[...] = lax.dot_general(wt_ref[...], xn, (((1,), (1,)), ((), ())), preferred_element_type=F32)


def norm_matmul_t(x, g, wt, *, n_seq):
    M, D = x.shape
    N = wt.shape[0]
    R = M // n_seq
    return pl.pallas_call(
        _norm_matmul_t_kernel,
        out_shape=jax.ShapeDtypeStruct((n_seq, N, R), F32),
        grid=(n_seq,),
        in_specs=[pl.BlockSpec((R, D), lambda s: (s, 0)),
                  pl.BlockSpec((1, D), lambda s: (0, 0)),
                  pl.BlockSpec((N, D), lambda s: (0, 0))],
        out_specs=pl.BlockSpec((None, N, R), lambda s: (s, 0, 0)),
        compiler_params=_cparams("parallel"),
        name="norm_matmul_t",
    )(x, g.reshape(1, D), wt)


def _gates_kernel(z_ref, o_ref, *, n_raw):
    z = z_ref[...]
    lane = lax.broadcasted_iota(jnp.int32, z.shape, 1)
    o_ref[...] = jnp.where(lane >= n_raw, _log_sigmoid(z), z)


def gate_activations(z, *, col_block, n_raw, tm):
    M = z.shape[0]
    return pl.pallas_call(
        functools.partial(_gates_kernel, n_raw=n_raw),
        out_shape=jax.ShapeDtypeStruct((M, LANES), F32),
        grid=(M // tm,),
        in_specs=[pl.BlockSpec((tm, LANES), lambda i: (i, col_block))],
        out_specs=pl.BlockSpec((tm, LANES), lambda i: (i, 0)),
        compiler_params=_cparams("parallel"),
        name="gate_activations",
    )(z)


def _out_proj_kernel(x_ref, r_ref, m_ref, f_ref, wr_ref, wm_ref, wf_ref, o_ref):
    o_ref[...] = (x_ref[...] + _dot(r_ref[...], wr_ref[...]) + _dot(m_ref[...], wm_ref[...])
                  + _dot(f_ref[...], wf_ref[...]))


def out_proj(x, r, m, f, w_out, *, tm):
    M, D = x.shape
    wr, wm = r.shape[1], m.shape[1]
    w_r, w_m, w_f = w_out[:wr], w_out[wr:wr + wm], w_out[wr + wm:]
    row = lambda w: pl.BlockSpec((tm, w), lambda i: (i, 0))
    full = lambda a: pl.BlockSpec(a.shape, lambda i: (0, 0))
    return pl.pallas_call(
        _out_proj_kernel,
        out_shape=jax.ShapeDtypeStruct((M, D), F32),
        grid=(M // tm,),
        in_specs=[row(D), row(wr), row(wm), row(f.shape[1]), full(w_r), full(w_m), full(w_f)],
        out_specs=row(D),
        compiler_params=_cparams("parallel"),
        name="out_proj",
    )(x, r, m, f, w_r, w_m, w_f)


def _mlp_kernel(x_ref, g_ref, wu_ref, wd_ref, o_ref, xn_ref, acc_ref):
    j = pl.program_id(1)

    @pl.when(j == 0)
    def _():
        xn_ref[...] = _rms(x_ref[...], g_ref[...]).astype(BF16)
        acc_ref[...] = x_ref[...]

    h = jnp.maximum(jnp.dot(xn_ref[...], wu_ref[...], preferred_element_type=F32), 0.0)
    acc_ref[...] += jnp.dot((h * h).astype(BF16), wd_ref[...], preferred_element_type=F32)

    @pl.when(j == pl.num_programs(1) - 1)
    def _():
        o_ref[...] = acc_ref[...]


def sq_relu_mlp(x, g, w_up, w_down, *, tm, tf):
    M, D = x.shape
    FF = w_up.shape[1]
    return pl.pallas_call(
        _mlp_kernel,
        out_shape=jax.ShapeDtypeStruct((M, D), F32),
        grid=(M // tm, FF // tf),
        in_specs=[pl.BlockSpec((tm, D), lambda i, j: (i, 0)),
                  pl.BlockSpec((1, D), lambda i, j: (0, 0)),
                  pl.BlockSpec((D, tf), lambda i, j: (0, j)),
                  pl.BlockSpec((tf, D), lambda i, j: (j, 0))],
        out_specs=pl.BlockSpec((tm, D), lambda i, j: (i, 0)),
        scratch_shapes=[pltpu.VMEM((tm, D), BF16), pltpu.VMEM((tm, D), F32)],
        compiler_params=_cparams("parallel", "arbitrary"),
        name="sq_relu_mlp",
    )(x, g.reshape(1, D), w_up, w_down)


def _final_norm_kernel(x_ref, g_ref, o_ref):
    o_ref[...] = _rms(x_ref[...], g_ref[...])


def final_norm(x, g, *, tm):
    M, D = x.shape
    return pl.pallas_call(
        _final_norm_kernel,
        out_shape=jax.ShapeDtypeStruct((M, D), F32),
        grid=(M // tm,),
        in_specs=[pl.BlockSpec((tm, D), lambda i: (i, 0)), pl.BlockSpec((1, D), lambda i: (0, 0))],
        out_specs=pl.BlockSpec((tm, D), lambda i: (i, 0)),
        compiler_params=_cparams("parallel"),
        name="final_norm",
    )(x, g.reshape(1, D))


def _retention_kernel(q_ref, k_ref, v_ref, g_ref, cos_ref, sin_ref, s0_ref, gain_ref,
                      y_ref, sout_ref, s_sc, *, L, valid, log_gamma, seqs):
    c = pl.program_id(1)
    n_heads = len(log_gamma)
    W = n_heads * HEAD_DIM
    P = n_heads // 2
    HD = HEAD_DIM
    upper = lax.broadcasted_iota(jnp.int32, (1, LANES), 1) >= HD
    r_i = lax.broadcasted_iota(jnp.int32, (LANES, LANES), 0)
    c_i = lax.broadcasted_iota(jnp.int32, (LANES, LANES), 1)
    same_half = (r_i >= HD) == (c_i >= HD)
    half_ones = jnp.where(same_half, 1.0, 0.0).astype(BF16)

    @pl.when(c == 0)
    def _():
        zero = jnp.zeros((HD, HD), F32)
        for j in range(seqs):
            for p in range(P):
                s_sc[j, p] = jnp.concatenate(
                    [jnp.concatenate([s0_ref[j, 2 * p], zero], axis=1),
                     jnp.concatenate([zero, s0_ref[j, 2 * p + 1]], axis=1)], axis=0)

    cos = cos_ref[...]
    sin = sin_ref[...]
    lane = lax.broadcasted_iota(jnp.int32, (L, W), 1)
    first_half = (lane % HD) < HALF

    def rope(x):
        swapped = jnp.where(first_half, pltpu.roll(x, W - HALF, 1), pltpu.roll(x, HALF, 1))
        return x * cos + swapped * sin

    t_i = lax.broadcasted_iota(jnp.int32, (L, L), 0)
    s_i = lax.broadcasted_iota(jnp.int32, (L, L), 1)
    causal = t_i >= s_i
    diff = jnp.where(causal, (t_i - s_i).astype(F32), 0.0)
    decay = [jnp.where(causal, jnp.exp(lg * diff), 0.0) for lg in log_gamma]
    t_rep = lax.broadcasted_iota(jnp.int32, (L, LANES), 0).astype(F32)
    pair_lg = [jnp.where(upper, log_gamma[2 * p + 1], log_gamma[2 * p]) for p in range(P)]
    q_decay = [jnp.exp(lg * (t_rep + 1.0)) for lg in pair_lg]
    k_decay = [jnp.where(t_rep < valid, jnp.exp(lg * (valid - 1.0 - t_rep)), 0.0) for lg in pair_lg]
    s_decay = [jnp.exp(lg * float(valid)) for lg in pair_lg]

    pairs = [(j, p) for j in range(seqs) for p in range(P)]
    q, k, v = {}, {}, {}
    for j in range(seqs):
        qj = rope(q_ref[j])
        kj = rope(k_ref[j]) * SCALE
        for p in range(P):
            tile = slice(p * LANES, (p + 1) * LANES)
            q[j, p], k[j, p], v[j, p] = qj[:, tile], kj[:, tile], v_ref[j, :, tile]
    scores, cross = {}, {}
    for j, p in pairs:
        qb = q[j, p].astype(BF16)
        for e in range(2):
            scores[j, 2 * p + e] = _dot_nt(qb, jnp.where(upper if e else ~upper, k[j, p], 0.0))
        cross[j, p] = _dot(qb, s_sc[j, p])
    outs = {}
    for j, p in pairs:
        lhs = jnp.concatenate([scores[j, 2 * p] * decay[2 * p], scores[j, 2 * p + 1] * decay[2 * p + 1]],
                              axis=1)
        v_sel = jnp.concatenate([jnp.where(upper, 0.0, v[j, p]), jnp.where(upper, v[j, p], 0.0)], axis=0)
        o = _dot(lhs, v_sel) + cross[j, p] * q_decay[p]
        o2 = o * o
        hi = o2.astype(BF16)
        lo = (o2 - hi.astype(F32)).astype(BF16)
        mean_sq = (jnp.dot(hi, half_ones, preferred_element_type=F32)
                   + jnp.dot(lo, half_ones, preferred_element_type=F32)) * (1.0 / HD)
        outs[j, p] = o * lax.rsqrt(mean_sq + EPS)
    for j, p in pairs:
        update = jnp.where(same_half, _dot_tn(k[j, p] * k_decay[p], v[j, p]), 0.0)
        s_sc[j, p] = s_decay[p] * s_sc[j, p] + update
    for j in range(seqs):
        g = g_ref[j]
        y_ref[j] = jnp.concatenate([outs[j, p] for p in range(P)], axis=-1) * gain_ref[...] * (
            g * _sigmoid(g))

    @pl.when(c == pl.num_programs(1) - 1)
    def _():
        for j in range(seqs):
            for p in range(P):
                blk = s_sc[j, p]
                sout_ref[j, 2 * p] = blk[:HD, :HD]
                sout_ref[j, 2 * p + 1] = blk[HD:, HD:]


def retention(z, cos, sin, s0, gain, *, n_seq, L, valid, col0, log_gamma, seqs):
    M, N = z.shape
    n_heads = len(log_gamma)
    W = n_heads * HEAD_DIM
    rows = M // n_seq
    cb = col0 // W
    z3 = z.reshape(n_seq, rows, N)
    zspec = lambda j: pl.BlockSpec((seqs, L, W), lambda b, c: (b, c, cb + j))
    tab = pl.BlockSpec((L, W), lambda b, c: (c, 0))
    st = pl.BlockSpec((seqs, n_heads, HEAD_DIM, HEAD_DIM), lambda b, c: (b, 0, 0, 0))
    y, s_out = pl.pallas_call(
        functools.partial(_retention_kernel, L=L, valid=valid, log_gamma=log_gamma, seqs=seqs),
        out_shape=(jax.ShapeDtypeStruct((n_seq, rows, W), F32),
                   jax.ShapeDtypeStruct((n_seq, n_heads, HEAD_DIM, HEAD_DIM), F32)),
        grid=(n_seq // seqs, rows // L),
        in_specs=[zspec(0), zspec(1), zspec(2), zspec(3), tab, tab, st,
                  pl.BlockSpec((1, W), lambda b, c: (0, 0))],
        out_specs=(pl.BlockSpec((seqs, L, W), lambda b, c: (b, c, 0)), st),
        scratch_shapes=[pltpu.VMEM((seqs, n_heads // 2, LANES, LANES), F32)],
        compiler_params=_cparams("parallel", "arbitrary"),
        name="retention",
    )(z3, z3, z3, z3, cos, sin, s0, gain.reshape(1, W))
    return y.reshape(M, W), s_out


def _cumsum_rows_any(x, n):
    if n > SUBLANES:
        return _cumsum_rows(x, n)
    t = lax.broadcasted_iota(jnp.int32, x.shape, 0)
    return sum(jnp.where(t >= u, x[u:u + 1, :], 0.0) for u in range(n))


def _cumsum_lanes_any(x, n):
    if n > SUBLANES:
        return _cumsum_lanes(x, n)
    s = lax.broadcasted_iota(jnp.int32, x.shape, 1)
    return sum(jnp.where(s >= u, x[:, u:u + 1], 0.0) for u in range(n))


def _cummax_rows(x, n):
    row = lax.broadcasted_iota(jnp.int32, x.shape, 0)
    shift = 1
    while shift < n:
        x = jnp.maximum(x, jnp.where(row >= shift, pltpu.roll(x, shift, 0), NEG_INF))
        shift *= 2
    return x


def _mlstm_kernel(q_ref, k_ref, v_ref, og_ref, gc_ref, gr_ref, c0_ref, n0_ref, m0_ref, gain_ref,
                  y_ref, cout_ref, nout_ref, mout_ref, c_sc, n_sc, m_sc, *, L, valid, n_heads, seqs):
    c = pl.program_id(1)
    P = n_heads // 2
    HD = HEAD_DIM
    lane = lax.broadcasted_iota(jnp.int32, (1, LANES), 1)
    upper = lane >= HD
    r_i = lax.broadcasted_iota(jnp.int32, (LANES, LANES), 0)
    c_i = lax.broadcasted_iota(jnp.int32, (LANES, LANES), 1)
    same_half = (r_i >= HD) == (c_i >= HD)
    half_ones = jnp.where(same_half, 1.0, 0.0).astype(BF16)

    @pl.when(c == 0)
    def _():
        zero = jnp.zeros((HD, HD), F32)
        for j in range(seqs):
            for p in range(P):
                c_sc[j, p] = jnp.concatenate(
                    [jnp.concatenate([c0_ref[j, 2 * p], zero], axis=1),
                     jnp.concatenate([zero, c0_ref[j, 2 * p + 1]], axis=1)], axis=0)
                n_sc[j, p] = jnp.concatenate([n0_ref[j, 2 * p:2 * p + 1, :], n0_ref[j, 2 * p + 1:2 * p + 2, :]],
                                             axis=1)
        m_sc[...] = m0_ref[...]

    t_i = lax.broadcasted_iota(jnp.int32, (L, L), 0)
    s_i = lax.broadcasted_iota(jnp.int32, (L, L), 1)
    causal = t_i >= s_i
    row_valid = lax.broadcasted_iota(jnp.int32, (L, LANES), 0) < valid
    rr = lax.broadcasted_iota(jnp.int32, (2 * L, LANES), 0)
    cc = lax.broadcasted_iota(jnp.int32, (2 * L, LANES), 1)
    sum_sel = jnp.where((rr >= L) == (cc >= HD), 1.0, 0.0)

    pairs = [(j, p) for j in range(seqs) for p in range(P)]
    heads = [(j, h) for j in range(seqs) for h in range(n_heads)]
    q, k, v, og = {}, {}, {}, {}
    f_rep, i_rep, w_row, m_prev = {}, {}, {}, {}
    for j in range(seqs):
        gc = gc_ref[j]
        gr = gr_ref[j]
        f_col = _cumsum_rows_any(gc, L)
        f_row = _cumsum_lanes_any(gr, L)
        for p in range(P):
            tile = slice(p * LANES, (p + 1) * LANES)
            q[j, p] = q_ref[j, :, tile]
            k[j, p] = k_ref[j, :, tile] * SCALE
            v[j, p] = v_ref[j, :, tile]
            og[j, p] = _sigmoid(og_ref[j, :, tile])
        for h in range(n_heads):
            f_rep[j, h] = jnp.broadcast_to(f_col[:, n_heads + h:n_heads + h + 1], (L, LANES))
            i_rep[j, h] = jnp.broadcast_to(gc[:, h:h + 1], (L, LANES))
            w_row[j, h] = f_row[n_heads + h:n_heads + h + 1, :] - gr[h:h + 1, :]
            m_prev[j, h] = m_sc[j, h:h + 1, :]

    qk_raw, cq, nq = {}, {}, {}
    for j, p in pairs:
        qb = q[j, p].astype(BF16)
        for e in range(2):
            k_half = jnp.where(upper if e else ~upper, k[j, p], 0.0)
            qk_raw[j, 2 * p + e] = _dot_nt(qb, k_half)
        cq[j, p] = _dot_nt(qb, c_sc[j, p])
        n_rep = jnp.where(same_half, jnp.broadcast_to(n_sc[j, p], (LANES, LANES)), 0.0)
        nq[j, p] = _dot_nt(qb, n_rep)

    m_t, u, w_inter, e_rep = {}, {}, {}, {}
    for j, h in heads:
        peak = jnp.maximum(m_prev[j, h], _cummax_rows(i_rep[j, h] - f_rep[j, h], L))
        m_t[j, h] = f_rep[j, h] + peak
        u[j, h] = f_rep[j, h] - m_t[j, h]
        w_inter[j, h] = jnp.exp(u[j, h] + m_prev[j, h])
        e_rep[j, h] = jnp.exp(-m_t[j, h])

    outs = {}
    for j, p in pairs:
        h0, h1 = 2 * p, 2 * p + 1
        gated = [qk_raw[j, h] * jnp.where(causal, jnp.exp(u[j, h][:, :L] - w_row[j, h]), 0.0)
                 for h in (h0, h1)]
        lhs = jnp.concatenate(gated, axis=1)
        v_sel = jnp.concatenate([jnp.where(upper, 0.0, v[j, p]), jnp.where(upper, v[j, p], 0.0)], axis=0)
        mix = _dot(lhs, jnp.concatenate([v_sel, sum_sel], axis=1))
        w_pair = jnp.where(upper, w_inter[j, h1], w_inter[j, h0])
        num = mix[:, :LANES] + w_pair * cq[j, p]
        den = mix[:, LANES:] + w_pair * nq[j, p]
        floor = jnp.where(upper, e_rep[j, h1], e_rep[j, h0])
        y = og[j, p] * (num / jnp.maximum(jnp.abs(den), floor))
        y2 = y * y
        hi = y2.astype(BF16)
        lo = (y2 - hi.astype(F32)).astype(BF16)
        mean_sq = (jnp.dot(hi, half_ones, preferred_element_type=F32)
                   + jnp.dot(lo, half_ones, preferred_element_type=F32)) * (1.0 / HD)
        outs[j, p] = y * lax.rsqrt(mean_sq + EPS)

    for j, p in pairs:
        h0, h1 = 2 * p, 2 * p + 1
        w_s, w_c, m_last = {}, {}, {}
        for h in (h0, h1):
            m_last[h] = m_t[j, h][valid - 1:valid, :]
            f_last = f_rep[j, h][valid - 1:valid, :]
            w_s[h] = jnp.where(row_valid, jnp.exp(f_last - f_rep[j, h] + i_rep[j, h] - m_last[h]), 0.0)
            w_c[h] = jnp.exp(f_last + m_prev[j, h] - m_last[h])
        ws_pair = jnp.where(upper, w_s[h1], w_s[h0])
        wc_pair = jnp.where(upper, w_c[h1], w_c[h0])
        update = jnp.where(same_half, _dot_tn(v[j, p] * ws_pair, k[j, p]), 0.0)
        c_sc[j, p] = wc_pair * c_sc[j, p] + update
        n_sc[j, p] = wc_pair * n_sc[j, p] + jnp.sum(k[j, p] * ws_pair, axis=0, keepdims=True)
        m_sc[j, h0:h0 + 1, :] = m_last[h0]
        m_sc[j, h1:h1 + 1, :] = m_last[h1]
    for j in range(seqs):
        y_ref[j] = jnp.concatenate([outs[j, p] for p in range(P)], axis=-1) * gain_ref[...]

    @pl.when(c == pl.num_programs(1) - 1)
    def _():
        for j in range(seqs):
            for p in range(P):
                blk = c_sc[j, p]
                row = n_sc[j, p]
                cout_ref[j, 2 * p] = blk[:HD, :HD]
                cout_ref[j, 2 * p + 1] = blk[HD:, HD:]
                nout_ref[j, 2 * p:2 * p + 1, :] = row[:, :HD]
                nout_ref[j, 2 * p + 1:2 * p + 2, :] = row[:, HD:]
        mout_ref[...] = m_sc[...]


def mlstm(z, gc, gr, c0, n0, m0, gain, *, n_seq, L, valid, col0, n_heads, seqs):
    M, N = z.shape
    W = n_heads * HEAD_DIM
    P = n_heads // 2
    assert n_heads % 2 == 0 and L <= LANES
    rows = M // n_seq
    cb = col0 // W
    z3 = z.reshape(n_seq, rows, N)
    zspec = lambda j: pl.BlockSpec((seqs, L, W), lambda b, c: (b, c, cb + j))
    cst = pl.BlockSpec((seqs, n_heads, HEAD_DIM, HEAD_DIM), lambda b, c: (b, 0, 0, 0))
    nst = pl.BlockSpec((seqs, n_heads, HEAD_DIM), lambda b, c: (b, 0, 0))
    mst = pl.BlockSpec((seqs, n_heads, LANES), lambda b, c: (b, 0, 0))
    m0_rep = jnp.broadcast_to(m0[:, :, None], (n_seq, n_heads, LANES))
    y, c_out, n_out, m_out = pl.pallas_call(
        functools.partial(_mlstm_kernel, L=L, valid=valid, n_heads=n_heads, seqs=seqs),
        out_shape=(jax.ShapeDtypeStruct((n_seq, rows, W), F32),
                   jax.ShapeDtypeStruct((n_seq, n_heads, HEAD_DIM, HEAD_DIM), F32),
                   jax.ShapeDtypeStruct((n_seq, n_heads, HEAD_DIM), F32),
                   jax.ShapeDtypeStruct((n_seq, n_heads, LANES), F32)),
        grid=(n_seq // seqs, rows // L),
        in_specs=[zspec(0), zspec(1), zspec(2), zspec(3),
                  pl.BlockSpec((seqs, L, LANES), lambda b, c: (b, c, 0)),
                  pl.BlockSpec((seqs, GATE_ROWS, L), lambda b, c: (b, 0, c)),
                  cst, nst, mst,
                  pl.BlockSpec((1, W), lambda b, c: (0, 0))],
        out_specs=(pl.BlockSpec((seqs, L, W), lambda b, c: (b, c, 0)), cst, nst, mst),
        scratch_shapes=[pltpu.VMEM((seqs, P, LANES, LANES), F32),
                        pltpu.VMEM((seqs, P, 1, LANES), F32),
                        pltpu.VMEM((seqs, n_heads, LANES), F32)],
        compiler_params=_cparams("parallel", "arbitrary"),
        name="mlstm",
    )(z3, z3, z3, z3, gc.reshape(n_seq, rows, LANES), gr, c0, n0, m0_rep, gain.reshape(1, W))
    return y.reshape(M, W), c_out, n_out, m_out[:, :, 0]


def _fox_prep_kernel(gc_ref, gr_ref, k_ref, qt_ref, vt_ref, ka_ref, qa_ref, vb_ref, cc_sc, cr_sc,
                     *, L, n_heads, g0):
    @pl.when(pl.program_id(1) == 0)
    def _():
        cc_sc[...] = jnp.zeros_like(cc_sc)
        cr_sc[...] = jnp.zeros_like(cr_sc)

    fc = _cumsum_rows(gc_ref[...], L) + cc_sc[...]
    fr = _cumsum_lanes(gr_ref[...], L) + cr_sc[...]
    cc_sc[...] = fc[L - 1:L, :]
    cr_sc[...] = fr[:, L - 1:L]
    qt = qt_ref[...] * SCALE
    sel_r = lax.broadcasted_iota(jnp.int32, (3 * LANES, n_heads * AUG), 0)
    sel_c = lax.broadcasted_iota(jnp.int32, (3 * LANES, n_heads * AUG), 1)
    place = ((sel_r % LANES - g0) == (sel_c // AUG)) & ((sel_c % AUG) == HEAD_DIM + 3 + sel_r // LANES)
    k_aug = jnp.dot(jnp.concatenate(_split3(fc), axis=1), jnp.where(place, -1.0, 0.0).astype(BF16),
                    preferred_element_type=F32)
    lane_in_tile = lax.broadcasted_iota(jnp.int32, (1, n_heads * AUG), 1) % AUG
    k_aug = k_aug + jnp.where((lane_in_tile >= HEAD_DIM) & (lane_in_tile < HEAD_DIM + 3), 1.0, 0.0)
    low = lax.broadcasted_iota(jnp.int32, (1, AUG), 1) < HEAD_DIM
    for t in range(n_heads // 2):
        k_pair = k_ref[:, t * LANES:(t + 1) * LANES]
        for e, k_low in enumerate((k_pair, pltpu.roll(k_pair, HEAD_DIM, 1))):
            h = 2 * t + e
            ka_ref[:, h * AUG:(h + 1) * AUG] = jnp.where(low, k_low, k_aug[:, h * AUG:(h + 1) * AUG]).astype(BF16)
    row = lax.broadcasted_iota(jnp.int32, (HEAD_DIM, L), 0)
    for h in range(n_heads):
        sl = _head_slice(h)
        qh, qm, ql = (p.astype(F32) for p in _split3(fr[g0 + h:g0 + h + 1, :]))
        q_aug = jnp.where(row == 0, qh,
                          jnp.where(row == 1, qm, jnp.where(row == 2, ql, jnp.where(row < 6, 1.0, 0.0))))
        qa_ref[h * AUG:(h + 1) * AUG, :] = jnp.concatenate([qt[sl, :], q_aug], axis=0).astype(BF16)
    vb_ref[...] = vt_ref[...].astype(BF16)


def fox_prep(gc, gr, z, qt, vt, *, n_seq, L, k_col0, n_heads, g0):
    M = gc.shape[0]
    S = M // n_seq
    nc = S // L
    W = n_heads * HEAD_DIM
    t_spec = pl.BlockSpec((None, W, L), lambda b, c: (b, 0, c))
    return pl.pallas_call(
        functools.partial(_fox_prep_kernel, L=L, n_heads=n_heads, g0=g0),
        out_shape=(jax.ShapeDtypeStruct((M, n_heads * AUG), BF16),
                   jax.ShapeDtypeStruct((n_seq, n_heads * AUG, S), BF16),
                   jax.ShapeDtypeStruct((n_seq, W, S), BF16)),
        grid=(n_seq, nc),
        in_specs=[pl.BlockSpec((L, LANES), lambda b, c: (b * nc + c, 0)),
                  pl.BlockSpec((None, GATE_ROWS, L), lambda b, c: (b, 0, c)),
                  pl.BlockSpec((L, W), lambda b, c: (b * nc + c, k_col0 // W)),
                  t_spec, t_spec],
        out_specs=(pl.BlockSpec((L, n_heads * AUG), lambda b, c: (b * nc + c, 0)),
                   pl.BlockSpec((None, n_heads * AUG, L), lambda b, c: (b, 0, c)),
                   t_spec),
        scratch_shapes=[pltpu.VMEM((1, LANES), F32), pltpu.VMEM((GATE_ROWS, 1), F32)],
        compiler_params=_cparams("parallel", "arbitrary"),
        name="fox_prep",
    )(gc, gr, z, qt, vt)


def _fox_flash_kernel(qi_ref, kj_ref, ka_ref, qa_ref, vt_ref, o_ref, m_sc, l_sc, acc_sc, s_sc, p_sc,
                      *, tq, n_heads):
    step = pl.program_id(1)
    qi = qi_ref[step]
    kj = kj_ref[step]

    @pl.when(kj == 0)
    def _():
        m_sc[...] = jnp.full_like(m_sc, NEG_INF)
        l_sc[...] = jnp.zeros_like(l_sc)
        acc_sc[...] = jnp.zeros_like(acc_sc)

    def update(diagonal):
        if diagonal:
            key = lax.broadcasted_iota(jnp.int32, (tq, tq), 0)
            qry = lax.broadcasted_iota(jnp.int32, (tq, tq), 1)
            visible = key <= qry
        block_max = []
        for h in range(n_heads):
            s = jnp.dot(ka_ref[:, h * AUG:(h + 1) * AUG], qa_ref[h * AUG:(h + 1) * AUG, :],
                        preferred_element_type=F32)
            if diagonal:
                s = jnp.where(visible, s, NEG_INF)
            s_sc[h] = s
            block_max.append(jnp.max(s, axis=0, keepdims=True))
        m_old = m_sc[...]
        m_new = jnp.maximum(m_old, jnp.concatenate(block_max, axis=0))
        alpha = jnp.exp(m_old - m_new)
        m_sc[...] = m_new
        sums = []
        for h in range(n_heads):
            p = jnp.exp(s_sc[h] - m_new[h:h + 1, :])
            sums.append(jnp.sum(p, axis=0, keepdims=True))
            p_sc[h] = p.astype(BF16)
        l_sc[...] = alpha * l_sc[...] + jnp.concatenate(sums, axis=0)
        for h in range(n_heads):
            hs = _head_slice(h)
            acc_sc[hs, :] = alpha[h:h + 1, :] * acc_sc[hs, :] + jnp.dot(
                vt_ref[hs, :], p_sc[h], preferred_element_type=F32)

    @pl.when(kj < qi)
    def _():
        update(False)

    @pl.when(kj == qi)
    def _():
        update(True)
        o_ref[...] = jnp.concatenate(
            [acc_sc[_head_slice(h), :] / l_sc[h:h + 1, :] for h in range(n_heads)], axis=0)


def fox_flash(ka, qa, vt, *, tq, n_heads):
    n_seq, W, S = vt.shape
    nq = S // tq
    pairs = [(i, j) for i in range(nq) for j in range(i + 1)]
    qi_tab = jnp.asarray([p[0] for p in pairs], jnp.int32)
    kj_tab = jnp.asarray([p[1] for p in pairs], jnp.int32)
    grid_spec = pltpu.PrefetchScalarGridSpec(
        num_scalar_prefetch=2,
        grid=(n_seq, len(pairs)),
        in_specs=[pl.BlockSpec((tq, n_heads * AUG), lambda b, s, qi, kj: (b * nq + kj[s], 0)),
                  pl.BlockSpec((None, n_heads * AUG, tq), lambda b, s, qi, kj: (b, 0, qi[s])),
                  pl.BlockSpec((None, W, tq), lambda b, s, qi, kj: (b, 0, kj[s]))],
        out_specs=pl.BlockSpec((None, W, tq), lambda b, s, qi, kj: (b, 0, qi[s])),
        scratch_shapes=[pltpu.VMEM((n_heads, tq), F32), pltpu.VMEM((n_heads, tq), F32),
                        pltpu.VMEM((W, tq), F32),
                        pltpu.VMEM((n_heads, tq, tq), F32), pltpu.VMEM((n_heads, tq, tq), BF16)],
    )
    return pl.pallas_call(
        functools.partial(_fox_flash_kernel, tq=tq, n_heads=n_heads),
        out_shape=jax.ShapeDtypeStruct((n_seq, W, S), F32),
        grid_spec=grid_spec,
        compiler_params=_cparams("parallel", "arbitrary"),
        name="fox_flash",
    )(qi_tab, kj_tab, ka, qa, vt)


def _fox_decode_kernel(pt_ref, q_ref, kn_ref, vn_ref, gc_ref, gr_ref, *rest, valid, n_heads, g0, n_pages):
    k_refs = rest[:n_pages]
    v_refs = rest[n_pages:2 * n_pages]
    lf_refs = rest[2 * n_pages:3 * n_pages]
    o_ref = rest[3 * n_pages]
    del pt_ref
    T = SUBLANES
    H = n_heads
    page = lf_refs[0].shape[-1]

    q = q_ref[...] * SCALE
    kn = kn_ref[...]
    vn = vn_ref[...]
    gc = gc_ref[...]
    gr = gr_ref[...]
    t_i = lax.broadcasted_iota(jnp.int32, (T, LANES), 0)
    cum_col = sum(jnp.where(t_i >= u, gc[u:u + 1, :], 0.0) for u in range(valid))
    u_i = lax.broadcasted_iota(jnp.int32, (GATE_ROWS, T), 1)
    cum_row = sum(jnp.where(u_i >= u, gr[:, u:u + 1], 0.0) for u in range(valid))
    t_q = lax.broadcasted_iota(jnp.int32, (T, T), 0)
    u_k = lax.broadcasted_iota(jnp.int32, (T, T), 1)
    new_visible = (u_k <= t_q) & (u_k < valid)

    s_a = lax.broadcasted_iota(jnp.int32, (page, page), 0)
    s_b = lax.broadcasted_iota(jnp.int32, (page, page), 1)
    later = jnp.where(s_a > s_b, 1.0, 0.0).astype(BF16)
    lf = jnp.concatenate([r[...] for r in lf_refs], axis=0)
    within = sum(jnp.dot(p, later, preferred_element_type=F32) for p in _split3(lf))
    totals = within[:, 0:1] + lf[:, 0:1]
    carry = jnp.zeros((H, 1), F32)
    suffix = [None] * n_pages
    for p in reversed(range(n_pages)):
        suffix[p] = within[p * H:(p + 1) * H, :] + carry
        carry = carry + totals[p * H:(p + 1) * H, :]

    s_past, s_new = [], []
    for h in range(H):
        qh = q[:, _head_slice(h)]
        f_t = cum_col[:, g0 + h:g0 + h + 1]
        kt = jnp.concatenate([r[h] for r in k_refs], axis=1)
        bias = jnp.concatenate([sp[h:h + 1, :] for sp in suffix], axis=1)
        s_past.append(_dot(qh, kt) + (f_t + bias))
        sn = _dot_nt(qh, kn[:, _head_slice(h)]) + (f_t - cum_row[g0 + h:g0 + h + 1, :])
        s_new.append(jnp.where(new_visible, sn, NEG_INF))
    p_past, p_new, denom = [], [], []
    for h in range(H):
        m = jnp.maximum(jnp.max(s_past[h], axis=1, keepdims=True), jnp.max(s_new[h], axis=1, keepdims=True))
        p_past.append(jnp.exp(s_past[h] - m))
        p_new.append(jnp.exp(s_new[h] - m))
        denom.append(jnp.sum(p_past[h], axis=1, keepdims=True) + jnp.sum(p_new[h], axis=1, keepdims=True))
    outs = []
    for h in range(H):
        vt = jnp.concatenate([r[h] for r in v_refs], axis=1)
        outs.append((_dot_nt(p_past[h], vt) + _dot(p_new[h], vn[:, _head_slice(h)])) / denom[h])
    o_ref[...] = jnp.concatenate(outs, axis=1)


def fox_decode(z, gc, gr, cache_kt, cache_vt, cache_lf_t, page_table, *, layer, valid, col0, n_heads, g0):
    M = z.shape[0]
    T = SUBLANES
    n_seq = M // T
    W = n_heads * HEAD_DIM
    n_pages = page_table.shape[1]
    page = cache_kt.shape[-1]
    cb = col0 // W

    def page_idx(i, nd):
        return lambda b, pt: (layer, pt[b * n_pages + i]) + (0,) * nd

    kv_specs = [pl.BlockSpec((None, None, n_heads, HEAD_DIM, page), page_idx(i, 3)) for i in range(n_pages)]
    lf_specs = [pl.BlockSpec((None, None, n_heads, page), page_idx(i, 2)) for i in range(n_pages)]
    zspec = lambda c: pl.BlockSpec((T, W), lambda b, pt: (b, c))
    grid_spec = pltpu.PrefetchScalarGridSpec(
        num_scalar_prefetch=1,
        grid=(n_seq,),
        in_specs=[zspec(cb), zspec(cb + 1), zspec(cb + 2),
                  pl.BlockSpec((T, LANES), lambda b, pt: (b, 0)),
                  pl.BlockSpec((None, GATE_ROWS, T), lambda b, pt: (b, 0, 0))] + kv_specs + kv_specs + lf_specs,
        out_specs=pl.BlockSpec((T, W), lambda b, pt: (b, 0)),
    )
    return pl.pallas_call(
        functools.partial(_fox_decode_kernel, valid=valid, n_heads=n_heads, g0=g0, n_pages=n_pages),
        out_shape=jax.ShapeDtypeStruct((M, W), F32),
        grid_spec=grid_spec,
        compiler_params=_cparams("parallel"),
        name="fox_decode",
    )(page_table.reshape(-1), z, z, z, gc, gr,
      *([cache_kt] * n_pages), *([cache_vt] * n_pages), *([cache_lf_t] * n_pages))


def _mix_cross_prompt_kernel(x_ref, r_ref, m_ref, f_ref, wr_ref, wm_ref, wf_ref,
                             g_ref, wq_ref, kt_ref, vt_ref, wo_ref, o_ref, *, n_heads):
    x = (x_ref[...] + _dot(r_ref[...], wr_ref[...]) + _dot(m_ref[...], wm_ref[...])
         + _dot_tn(f_ref[...], wf_ref[...]))
    q = jnp.dot(_rms(x, g_ref[...]).astype(BF16), wq_ref[...], preferred_element_type=F32) * SCALE
    scores = [_dot(q[:, _head_slice(h)], kt_ref[_head_slice(h), :]) for h in range(n_heads)]
    probs = [_softmax_rows(s) for s in scores]
    outs = [_dot_nt(probs[h], vt_ref[_head_slice(h), :]) for h in range(n_heads)]
    o_ref[...] = x + jnp.dot(jnp.concatenate(outs, axis=-1).astype(BF16), wo_ref[...],
                             preferred_element_type=F32)


def mix_cross_prompt(x, r, m, f_t, w_out, g, w_q, mkv_t, w_o, *, n_heads, tm):
    M, D = x.shape
    W = n_heads * HEAD_DIM
    n_seq, _, mem_len = mkv_t.shape
    per_seq = M // n_seq // tm
    wr, wm = r.shape[1], m.shape[1]
    w_r, w_m, w_f = w_out[:wr], w_out[wr:wr + wm], w_out[wr + wm:]
    row = lambda w: pl.BlockSpec((tm, w), lambda i: (i, 0))
    full = lambda a: pl.BlockSpec(a.shape, lambda i: (0, 0))
    return pl.pallas_call(
        functools.partial(_mix_cross_prompt_kernel, n_heads=n_heads),
        out_shape=jax.ShapeDtypeStruct((M, D), F32),
        grid=(M // tm,),
        in_specs=[row(D), row(wr), row(wm),
                  pl.BlockSpec((None, f_t.shape[1], tm), lambda i: (i // per_seq, 0, i % per_seq)),
                  full(w_r), full(w_m), full(w_f),
                  pl.BlockSpec((1, D), lambda i: (0, 0)),
                  full(w_q),
                  pl.BlockSpec((None, W, mem_len), lambda i: (i // per_seq, 0, 0)),
                  pl.BlockSpec((None, W, mem_len), lambda i: (i // per_seq, 1, 0)),
                  full(w_o)],
        out_specs=row(D),
        compiler_params=_cparams("parallel"),
        name="mix_cross_prompt",
    )(x, r, m, f_t, w_r, w_m, w_f, g.reshape(1, D), w_q, mkv_t, mkv_t, w_o)


def _cross_sample_kernel(x_ref, g_ref, wq_ref, kt_ref, vt_ref, wo_ref, o_ref, *, n_heads, seqs):
    T = SUBLANES
    x = x_ref[...]
    q = jnp.dot(_rms(x, g_ref[...]).astype(BF16), wq_ref[...], preferred_element_type=F32) * SCALE
    items = [(b, h) for b in range(seqs) for h in range(n_heads)]
    scores = {(b, h): _dot(q[b * T:(b + 1) * T, _head_slice(h)], kt_ref[b, h]) for b, h in items}
    probs = {i: _softmax_rows(scores[i]) for i in items}
    outs = {(b, h): _dot_nt(probs[b, h], vt_ref[b, h]) for b, h in items}
    rows = [jnp.concatenate([outs[b, h] for h in range(n_heads)], axis=1) for b in range(seqs)]
    o_ref[...] = x + jnp.dot(jnp.concatenate(rows, axis=0).astype(BF16), wo_ref[...],
                             preferred_element_type=F32)


def cross_attn_sample(x, g, w_q, cache_kt, cache_vt, w_o, *, layer, n_heads, seqs):
    M, D = x.shape
    T = SUBLANES
    W = n_heads * HEAD_DIM
    mem_len = cache_kt.shape[-1]
    kv = pl.BlockSpec((None, seqs, n_heads, HEAD_DIM, mem_len), lambda i: (layer, i, 0, 0, 0))
    return pl.pallas_call(
        functools.partial(_cross_sample_kernel, n_heads=n_heads, seqs=seqs),
        out_shape=jax.ShapeDtypeStruct((M, D), F32),
        grid=(M // (seqs * T),),
        in_specs=[pl.BlockSpec((seqs * T, D), lambda i: (i, 0)),
                  pl.BlockSpec((1, D), lambda i: (0, 0)),
                  pl.BlockSpec((D, W), lambda i: (0, 0)),
                  kv, kv,
                  pl.BlockSpec((W, D), lambda i: (0, 0))],
        out_specs=pl.BlockSpec((seqs * T, D), lambda i: (i, 0)),
        compiler_params=_cparams("parallel"),
        name="cross_attn_sample",
    )(x, g.reshape(1, D), w_q, cache_kt, cache_vt, w_o)


def _rope_tables(pos, n_heads):
    inv = ROPE_BASE ** (-jnp.arange(HALF, dtype=F32) / HALF)
    ang = pos.astype(F32)[:, None] * inv[None, :]
    cos, sin = jnp.cos(ang), jnp.sin(ang)
    return (jnp.tile(jnp.concatenate([cos, cos], axis=-1), (1, n_heads)),
            jnp.tile(jnp.concatenate([-sin, sin], axis=-1), (1, n_heads)))


def _pick(n, candidates):
    for c in candidates:
        if n % c == 0:
            return c
    return n


def kernel(x_prompt, x_sample, mem_prompt, state_ret, state_mlstm_c, state_mlstm_n, state_mlstm_m,
           cache_fox_k, cache_fox_v, cache_fox_logf, cache_mem_k, cache_mem_v, page_table,
           g_mix, w_in, b_in, g_ret, g_mlstm, w_out, g_cross, g_mem, w_cq, w_ckv, w_co,
           g_mlp, w_up, w_down, g_final):
    B, S, D = x_prompt.shape
    NB, T, _ = x_sample.shape
    depth = w_in.shape[0]
    RH = state_ret.shape[2]
    MH = state_mlstm_c.shape[2]
    FH = cache_fox_k.shape[3]
    MEMH = cache_mem_k.shape[3]
    RW, MW, FW, MEMW = RH * HEAD_DIM, MH * HEAD_DIM, FH * HEAD_DIM, MEMH * HEAD_DIM
    n_pages = page_table.shape[1]
    page = cache_fox_k.shape[2]
    mem_len = mem_prompt.shape[1]
    TP = SUBLANES
    assert T <= TP and S % CHUNK == 0 and 2 * MH + FH <= GATE_ROWS
    assert RW % LANES == 0 and MW % LANES == 0 and FW % LANES == 0

    sizes = (RW,) * 4 + (MW,) * 4 + (MH, MH) + (FW,) * 3 + (FH,)
    starts = np.concatenate([[0], np.cumsum(sizes)[:-1]])
    cols = lambda a, *ids: jnp.concatenate([a[..., starts[i]:starts[i] + sizes[i]] for i in ids], axis=-1)
    ret_col0, mls_col0 = 0, 4 * RW
    fox_col0 = mls_col0 + 4 * MW
    assert fox_col0 % FW == 0 and mls_col0 % MW == 0
    fox_g0 = 2 * MH

    def regroup(ids, gate_col0):
        ids = ids + (8, 9, 13)
        n_cols = gate_col0 + 2 * LANES
        pad = n_cols - sum(sizes[i] for i in ids)
        return (jnp.pad(cols(w_in, *ids), ((0, 0), (0, 0), (0, pad))).astype(BF16),
                jnp.pad(cols(b_in, *ids), ((0, 0), (0, pad))), n_cols)

    gate_col0_s = fox_col0 + 3 * FW
    w_in_s, b_in_s, n_cols_s = regroup((0, 1, 2, 3, 4, 5, 6, 7, 10, 11, 12), gate_col0_s)
    gate_col0_p = fox_col0 + FW
    w_in_p, b_in_p, n_cols_p = regroup((0, 1, 2, 3, 4, 5, 6, 7, 11), gate_col0_p)
    w_fox_t = jnp.swapaxes(cols(w_in, 10, 11, 12), 1, 2).astype(BF16)
    b_fox = cols(b_in, 10, 11, 12)

    w_out_b, w_cq_b, w_co_b = (w.astype(BF16) for w in (w_out, w_cq, w_co))
    w_ckv_t = jnp.swapaxes(w_ckv, 1, 2).astype(BF16)
    w_up_b, w_down_b = w_up.astype(BF16), w_down.astype(BF16)

    past = n_pages * page
    cos_p, sin_p = _rope_tables(jnp.arange(S, dtype=jnp.int32), RH)
    pos_s = jnp.minimum(past + jnp.arange(TP, dtype=jnp.int32), past + T - 1)
    cos_s, sin_s = _rope_tables(pos_s, RH)
    log_gamma = tuple(float(np.log1p(-np.exp2(-5.0 - h))) for h in range(RH))

    cache_kt = jnp.transpose(cache_fox_k, (0, 1, 3, 4, 2))
    cache_vt = jnp.transpose(cache_fox_v, (0, 1, 3, 4, 2))
    cache_lf_t = jnp.swapaxes(cache_fox_logf, 2, 3)
    mem_kt = jnp.transpose(cache_mem_k, (0, 1, 3, 4, 2))
    mem_vt = jnp.transpose(cache_mem_v, (0, 1, 3, 4, 2))

    Mp, Ms = B * S, NB * TP
    xp = x_prompt.reshape(Mp, D)
    xs = jnp.pad(x_sample, ((0, 0), (0, TP - T), (0, 0))).reshape(Ms, D)
    mem = mem_prompt.reshape(B * mem_len, D)

    tm_p = _pick(S, (1024, 512, 256, 128))
    tm_in = _pick(S, (512, 256, 128))
    tm_s = _pick(Ms, (1024, 512, 256, 128, 64, 32, 16, 8))
    tn_s = _pick(n_cols_s, (1280, 768, 640, 512, 256, 128))
    tn_p = n_cols_p
    tf = _pick(w_up.shape[2], (2048, 1024, 512, 256, 128))
    tq = _pick(S, (512, 256, 128))
    l_prep = _pick(S, (512, 256, 128))
    seqs = _pick(NB, (8, 4, 2, 1))
    seqs_p = _pick(B, (2, 1))
    zeros = lambda *s: jnp.zeros(s, F32)

    def gates_of(z, gate_col0, n_seq, rows, tm):
        gc = gate_activations(z, col_block=gate_col0 // LANES, n_raw=MH, tm=tm)
        gr = jnp.swapaxes(gc[:, :GATE_ROWS].reshape(n_seq, rows, GATE_ROWS), 1, 2)
        return gc, gr

    outs_p = [[] for _ in range(9)]
    outs_s = [[] for _ in range(7)]
    for l in range(depth):
        zp, qt_p, kt_p, vt_p = in_proj_prompt(xp, g_mix[l], w_in_p[l], b_in_p[l], w_fox_t[l], b_fox[l],
                                              n_seq=B, tm=tm_in, tn=tn_p)
        gcp, grp = gates_of(zp, gate_col0_p, B, S, tm_p)
        r_p, S_p = retention(zp, cos_p, sin_p, zeros(B, RH, HEAD_DIM, HEAD_DIM), g_ret[l],
                             n_seq=B, L=CHUNK, valid=CHUNK, col0=ret_col0, log_gamma=log_gamma,
                             seqs=seqs_p)
        m_p, C_p, n_p, mm_p = mlstm(zp, gcp, grp, zeros(B, MH, HEAD_DIM, HEAD_DIM), zeros(B, MH, HEAD_DIM),
                                    zeros(B, MH), g_mlstm[l], n_seq=B, L=CHUNK, valid=CHUNK,
                                    col0=mls_col0, n_heads=MH, seqs=1)
        ka, qa, vb = fox_prep(gcp, grp, zp, qt_p, vt_p, n_seq=B, L=l_prep, k_col0=fox_col0,
                              n_heads=FH, g0=fox_g0)
        f_p = fox_flash(ka, qa, vb, tq=tq, n_heads=FH)
        mkv_t = norm_matmul_t(mem, g_mem[l], w_ckv_t[l], n_seq=B)
        xp = mix_cross_prompt(xp, r_p, m_p, f_p, w_out_b[l], g_cross[l], w_cq_b[l], mkv_t, w_co_b[l],
                              n_heads=MEMH, tm=tm_p)
        xp = sq_relu_mlp(xp, g_mlp[l], w_up_b[l], w_down_b[l], tm=tm_p, tf=tf)

        zs = norm_matmul(xs, g_mix[l], w_in_s[l], b_in_s[l], tm=tm_s, tn=tn_s)
        gcs, grs = gates_of(zs, gate_col0_s, NB, TP, tm_s)
        r_s, S_s = retention(zs, cos_s, sin_s, state_ret[l], g_ret[l], n_seq=NB, L=TP, valid=T,
                             col0=ret_col0, log_gamma=log_gamma, seqs=seqs)
        m_s, C_s, n_s, mm_s = mlstm(zs, gcs, grs, state_mlstm_c[l], state_mlstm_n[l], state_mlstm_m[l],
                                    g_mlstm[l], n_seq=NB, L=TP, valid=T, col0=mls_col0, n_heads=MH,
                                    seqs=seqs)
        f_s = fox_decode(zs, gcs, grs, cache_kt, cache_vt, cache_lf_t, page_table, layer=l, valid=T,
                         col0=fox_col0, n_heads=FH, g0=fox_g0)
        xs = out_proj(xs, r_s, m_s, f_s, w_out_b[l], tm=tm_s)
        xs = cross_attn_sample(xs, g_cross[l], w_cq_b[l], mem_kt, mem_vt, w_co_b[l], layer=l,
                               n_heads=MEMH, seqs=seqs)
        xs = sq_relu_mlp(xs, g_mlp[l], w_up_b[l], w_down_b[l], tm=tm_s, tf=tf)

        heads_last = lambda a, n: jnp.transpose(a.reshape(a.shape[0], n, HEAD_DIM, a.shape[-1]), (0, 3, 1, 2))
        fks = lambda c: zs[:, fox_col0 + c * FW:fox_col0 + (c + 1) * FW].reshape(NB, TP, FH, HEAD_DIM)[:, :T]
        new_p = (S_p, C_p, n_p, mm_p,
                 heads_last(kt_p, FH), heads_last(vt_p, FH),
                 gcp[:, fox_g0:fox_g0 + FH].reshape(B, S, FH),
                 heads_last(mkv_t[:, :MEMW], MEMH), heads_last(mkv_t[:, MEMW:], MEMH))
        new_s = (S_s, C_s, n_s, mm_s, fks(1), fks(2),
                 gcs[:, fox_g0:fox_g0 + FH].reshape(NB, TP, FH)[:, :T])
        for lst, a in zip(outs_p, new_p):
            lst.append(a)
        for lst, a in zip(outs_s, new_s):
            lst.append(a)

    y_prompt = final_norm(xp, g_final, tm=tm_p).reshape(B, S, D)
    y_sample = final_norm(xs, g_final, tm=tm_s).reshape(NB, TP, D)[:, :T]
    ret_p, c_p, n_p, m_p, fk_p, fv_p, flf_p, memk_p, memv_p = [jnp.stack(a, axis=0) for a in outs_p]
    ret_s, c_s, n_s, m_s, fk_s, fv_s, flf_s = [jnp.stack(a, axis=0) for a in outs_s]
    return (y_prompt, y_sample, ret_p, ret_s, c_p, c_s, n_p, n_s, m_p, m_s,
            fk_p, fk_s, fv_p, fv_s, flf_p, flf_s, memk_p, memv_p)
```

```python
import functools
import math

import numpy as np
import jax
import jax.numpy as jnp
from jax import lax
from jax.experimental import pallas as pl
from jax.experimental.pallas import tpu as pltpu

F32 = jnp.float32
BF16 = jnp.bfloat16

HEAD_DIM = 64
HALF = HEAD_DIM // 2
EPS = 1e-6
ROPE_BASE = 10000.0
CHUNK = 128
SUBLANES = 8
LANES = 128
GATE_ROWS = 16
AUG = 2 * HEAD_DIM
SCALE = HEAD_DIM ** -0.5
VMEM_LIMIT = 56 * 1024 * 1024
NEG_INF = float("-inf")


def _cparams(*sem):
    return pltpu.CompilerParams(dimension_semantics=sem, vmem_limit_bytes=VMEM_LIMIT)


def _dot(a, b):
    return jnp.dot(a.astype(BF16), b.astype(BF16), preferred_element_type=F32)


def _dot_nt(a, b):
    return lax.dot_general(a.astype(BF16), b.astype(BF16), (((1,), (1,)), ((), ())),
                           preferred_element_type=F32)


def _dot_tn(a, b):
    return lax.dot_general(a.astype(BF16), b.astype(BF16), (((0,), (0,)), ((), ())),
                           preferred_element_type=F32)


def _split3(x):
    hi = x.astype(BF16)
    r1 = x - hi.astype(F32)
    mid = r1.astype(BF16)
    lo = (r1 - mid.astype(F32)).astype(BF16)
    return hi, mid, lo


def _cumsum_rows(x, n):
    t = lax.broadcasted_iota(jnp.int32, (n, n), 0)
    s = lax.broadcasted_iota(jnp.int32, (n, n), 1)
    tri = jnp.where(t >= s, 1.0, 0.0).astype(BF16)
    return sum(jnp.dot(tri, p, preferred_element_type=F32) for p in _split3(x))


def _cumsum_lanes(x, n):
    t = lax.broadcasted_iota(jnp.int32, (n, n), 0)
    s = lax.broadcasted_iota(jnp.int32, (n, n), 1)
    tri = jnp.where(t <= s, 1.0, 0.0).astype(BF16)
    return sum(jnp.dot(p, tri, preferred_element_type=F32) for p in _split3(x))


def _rms(x, g):
    return x * lax.rsqrt(jnp.mean(x * x, axis=-1, keepdims=True) + EPS) * g


def _sigmoid(x):
    return 1.0 / (1.0 + jnp.exp(-x))


def _log_sigmoid(x):
    return jnp.minimum(x, 0.0) - jnp.log1p(jnp.exp(-jnp.abs(x)))


def _head_norm(o):
    return o * lax.rsqrt(jnp.mean(o * o, axis=-1, keepdims=True) + EPS)


def _softmax_rows(s):
    p = jnp.exp(s - jnp.max(s, axis=1, keepdims=True))
    return p / jnp.sum(p, axis=1, keepdims=True)


def _head_slice(h):
    return slice(h * HEAD_DIM, (h + 1) * HEAD_DIM)


def _norm_matmul_kernel(x_ref, g_ref, w_ref, b_ref, o_ref, xn_ref):
    @pl.when(pl.program_id(1) == 0)
    def _():
        xn_ref[...] = _rms(x_ref[...], g_ref[...]).astype(BF16)

    o_ref[...] = jnp.dot(xn_ref[...], w_ref[...], preferred_element_type=F32) + b_ref[...]


def norm_matmul(x, g, w, b, *, tm, tn):
    M, D = x.shape
    N = w.shape[1]
    return pl.pallas_call(
        _norm_matmul_kernel,
        out_shape=jax.ShapeDtypeStruct((M, N), F32),
        grid=(M // tm, N // tn),
        in_specs=[pl.BlockSpec((tm, D), lambda i, j: (i, 0)),
                  pl.BlockSpec((1, D), lambda i, j: (0, 0)),
                  pl.BlockSpec((D, tn), lambda i, j: (0, j)),
                  pl.BlockSpec((1, tn), lambda i, j: (0, j))],
        out_specs=pl.BlockSpec((tm, tn), lambda i, j: (i, j)),
        scratch_shapes=[pltpu.VMEM((tm, D), BF16)],
        compiler_params=_cparams("parallel", "arbitrary"),
        name="norm_matmul",
    )(x, g.reshape(1, D), w, b.reshape(1, N))


def _in_proj_prompt_kernel(x_ref, g_ref, w_ref, b_ref, wt_ref, bt_ref, z_ref, qt_ref, kt_ref, vt_ref,
                           xn_ref, *, fw):
    @pl.when(pl.program_id(2) == 0)
    def _():
        xn_ref[...] = _rms(x_ref[...], g_ref[...]).astype(BF16)

    z_ref[...] = jnp.dot(xn_ref[...], w_ref[...], preferred_element_type=F32) + b_ref[...]

    @pl.when(pl.program_id(2) == pl.num_programs(2) - 1)
    def _():
        zt = lax.dot_general(wt_ref[...], xn_ref[...], (((1,), (1,)), ((), ())),
                             preferred_element_type=F32) + bt_ref[...]
        qt_ref[...] = zt[:fw]
        kt_ref[...] = zt[fw:2 * fw]
        vt_ref[...] = zt[2 * fw:]


def in_proj_prompt(x, g, w, b, wt, bt, *, n_seq, tm, tn):
    M, D = x.shape
    N = w.shape[1]
    S = M // n_seq
    fw = wt.shape[0] // 3
    nt = S // tm
    t_out = jax.ShapeDtypeStruct((n_seq, fw, S), F32)
    t_spec = pl.BlockSpec((None, fw, tm), lambda s, i, j: (s, 0, i))
    return pl.pallas_call(
        functools.partial(_in_proj_prompt_kernel, fw=fw),
        out_shape=(jax.ShapeDtypeStruct((M, N), F32), t_out, t_out, t_out),
        grid=(n_seq, nt, N // tn),
        in_specs=[pl.BlockSpec((tm, D), lambda s, i, j: (s * nt + i, 0)),
                  pl.BlockSpec((1, D), lambda s, i, j: (0, 0)),
                  pl.BlockSpec((D, tn), lambda s, i, j: (0, j)),
                  pl.BlockSpec((1, tn), lambda s, i, j: (0, j)),
                  pl.BlockSpec((3 * fw, D), lambda s, i, j: (0, 0)),
                  pl.BlockSpec((3 * fw, 1), lambda s, i, j: (0, 0))],
        out_specs=(pl.BlockSpec((tm, tn), lambda s, i, j: (s * nt + i, j)), t_spec, t_spec, t_spec),
        scratch_shapes=[pltpu.VMEM((tm, D), BF16)],
        compiler_params=_cparams("parallel", "parallel", "arbitrary"),
        name="in_proj_prompt",
    )(x, g.reshape(1, D), w, b.reshape(1, N), wt, bt.reshape(3 * fw, 1))


def _norm_matmul_t_kernel(x_ref, g_ref, wt_ref, o_ref):
    xn = _rms(x_ref[...], g_ref[...]).astype(BF16)
    o_ref[...] = lax.dot_general(wt_ref[...], xn, (((1,), (1,)), ((), ())), preferred_element_type=F32)


def norm_matmul_t(x, g, wt, *, n_seq):
    M, D = x.shape
    N = wt.shape[0]
    R = M // n_seq
    return pl.pallas_call(
        _norm_matmul_t_kernel,
        out_shape=jax.ShapeDtypeStruct((n_seq, N, R), F32),
        grid=(n_seq,),
        in_specs=[pl.BlockSpec((R, D), lambda s: (s, 0)),
                  pl.BlockSpec((1, D), lambda s: (0, 0)),
                  pl.BlockSpec((N, D), lambda s: (0, 0))],
        out_specs=pl.BlockSpec((None, N, R), lambda s: (s, 0, 0)),
        compiler_params=_cparams("parallel"),
        name="norm_matmul_t",
    )(x, g.reshape(1, D), wt)


def _gates_kernel(z_ref, o_ref, *, n_raw):
    z = z_ref[...]
    lane = lax.broadcasted_iota(jnp.int32, z.shape, 1)
    o_ref[...] = jnp.where(lane >= n_raw, _log_sigmoid(z), z)


def gate_activations(z, *, col_block, n_raw, tm):
    M = z.shape[0]
    return pl.pallas_call(
        functools.partial(_gates_kernel, n_raw=n_raw),
        out_shape=jax.ShapeDtypeStruct((M, LANES), F32),
        grid=(M // tm,),
        in_specs=[pl.BlockSpec((tm, LANES), lambda i: (i, col_block))],
        out_specs=pl.BlockSpec((tm, LANES), lambda i: (i, 0)),
        compiler_params=_cparams("parallel"),
        name="gate_activations",
    )(z)


def _out_proj_kernel(x_ref, r_ref, m_ref, f_ref, wr_ref, wm_ref, wf_ref, o_ref):
    o_ref[...] = (x_ref[...] + _dot(r_ref[...], wr_ref[...]) + _dot(m_ref[...], wm_ref[...])
                  + _dot(f_ref[...], wf_ref[...]))


def out_proj(x, r, m, f, w_out, *, tm):
    M, D = x.shape
    wr, wm = r.shape[1], m.shape[1]
    w_r, w_m, w_f = w_out[:wr], w_out[wr:wr + wm], w_out[wr + wm:]
    row = lambda w: pl.BlockSpec((tm, w), lambda i: (i, 0))
    full = lambda a: pl.BlockSpec(a.shape, lambda i: (0, 0))
    return pl.pallas_call(
        _out_proj_kernel,
        out_shape=jax.ShapeDtypeStruct((M, D), F32),
        grid=(M // tm,),
        in_specs=[row(D), row(wr), row(wm), row(f.shape[1]), full(w_r), full(w_m), full(w_f)],
        out_specs=row(D),
        compiler_params=_cparams("parallel"),
        name="out_proj",
    )(x, r, m, f, w_r, w_m, w_f)


def _mlp_kernel(x_ref, g_ref, wu_ref, wd_ref, o_ref, xn_ref, acc_ref):
    j = pl.program_id(1)

    @pl.when(j == 0)
    def _():
        xn_ref[...] = _rms(x_ref[...], g_ref[...]).astype(BF16)
        acc_ref[...] = x_ref[...]

    h = jnp.maximum(jnp.dot(xn_ref[...], wu_ref[...], preferred_element_type=F32), 0.0)
    acc_ref[...] += jnp.dot((h * h).astype(BF16), wd_ref[...], preferred_element_type=F32)

    @pl.when(j == pl.num_programs(1) - 1)
    def _():
        o_ref[...] = acc_ref[...]


def sq_relu_mlp(x, g, w_up, w_down, *, tm, tf):
    M, D = x.shape
    FF = w_up.shape[1]
    return pl.pallas_call(
        _mlp_kernel,
        out_shape=jax.ShapeDtypeStruct((M, D), F32),
        grid=(M // tm, FF // tf),
        in_specs=[pl.BlockSpec((tm, D), lambda i, j: (i, 0)),
                  pl.BlockSpec((1, D), lambda i, j: (0, 0)),
                  pl.BlockSpec((D, tf), lambda i, j: (0, j)),
                  pl.BlockSpec((tf, D), lambda i, j: (j, 0))],
        out_specs=pl.BlockSpec((tm, D), lambda i, j: (i, 0)),
        scratch_shapes=[pltpu.VMEM((tm, D), BF16), pltpu.VMEM((tm, D), F32)],
        compiler_params=_cparams("parallel", "arbitrary"),
        name="sq_relu_mlp",
    )(x, g.reshape(1, D), w_up, w_down)


def _final_norm_kernel(x_ref, g_ref, o_ref):
    o_ref[...] = _rms(x_ref[...], g_ref[...])


def final_norm(x, g, *, tm):
    M, D = x.shape
    return pl.pallas_call(
        _final_norm_kernel,
        out_shape=jax.ShapeDtypeStruct((M, D), F32),
        grid=(M // tm,),
        in_specs=[pl.BlockSpec((tm, D), lambda i: (i, 0)), pl.BlockSpec((1, D), lambda i: (0, 0))],
        out_specs=pl.BlockSpec((tm, D), lambda i: (i, 0)),
        compiler_params=_cparams("parallel"),
        name="final_norm",
    )(x, g.reshape(1, D))


def _retention_kernel(q_ref, k_ref, v_ref, g_ref, cos_ref, sin_ref, s0_ref, gain_ref,
                      y_ref, sout_ref, s_sc, *, L, valid, log_gamma, seqs):
    c = pl.program_id(1)
    n_heads = len(log_gamma)
    W = n_heads * HEAD_DIM
    P = n_heads // 2
    HD = HEAD_DIM
    upper = lax.broadcasted_iota(jnp.int32, (1, LANES), 1) >= HD
    r_i = lax.broadcasted_iota(jnp.int32, (LANES, LANES), 0)
    c_i = lax.broadcasted_iota(jnp.int32, (LANES, LANES), 1)
    same_half = (r_i >= HD) == (c_i >= HD)
    half_ones = jnp.where(same_half, 1.0, 0.0).astype(BF16)

    @pl.when(c == 0)
    def _():
        zero = jnp.zeros((HD, HD), F32)
        for j in range(seqs):
            for p in range(P):
                s_sc[j, p] = jnp.concatenate(
                    [jnp.concatenate([s0_ref[j, 2 * p], zero], axis=1),
                     jnp.concatenate([zero, s0_ref[j, 2 * p + 1]], axis=1)], axis=0)

    cos = cos_ref[...]
    sin = sin_ref[...]
    lane = lax.broadcasted_iota(jnp.int32, (L, W), 1)
    first_half = (lane % HD) < HALF

    def rope(x):
        swapped = jnp.where(first_half, pltpu.roll(x, W - HALF, 1), pltpu.roll(x, HALF, 1))
        return x * cos + swapped * sin

    t_i = lax.broadcasted_iota(jnp.int32, (L, L), 0)
    s_i = lax.broadcasted_iota(jnp.int32, (L, L), 1)
    causal = t_i >= s_i
    diff = jnp.where(causal, (t_i - s_i).astype(F32), 0.0)
    decay = [jnp.where(causal, jnp.exp(lg * diff), 0.0) for lg in log_gamma]
    t_rep = lax.broadcasted_iota(jnp.int32, (L, LANES), 0).astype(F32)
    pair_lg = [jnp.where(upper, log_gamma[2 * p + 1], log_gamma[2 * p]) for p in range(P)]
    q_decay = [jnp.exp(lg * (t_rep + 1.0)) for lg in pair_lg]
    k_decay = [jnp.where(t_rep < valid, jnp.exp(lg * (valid - 1.0 - t_rep)), 0.0) for lg in pair_lg]
    s_decay = [jnp.exp(lg * float(valid)) for lg in pair_lg]

    pairs = [(j, p) for j in range(seqs) for p in range(P)]
    q, k, v = {}, {}, {}
    for j in range(seqs):
        qj = rope(q_ref[j])
        kj = rope(k_ref[j]) * SCALE
        for p in range(P):
            tile = slice(p * LANES, (p + 1) * LANES)
            q[j, p], k[j, p], v[j, p] = qj[:, tile], kj[:, tile], v_ref[j, :, tile]
    scores, cross = {}, {}
    for j, p in pairs:
        qb = q[j, p].astype(BF16)
        for e in range(2):
            scores[j, 2 * p + e] = _dot_nt(qb, jnp.where(upper if e else ~upper, k[j, p], 0.0))
        cross[j, p] = _dot(qb, s_sc[j, p])
    outs = {}
    for j, p in pairs:
        lhs = jnp.concatenate([scores[j, 2 * p] * decay[2 * p], scores[j, 2 * p + 1] * decay[2 * p + 1]],
                              axis=1)
        v_sel = jnp.concatenate([jnp.where(upper, 0.0, v[j, p]), jnp.where(upper, v[j, p], 0.0)], axis=0)
        o = _dot(lhs, v_sel) + cross[j, p] * q_decay[p]
        o2 = o * o
        hi = o2.astype(BF16)
        lo = (o2 - hi.astype(F32)).astype(BF16)
        mean_sq = (jnp.dot(hi, half_ones, preferred_element_type=F32)
                   + jnp.dot(lo, half_ones, preferred_element_type=F32)) * (1.0 / HD)
        outs[j, p] = o * lax.rsqrt(mean_sq + EPS)
    for j, p in pairs:
        update = jnp.where(same_half, _dot_tn(k[j, p] * k_decay[p], v[j, p]), 0.0)
        s_sc[j, p] = s_decay[p] * s_sc[j, p] + update
    for j in range(seqs):
        g = g_ref[j]
        y_ref[j] = jnp.concatenate([outs[j, p] for p in range(P)], axis=-1) * gain_ref[...] * (
            g * _sigmoid(g))

    @pl.when(c == pl.num_programs(1) - 1)
    def _():
        for j in range(seqs):
            for p in range(P):
                blk = s_sc[j, p]
                sout_ref[j, 2 * p] = blk[:HD, :HD]
                sout_ref[j, 2 * p + 1] = blk[HD:, HD:]


def retention(z, cos, sin, s0, gain, *, n_seq, L, valid, col0, log_gamma, seqs):
    M, N = z.shape
    n_heads = len(log_gamma)
    W = n_heads * HEAD_DIM
    rows = M // n_seq
    cb = col0 // W
    z3 = z.reshape(n_seq, rows, N)
    zspec = lambda j: pl.BlockSpec((seqs, L, W), lambda b, c: (b, c, cb + j))
    tab = pl.BlockSpec((L, W), lambda b, c: (c, 0))
    st = pl.BlockSpec((seqs, n_heads, HEAD_DIM, HEAD_DIM), lambda b, c: (b, 0, 0, 0))
    y, s_out = pl.pallas_call(
        functools.partial(_retention_kernel, L=L, valid=valid, log_gamma=log_gamma, seqs=seqs),
        out_shape=(jax.ShapeDtypeStruct((n_seq, rows, W), F32),
                   jax.ShapeDtypeStruct((n_seq, n_heads, HEAD_DIM, HEAD_DIM), F32)),
        grid=(n_seq // seqs, rows // L),
        in_specs=[zspec(0), zspec(1), zspec(2), zspec(3), tab, tab, st,
                  pl.BlockSpec((1, W), lambda b, c: (0, 0))],
        out_specs=(pl.BlockSpec((seqs, L, W), lambda b, c: (b, c, 0)), st),
        scratch_shapes=[pltpu.VMEM((seqs, n_heads // 2, LANES, LANES), F32)],
        compiler_params=_cparams("parallel", "arbitrary"),
        name="retention",
    )(z3, z3, z3, z3, cos, sin, s0, gain.reshape(1, W))
    return y.reshape(M, W), s_out


def _cumsum_rows_any(x, n):
    if n > SUBLANES:
        return _cumsum_rows(x, n)
    t = lax.broadcasted_iota(jnp.int32, x.shape, 0)
    return sum(jnp.where(t >= u, x[u:u + 1, :], 0.0) for u in range(n))


def _cumsum_lanes_any(x, n):
    if n > SUBLANES:
        return _cumsum_lanes(x, n)
    s = lax.broadcasted_iota(jnp.int32, x.shape, 1)
    return sum(jnp.where(s >= u, x[:, u:u + 1], 0.0) for u in range(n))


def _cummax_rows(x, n):
    row = lax.broadcasted_iota(jnp.int32, x.shape, 0)
    shift = 1
    while shift < n:
        x = jnp.maximum(x, jnp.where(row >= shift, pltpu.roll(x, shift, 0), NEG_INF))
        shift *= 2
    return x


def _mlstm_kernel(q_ref, k_ref, v_ref, og_ref, gc_ref, gr_ref, c0_ref, n0_ref, m0_ref, gain_ref,
                  y_ref, cout_ref, nout_ref, mout_ref, c_sc, n_sc, m_sc, *, L, valid, n_heads, seqs):
    c = pl.program_id(1)
    P = n_heads // 2
    HD = HEAD_DIM
    lane = lax.broadcasted_iota(jnp.int32, (1, LANES), 1)
    upper = lane >= HD
    r_i = lax.broadcasted_iota(jnp.int32, (LANES, LANES), 0)
    c_i = lax.broadcasted_iota(jnp.int32, (LANES, LANES), 1)
    same_half = (r_i >= HD) == (c_i >= HD)
    half_ones = jnp.where(same_half, 1.0, 0.0).astype(BF16)

    @pl.when(c == 0)
    def _():
        zero = jnp.zeros((HD, HD), F32)
        for j in range(seqs):
            for p in range(P):
                c_sc[j, p] = jnp.concatenate(
                    [jnp.concatenate([c0_ref[j, 2 * p], zero], axis=1),
                     jnp.concatenate([zero, c0_ref[j, 2 * p + 1]], axis=1)], axis=0)
                n_sc[j, p] = jnp.concatenate([n0_ref[j, 2 * p:2 * p + 1, :], n0_ref[j, 2 * p + 1:2 * p + 2, :]],
                                             axis=1)
        m_sc[...] = m0_ref[...]

    t_i = lax.broadcasted_iota(jnp.int32, (L, L), 0)
    s_i = lax.broadcasted_iota(jnp.int32, (L, L), 1)
    causal = t_i >= s_i
    row_valid = lax.broadcasted_iota(jnp.int32, (L, LANES), 0) < valid
    rr = lax.broadcasted_iota(jnp.int32, (2 * L, LANES), 0)
    cc = lax.broadcasted_iota(jnp.int32, (2 * L, LANES), 1)
    sum_sel = jnp.where((rr >= L) == (cc >= HD), 1.0, 0.0)

    pairs = [(j, p) for j in range(seqs) for p in range(P)]
    heads = [(j, h) for j in range(seqs) for h in range(n_heads)]
    q, k, v, og = {}, {}, {}, {}
    f_rep, i_rep, w_row, m_prev = {}, {}, {}, {}
    for j in range(seqs):
        gc = gc_ref[j]
        gr = gr_ref[j]
        f_col = _cumsum_rows_any(gc, L)
        f_row = _cumsum_lanes_any(gr, L)
        for p in range(P):
            tile = slice(p * LANES, (p + 1) * LANES)
            q[j, p] = q_ref[j, :, tile]
            k[j, p] = k_ref[j, :, tile] * SCALE
            v[j, p] = v_ref[j, :, tile]
            og[j, p] = _sigmoid(og_ref[j, :, tile])
        for h in range(n_heads):
            f_rep[j, h] = jnp.broadcast_to(f_col[:, n_heads + h:n_heads + h + 1], (L, LANES))
            i_rep[j, h] = jnp.broadcast_to(gc[:, h:h + 1], (L, LANES))
            w_row[j, h] = f_row[n_heads + h:n_heads + h + 1, :] - gr[h:h + 1, :]
            m_prev[j, h] = m_sc[j, h:h + 1, :]

    qk_raw, cq, nq = {}, {}, {}
    for j, p in pairs:
        qb = q[j, p].astype(BF16)
        for e in range(2):
            k_half = jnp.where(upper if e else ~upper, k[j, p], 0.0)
            qk_raw[j, 2 * p + e] = _dot_nt(qb, k_half)
        cq[j, p] = _dot_nt(qb, c_sc[j, p])
        n_rep = jnp.where(same_half, jnp.broadcast_to(n_sc[j, p], (LANES, LANES)), 0.0)
        nq[j, p] = _dot_nt(qb, n_rep)

    m_t, u, w_inter, e_rep = {}, {}, {}, {}
    for j, h in heads:
        peak = jnp.maximum(m_prev[j, h], _cummax_rows(i_rep[j, h] - f_rep[j, h], L))
        m_t[j, h] = f_rep[j, h] + peak
        u[j, h] = f_rep[j, h] - m_t[j, h]
        w_inter[j, h] = jnp.exp(u[j, h] + m_prev[j, h])
        e_rep[j, h] = jnp.exp(-m_t[j, h])

    outs = {}
    for j, p in pairs:
        h0, h1 = 2 * p, 2 * p + 1
        gated = [qk_raw[j, h] * jnp.where(causal, jnp.exp(u[j, h][:, :L] - w_row[j, h]), 0.0)
                 for h in (h0, h1)]
        lhs = jnp.concatenate(gated, axis=1)
        v_sel = jnp.concatenate([jnp.where(upper, 0.0, v[j, p]), jnp.where(upper, v[j, p], 0.0)], axis=0)
        mix = _dot(lhs, jnp.concatenate([v_sel, sum_sel], axis=1))
        w_pair = jnp.where(upper, w_inter[j, h1], w_inter[j, h0])
        num = mix[:, :LANES] + w_pair * cq[j, p]
        den = mix[:, LANES:] + w_pair * nq[j, p]
        floor = jnp.where(upper, e_rep[j, h1], e_rep[j, h0])
        y = og[j, p] * (num / jnp.maximum(jnp.abs(den), floor))
        y2 = y * y
        hi = y2.astype(BF16)
        lo = (y2 - hi.astype(F32)).astype(BF16)
        mean_sq = (jnp.dot(hi, half_ones, preferred_element_type=F32)
                   + jnp.dot(lo, half_ones, preferred_element_type=F32)) * (1.0 / HD)
        outs[j, p] = y * lax.rsqrt(mean_sq + EPS)

    for j, p in pairs:
        h0, h1 = 2 * p, 2 * p + 1
        w_s, w_c, m_last = {}, {}, {}
        for h in (h0, h1):
            m_last[h] = m_t[j, h][valid - 1:valid, :]
            f_last = f_rep[j, h][valid - 1:valid, :]
            w_s[h] = jnp.where(row_valid, jnp.exp(f_last - f_rep[j, h] + i_rep[j, h] - m_last[h]), 0.0)
            w_c[h] = jnp.exp(f_last + m_prev[j, h] - m_last[h])
        ws_pair = jnp.where(upper, w_s[h1], w_s[h0])
        wc_pair = jnp.where(upper, w_c[h1], w_c[h0])
        update = jnp.where(same_half, _dot_tn(v[j, p] * ws_pair, k[j, p]), 0.0)
        c_sc[j, p] = wc_pair * c_sc[j, p] + update
        n_sc[j, p] = wc_pair * n_sc[j, p] + jnp.sum(k[j, p] * ws_pair, axis=0, keepdims=True)
        m_sc[j, h0:h0 + 1, :] = m_last[h0]
        m_sc[j, h1:h1 + 1, :] = m_last[h1]
    for j in range(seqs):
        y_ref[j] = jnp.concatenate([outs[j, p] for p in range(P)], axis=-1) * gain_ref[...]

    @pl.when(c == pl.num_programs(1) - 1)
    def _():
        for j in range(seqs):
            for p in range(P):
                blk = c_sc[j, p]
                row = n_sc[j, p]
                cout_ref[j, 2 * p] = blk[:HD, :HD]
                cout_ref[j, 2 * p + 1] = blk[HD:, HD:]
                nout_ref[j, 2 * p:2 * p + 1, :] = row[:, :HD]
                nout_ref[j, 2 * p + 1:2 * p + 2, :] = row[:, HD:]
        mout_ref[...] = m_sc[...]


def mlstm(z, gc, gr, c0, n0, m0, gain, *, n_seq, L, valid, col0, n_heads, seqs):
    M, N = z.shape
    W = n_heads * HEAD_DIM
    P = n_heads // 2
    assert n_heads % 2 == 0 and L <= LANES
    rows = M // n_seq
    cb = col0 // W
    z3 = z.reshape(n_seq, rows, N)
    zspec = lambda j: pl.BlockSpec((seqs, L, W), lambda b, c: (b, c, cb + j))
    cst = pl.BlockSpec((seqs, n_heads, HEAD_DIM, HEAD_DIM), lambda b, c: (b, 0, 0, 0))
    nst = pl.BlockSpec((seqs, n_heads, HEAD_DIM), lambda b, c: (b, 0, 0))
    mst = pl.BlockSpec((seqs, n_heads, LANES), lambda b, c: (b, 0, 0))
    m0_rep = jnp.broadcast_to(m0[:, :, None], (n_seq, n_heads, LANES))
    y, c_out, n_out, m_out = pl.pallas_call(
        functools.partial(_mlstm_kernel, L=L, valid=valid, n_heads=n_heads, seqs=seqs),
        out_shape=(jax.ShapeDtypeStruct((n_seq, rows, W), F32),
                   jax.ShapeDtypeStruct((n_seq, n_heads, HEAD_DIM, HEAD_DIM), F32),
                   jax.ShapeDtypeStruct((n_seq, n_heads, HEAD_DIM), F32),
                   jax.ShapeDtypeStruct((n_seq, n_heads, LANES), F32)),
        grid=(n_seq // seqs, rows // L),
        in_specs=[zspec(0), zspec(1), zspec(2), zspec(3),
                  pl.BlockSpec((seqs, L, LANES), lambda b, c: (b, c, 0)),
                  pl.BlockSpec((seqs, GATE_ROWS, L), lambda b, c: (b, 0, c)),
                  cst, nst, mst,
                  pl.BlockSpec((1, W), lambda b, c: (0, 0))],
        out_specs=(pl.BlockSpec((seqs, L, W), lambda b, c: (b, c, 0)), cst, nst, mst),
        scratch_shapes=[pltpu.VMEM((seqs, P, LANES, LANES), F32),
                        pltpu.VMEM((seqs, P, 1, LANES), F32),
                        pltpu.VMEM((seqs, n_heads, LANES), F32)],
        compiler_params=_cparams("parallel", "arbitrary"),
        name="mlstm",
    )(z3, z3, z3, z3, gc.reshape(n_seq, rows, LANES), gr, c0, n0, m0_rep, gain.reshape(1, W))
    return y.reshape(M, W), c_out, n_out, m_out[:, :, 0]


def _fox_prep_kernel(gc_ref, gr_ref, k_ref, qt_ref, vt_ref, ka_ref, qa_ref, vb_ref, cc_sc, cr_sc,
                     *, L, n_heads, g0):
    @pl.when(pl.program_id(1) == 0)
    def _():
        cc_sc[...] = jnp.zeros_like(cc_sc)
        cr_sc[...] = jnp.zeros_like(cr_sc)

    fc = _cumsum_rows(gc_ref[...], L) + cc_sc[...]
    fr = _cumsum_lanes(gr_ref[...], L) + cr_sc[...]
    cc_sc[...] = fc[L - 1:L, :]
    cr_sc[...] = fr[:, L - 1:L]
    qt = qt_ref[...] * SCALE
    sel_r = lax.broadcasted_iota(jnp.int32, (3 * LANES, n_heads * AUG), 0)
    sel_c = lax.broadcasted_iota(jnp.int32, (3 * LANES, n_heads * AUG), 1)
    place = ((sel_r % LANES - g0) == (sel_c // AUG)) & ((sel_c % AUG) == HEAD_DIM + 3 + sel_r // LANES)
    k_aug = jnp.dot(jnp.concatenate(_split3(fc), axis=1), jnp.where(place, -1.0, 0.0).astype(BF16),
                    preferred_element_type=F32)
    lane_in_tile = lax.broadcasted_iota(jnp.int32, (1, n_heads * AUG), 1) % AUG
    k_aug = k_aug + jnp.where((lane_in_tile >= HEAD_DIM) & (lane_in_tile < HEAD_DIM + 3), 1.0, 0.0)
    low = lax.broadcasted_iota(jnp.int32, (1, AUG), 1) < HEAD_DIM
    for t in range(n_heads // 2):
        k_pair = k_ref[:, t * LANES:(t + 1) * LANES]
        for e, k_low in enumerate((k_pair, pltpu.roll(k_pair, HEAD_DIM, 1))):
            h = 2 * t + e
            ka_ref[:, h * AUG:(h + 1) * AUG] = jnp.where(low, k_low, k_aug[:, h * AUG:(h + 1) * AUG]).astype(BF16)
    row = lax.broadcasted_iota(jnp.int32, (HEAD_DIM, L), 0)
    for h in range(n_heads):
        sl = _head_slice(h)
        qh, qm, ql = (p.astype(F32) for p in _split3(fr[g0 + h:g0 + h + 1, :]))
        q_aug = jnp.where(row == 0, qh,
                          jnp.where(row == 1, qm, jnp.where(row == 2, ql, jnp.where(row < 6, 1.0, 0.0))))
        qa_ref[h * AUG:(h + 1) * AUG, :] = jnp.concatenate([qt[sl, :], q_aug], axis=0).astype(BF16)
    vb_ref[...] = vt_ref[...].astype(BF16)


def fox_prep(gc, gr, z, qt, vt, *, n_seq, L, k_col0, n_heads, g0):
    M = gc.shape[0]
    S = M // n_seq
    nc = S // L
    W = n_heads * HEAD_DIM
    t_spec = pl.BlockSpec((None, W, L), lambda b, c: (b, 0, c))
    return pl.pallas_call(
        functools.partial(_fox_prep_kernel, L=L, n_heads=n_heads, g0=g0),
        out_shape=(jax.ShapeDtypeStruct((M, n_heads * AUG), BF16),
                   jax.ShapeDtypeStruct((n_seq, n_heads * AUG, S), BF16),
                   jax.ShapeDtypeStruct((n_seq, W, S), BF16)),
        grid=(n_seq, nc),
        in_specs=[pl.BlockSpec((L, LANES), lambda b, c: (b * nc + c, 0)),
                  pl.BlockSpec((None, GATE_ROWS, L), lambda b, c: (b, 0, c)),
                  pl.BlockSpec((L, W), lambda b, c: (b * nc + c, k_col0 // W)),
                  t_spec, t_spec],
        out_specs=(pl.BlockSpec((L, n_heads * AUG), lambda b, c: (b * nc + c, 0)),
                   pl.BlockSpec((None, n_heads * AUG, L), lambda b, c: (b, 0, c)),
                   t_spec),
        scratch_shapes=[pltpu.VMEM((1, LANES), F32), pltpu.VMEM((GATE_ROWS, 1), F32)],
        compiler_params=_cparams("parallel", "arbitrary"),
        name="fox_prep",
    )(gc, gr, z, qt, vt)


def _fox_flash_kernel(qi_ref, kj_ref, ka_ref, qa_ref, vt_ref, o_ref, m_sc, l_sc, acc_sc, s_sc, p_sc,
                      *, tq, n_heads):
    step = pl.program_id(1)
    qi = qi_ref[step]
    kj = kj_ref[step]

    @pl.when(kj == 0)
    def _():
        m_sc[...] = jnp.full_like(m_sc, NEG_INF)
        l_sc[...] = jnp.zeros_like(l_sc)
        acc_sc[...] = jnp.zeros_like(acc_sc)

    hq = tq // 2

    def rescale(block_max):
        m_old = m_sc[...]
        m_new = jnp.maximum(m_old, jnp.concatenate(block_max, axis=0))
        alpha = jnp.exp(m_old - m_new)
        m_sc[...] = m_new
        return m_new, alpha

    def accumulate(alpha, sums):
        l_sc[...] = alpha * l_sc[...] + jnp.concatenate(sums, axis=0)
        for h in range(n_heads):
            hs = _head_slice(h)
            acc_sc[hs, :] = alpha[h:h + 1, :] * acc_sc[hs, :] + jnp.dot(
                vt_ref[hs, :], p_sc[h], preferred_element_type=F32)

    def update_full():
        block_max = []
        for h in range(n_heads):
            s = jnp.dot(ka_ref[:, h * AUG:(h + 1) * AUG], qa_ref[h * AUG:(h + 1) * AUG, :],
                        preferred_element_type=F32)
            s_sc[h] = s
            block_max.append(jnp.max(s, axis=0, keepdims=True))
        m_new, alpha = rescale(block_max)
        sums = []
        for h in range(n_heads):
            p = jnp.exp(s_sc[h] - m_new[h:h + 1, :])
            sums.append(jnp.sum(p, axis=0, keepdims=True))
            p_sc[h] = p.astype(BF16)
        accumulate(alpha, sums)

    def update_diagonal():
        key = lax.broadcasted_iota(jnp.int32, (hq, tq), 0)
        qry = lax.broadcasted_iota(jnp.int32, (hq, tq), 1)
        early_visible = key <= qry
        late_visible = early_visible[:, :hq]
        no_max = jnp.full((1, hq), NEG_INF, F32)
        block_max = []
        for h in range(n_heads):
            tile = slice(h * AUG, (h + 1) * AUG)
            s_early = jnp.where(early_visible, jnp.dot(ka_ref[:hq, tile], qa_ref[tile, :],
                                                       preferred_element_type=F32), NEG_INF)
            s_late = jnp.where(late_visible, jnp.dot(ka_ref[hq:, tile], qa_ref[tile, hq:],
                                                     preferred_element_type=F32), NEG_INF)
            s_sc[h, :hq, :] = s_early
            s_sc[h, hq:, hq:] = s_late
            block_max.append(jnp.maximum(
                jnp.max(s_early, axis=0, keepdims=True),
                jnp.concatenate([no_max, jnp.max(s_late, axis=0, keepdims=True)], axis=1)))
        m_new, alpha = rescale(block_max)
        sums = []
        for h in range(n_heads):
            p_early = jnp.exp(s_sc[h, :hq, :] - m_new[h:h + 1, :])
            p_late = jnp.exp(s_sc[h, hq:, hq:] - m_new[h:h + 1, hq:])
            sums.append(jnp.sum(p_early, axis=0, keepdims=True) + jnp.concatenate(
                [jnp.zeros((1, hq), F32), jnp.sum(p_late, axis=0, keepdims=True)], axis=1))
            p_sc[h, :hq, :] = p_early.astype(BF16)
            p_sc[h, hq:, :hq] = jnp.zeros((hq, hq), BF16)
            p_sc[h, hq:, hq:] = p_late.astype(BF16)
        accumulate(alpha, sums)

    @pl.when(kj < qi)
    def _():
        update_full()

    @pl.when(kj == qi)
    def _():
        update_diagonal()
        o_ref[...] = jnp.concatenate(
            [acc_sc[_head_slice(h), :] / l_sc[h:h + 1, :] for h in range(n_heads)], axis=0)


def fox_flash(ka, qa, vt, *, tq, n_heads):
    n_seq, W, S = vt.shape
    nq = S // tq
    pairs = [(i, j) for i in range(nq) for j in range(i + 1)]
    qi_tab = jnp.asarray([p[0] for p in pairs], jnp.int32)
    kj_tab = jnp.asarray([p[1] for p in pairs], jnp.int32)
    grid_spec = pltpu.PrefetchScalarGridSpec(
        num_scalar_prefetch=2,
        grid=(n_seq, len(pairs)),
        in_specs=[pl.BlockSpec((tq, n_heads * AUG), lambda b, s, qi, kj: (b * nq + kj[s], 0)),
                  pl.BlockSpec((None, n_heads * AUG, tq), lambda b, s, qi, kj: (b, 0, qi[s])),
                  pl.BlockSpec((None, W, tq), lambda b, s, qi, kj: (b, 0, kj[s]))],
        out_specs=pl.BlockSpec((None, W, tq), lambda b, s, qi, kj: (b, 0, qi[s])),
        scratch_shapes=[pltpu.VMEM((n_heads, tq), F32), pltpu.VMEM((n_heads, tq), F32),
                        pltpu.VMEM((W, tq), F32),
                        pltpu.VMEM((n_heads, tq, tq), F32), pltpu.VMEM((n_heads, tq, tq), BF16)],
    )
    return pl.pallas_call(
        functools.partial(_fox_flash_kernel, tq=tq, n_heads=n_heads),
        out_shape=jax.ShapeDtypeStruct((n_seq, W, S), F32),
        grid_spec=grid_spec,
        compiler_params=_cparams("parallel", "arbitrary"),
        name="fox_flash",
    )(qi_tab, kj_tab, ka, qa, vt)


def _fox_decode_kernel(pt_ref, q_ref, kn_ref, vn_ref, gc_ref, gr_ref, *rest, valid, n_heads, g0, n_pages):
    k_refs = rest[:n_pages]
    v_refs = rest[n_pages:2 * n_pages]
    lf_refs = rest[2 * n_pages:3 * n_pages]
    o_ref = rest[3 * n_pages]
    del pt_ref
    T = SUBLANES
    H = n_heads
    page = lf_refs[0].shape[-1]

    q = q_ref[...] * SCALE
    kn = kn_ref[...]
    vn = vn_ref[...]
    gc = gc_ref[...]
    gr = gr_ref[...]
    t_i = lax.broadcasted_iota(jnp.int32, (T, LANES), 0)
    cum_col = sum(jnp.where(t_i >= u, gc[u:u + 1, :], 0.0) for u in range(valid))
    u_i = lax.broadcasted_iota(jnp.int32, (GATE_ROWS, T), 1)
    cum_row = sum(jnp.where(u_i >= u, gr[:, u:u + 1], 0.0) for u in range(valid))
    t_q = lax.broadcasted_iota(jnp.int32, (T, T), 0)
    u_k = lax.broadcasted_iota(jnp.int32, (T, T), 1)
    new_visible = (u_k <= t_q) & (u_k < valid)

    s_a = lax.broadcasted_iota(jnp.int32, (page, page), 0)
    s_b = lax.broadcasted_iota(jnp.int32, (page, page), 1)
    later = jnp.where(s_a > s_b, 1.0, 0.0).astype(BF16)
    lf = jnp.concatenate([r[...] for r in lf_refs], axis=0)
    within = sum(jnp.dot(p, later, preferred_element_type=F32) for p in _split3(lf))
    totals = within[:, 0:1] + lf[:, 0:1]
    carry = jnp.zeros((H, 1), F32)
    suffix = [None] * n_pages
    for p in reversed(range(n_pages)):
        suffix[p] = within[p * H:(p + 1) * H, :] + carry
        carry = carry + totals[p * H:(p + 1) * H, :]

    s_past, s_new = [], []
    for h in range(H):
        qh = q[:, _head_slice(h)]
        f_t = cum_col[:, g0 + h:g0 + h + 1]
        kt = jnp.concatenate([r[h] for r in k_refs], axis=1)
        bias = jnp.concatenate([sp[h:h + 1, :] for sp in suffix], axis=1)
        s_past.append(_dot(qh, kt) + (f_t + bias))
        sn = _dot_nt(qh, kn[:, _head_slice(h)]) + (f_t - cum_row[g0 + h:g0 + h + 1, :])
        s_new.append(jnp.where(new_visible, sn, NEG_INF))
    p_past, p_new, denom = [], [], []
    for h in range(H):
        m = jnp.maximum(jnp.max(s_past[h], axis=1, keepdims=True), jnp.max(s_new[h], axis=1, keepdims=True))
        p_past.append(jnp.exp(s_past[h] - m))
        p_new.append(jnp.exp(s_new[h] - m))
        denom.append(jnp.sum(p_past[h], axis=1, keepdims=True) + jnp.sum(p_new[h], axis=1, keepdims=True))
    outs = []
    for h in range(H):
        vt = jnp.concatenate([r[h] for r in v_refs], axis=1)
        outs.append((_dot_nt(p_past[h], vt) + _dot(p_new[h], vn[:, _head_slice(h)])) / denom[h])
    o_ref[...] = jnp.concatenate(outs, axis=1)


def fox_decode(z, gc, gr, cache_kt, cache_vt, cache_lf_t, page_table, *, layer, valid, col0, n_heads, g0):
    M = z.shape[0]
    T = SUBLANES
    n_seq = M // T
    W = n_heads * HEAD_DIM
    n_pages = page_table.shape[1]
    page = cache_kt.shape[-1]
    cb = col0 // W

    def page_idx(i, nd):
        return lambda b, pt: (layer, pt[b * n_pages + i]) + (0,) * nd

    kv_specs = [pl.BlockSpec((None, None, n_heads, HEAD_DIM, page), page_idx(i, 3)) for i in range(n_pages)]
    lf_specs = [pl.BlockSpec((None, None, n_heads, page), page_idx(i, 2)) for i in range(n_pages)]
    zspec = lambda c: pl.BlockSpec((T, W), lambda b, pt: (b, c))
    grid_spec = pltpu.PrefetchScalarGridSpec(
        num_scalar_prefetch=1,
        grid=(n_seq,),
        in_specs=[zspec(cb), zspec(cb + 1), zspec(cb + 2),
                  pl.BlockSpec((T, LANES), lambda b, pt: (b, 0)),
                  pl.BlockSpec((None, GATE_ROWS, T), lambda b, pt: (b, 0, 0))] + kv_specs + kv_specs + lf_specs,
        out_specs=pl.BlockSpec((T, W), lambda b, pt: (b, 0)),
    )
    return pl.pallas_call(
        functools.partial(_fox_decode_kernel, valid=valid, n_heads=n_heads, g0=g0, n_pages=n_pages),
        out_shape=jax.ShapeDtypeStruct((M, W), F32),
        grid_spec=grid_spec,
        compiler_params=_cparams("parallel"),
        name="fox_decode",
    )(page_table.reshape(-1), z, z, z, gc, gr,
      *([cache_kt] * n_pages), *([cache_vt] * n_pages), *([cache_lf_t] * n_pages))


def _mix_cross_prompt_kernel(x_ref, r_ref, m_ref, f_ref, wr_ref, wm_ref, wf_ref,
                             g_ref, wq_ref, kt_ref, vt_ref, wo_ref, o_ref, *, n_heads):
    x = (x_ref[...] + _dot(r_ref[...], wr_ref[...]) + _dot(m_ref[...], wm_ref[...])
         + _dot_tn(f_ref[...], wf_ref[...]))
    q = jnp.dot(_rms(x, g_ref[...]).astype(BF16), wq_ref[...], preferred_element_type=F32) * SCALE
    scores = [_dot(q[:, _head_slice(h)], kt_ref[_head_slice(h), :]) for h in range(n_heads)]
    probs = [_softmax_rows(s) for s in scores]
    outs = [_dot_nt(probs[h], vt_ref[_head_slice(h), :]) for h in range(n_heads)]
    o_ref[...] = x + jnp.dot(jnp.concatenate(outs, axis=-1).astype(BF16), wo_ref[...],
                             preferred_element_type=F32)


def mix_cross_prompt(x, r, m, f_t, w_out, g, w_q, mkv_t, w_o, *, n_heads, tm):
    M, D = x.shape
    W = n_heads * HEAD_DIM
    n_seq, _, mem_len = mkv_t.shape
    per_seq = M // n_seq // tm
    wr, wm = r.shape[1], m.shape[1]
    w_r, w_m, w_f = w_out[:wr], w_out[wr:wr + wm], w_out[wr + wm:]
    row = lambda w: pl.BlockSpec((tm, w), lambda i: (i, 0))
    full = lambda a: pl.BlockSpec(a.shape, lambda i: (0, 0))
    return pl.pallas_call(
        functools.partial(_mix_cross_prompt_kernel, n_heads=n_heads),
        out_shape=jax.ShapeDtypeStruct((M, D), F32),
        grid=(M // tm,),
        in_specs=[row(D), row(wr), row(wm),
                  pl.BlockSpec((None, f_t.shape[1], tm), lambda i: (i // per_seq, 0, i % per_seq)),
                  full(w_r), full(w_m), full(w_f),
                  pl.BlockSpec((1, D), lambda i: (0, 0)),
                  full(w_q),
                  pl.BlockSpec((None, W, mem_len), lambda i: (i // per_seq, 0, 0)),
                  pl.BlockSpec((None, W, mem_len), lambda i: (i // per_seq, 1, 0)),
                  full(w_o)],
        out_specs=row(D),
        compiler_params=_cparams("parallel"),
        name="mix_cross_prompt",
    )(x, r, m, f_t, w_r, w_m, w_f, g.reshape(1, D), w_q, mkv_t, mkv_t, w_o)


def _cross_sample_kernel(x_ref, g_ref, wq_ref, kt_ref, vt_ref, wo_ref, o_ref, *, n_heads, seqs):
    T = SUBLANES
    x = x_ref[...]
    q = jnp.dot(_rms(x, g_ref[...]).astype(BF16), wq_ref[...], preferred_element_type=F32) * SCALE
    items = [(b, h) for b in range(seqs) for h in range(n_heads)]
    scores = {(b, h): _dot(q[b * T:(b + 1) * T, _head_slice(h)], kt_ref[b, h]) for b, h in items}
    probs = {i: _softmax_rows(scores[i]) for i in items}
    outs = {(b, h): _dot_nt(probs[b, h], vt_ref[b, h]) for b, h in items}
    rows = [jnp.concatenate([outs[b, h] for h in range(n_heads)], axis=1) for b in range(seqs)]
    o_ref[...] = x + jnp.dot(jnp.concatenate(rows, axis=0).astype(BF16), wo_ref[...],
                             preferred_element_type=F32)


def cross_attn_sample(x, g, w_q, cache_kt, cache_vt, w_o, *, layer, n_heads, seqs):
    M, D = x.shape
    T = SUBLANES
    W = n_heads * HEAD_DIM
    mem_len = cache_kt.shape[-1]
    kv = pl.BlockSpec((None, seqs, n_heads, HEAD_DIM, mem_len), lambda i: (layer, i, 0, 0, 0))
    return pl.pallas_call(
        functools.partial(_cross_sample_kernel, n_heads=n_heads, seqs=seqs),
        out_shape=jax.ShapeDtypeStruct((M, D), F32),
        grid=(M // (seqs * T),),
        in_specs=[pl.BlockSpec((seqs * T, D), lambda i: (i, 0)),
                  pl.BlockSpec((1, D), lambda i: (0, 0)),
                  pl.BlockSpec((D, W), lambda i: (0, 0)),
                  kv, kv,
                  pl.BlockSpec((W, D), lambda i: (0, 0))],
        out_specs=pl.BlockSpec((seqs * T, D), lambda i: (i, 0)),
        compiler_params=_cparams("parallel"),
        name="cross_attn_sample",
    )(x, g.reshape(1, D), w_q, cache_kt, cache_vt, w_o)


def _rope_tables(pos, n_heads):
    inv = ROPE_BASE ** (-jnp.arange(HALF, dtype=F32) / HALF)
    ang = pos.astype(F32)[:, None] * inv[None, :]
    cos, sin = jnp.cos(ang), jnp.sin(ang)
    return (jnp.tile(jnp.concatenate([cos, cos], axis=-1), (1, n_heads)),
            jnp.tile(jnp.concatenate([-sin, sin], axis=-1), (1, n_heads)))


def _pick(n, candidates):
    for c in candidates:
        if n % c == 0:
            return c
    return n


def kernel(x_prompt, x_sample, mem_prompt, state_ret, state_mlstm_c, state_mlstm_n, state_mlstm_m,
           cache_fox_k, cache_fox_v, cache_fox_logf, cache_mem_k, cache_mem_v, page_table,
           g_mix, w_in, b_in, g_ret, g_mlstm, w_out, g_cross, g_mem, w_cq, w_ckv, w_co,
           g_mlp, w_up, w_down, g_final):
    B, S, D = x_prompt.shape
    NB, T, _ = x_sample.shape
    depth = w_in.shape[0]
    RH = state_ret.shape[2]
    MH = state_mlstm_c.shape[2]
    FH = cache_fox_k.shape[3]
    MEMH = cache_mem_k.shape[3]
    RW, MW, FW, MEMW = RH * HEAD_DIM, MH * HEAD_DIM, FH * HEAD_DIM, MEMH * HEAD_DIM
    n_pages = page_table.shape[1]
    page = cache_fox_k.shape[2]
    mem_len = mem_prompt.shape[1]
    TP = SUBLANES
    assert T <= TP and S % CHUNK == 0 and 2 * MH + FH <= GATE_ROWS
    assert RW % LANES == 0 and MW % LANES == 0 and FW % LANES == 0

    sizes = (RW,) * 4 + (MW,) * 4 + (MH, MH) + (FW,) * 3 + (FH,)
    starts = np.concatenate([[0], np.cumsum(sizes)[:-1]])
    cols = lambda a, *ids: jnp.concatenate([a[..., starts[i]:starts[i] + sizes[i]] for i in ids], axis=-1)
    ret_col0, mls_col0 = 0, 4 * RW
    fox_col0 = mls_col0 + 4 * MW
    assert fox_col0 % FW == 0 and mls_col0 % MW == 0
    fox_g0 = 2 * MH

    def regroup(ids, gate_col0):
        ids = ids + (8, 9, 13)
        n_cols = gate_col0 + 2 * LANES
        pad = n_cols - sum(sizes[i] for i in ids)
        return (jnp.pad(cols(w_in, *ids), ((0, 0), (0, 0), (0, pad))).astype(BF16),
                jnp.pad(cols(b_in, *ids), ((0, 0), (0, pad))), n_cols)

    gate_col0_s = fox_col0 + 3 * FW
    w_in_s, b_in_s, n_cols_s = regroup((0, 1, 2, 3, 4, 5, 6, 7, 10, 11, 12), gate_col0_s)
    gate_col0_p = fox_col0 + FW
    w_in_p, b_in_p, n_cols_p = regroup((0, 1, 2, 3, 4, 5, 6, 7, 11), gate_col0_p)
    w_fox_t = jnp.swapaxes(cols(w_in, 10, 11, 12), 1, 2).astype(BF16)
    b_fox = cols(b_in, 10, 11, 12)

    w_out_b, w_cq_b, w_co_b = (w.astype(BF16) for w in (w_out, w_cq, w_co))
    w_ckv_t = jnp.swapaxes(w_ckv, 1, 2).astype(BF16)
    w_up_b, w_down_b = w_up.astype(BF16), w_down.astype(BF16)

    past = n_pages * page
    cos_p, sin_p = _rope_tables(jnp.arange(S, dtype=jnp.int32), RH)
    pos_s = jnp.minimum(past + jnp.arange(TP, dtype=jnp.int32), past + T - 1)
    cos_s, sin_s = _rope_tables(pos_s, RH)
    log_gamma = tuple(float(np.log1p(-np.exp2(-5.0 - h))) for h in range(RH))

    cache_kt = jnp.transpose(cache_fox_k, (0, 1, 3, 4, 2))
    cache_vt = jnp.transpose(cache_fox_v, (0, 1, 3, 4, 2))
    cache_lf_t = jnp.swapaxes(cache_fox_logf, 2, 3)
    mem_kt = jnp.transpose(cache_mem_k, (0, 1, 3, 4, 2))
    mem_vt = jnp.transpose(cache_mem_v, (0, 1, 3, 4, 2))

    Mp, Ms = B * S, NB * TP
    xp = x_prompt.reshape(Mp, D)
    xs = jnp.pad(x_sample, ((0, 0), (0, TP - T), (0, 0))).reshape(Ms, D)
    mem = mem_prompt.reshape(B * mem_len, D)

    tm_p = _pick(S, (1024, 512, 256, 128))
    tm_in = _pick(S, (512, 256, 128))
    tm_s = _pick(Ms, (1024, 512, 256, 128, 64, 32, 16, 8))
    tn_s = _pick(n_cols_s, (1280, 768, 640, 512, 256, 128))
    tn_p = n_cols_p
    tf = _pick(w_up.shape[2], (2048, 1024, 512, 256, 128))
    tq = _pick(S, (512, 256, 128))
    l_prep = _pick(S, (512, 256, 128))
    seqs = _pick(NB, (16, 8, 4, 2, 1))
    seqs_p = _pick(B, (8, 4, 2, 1))
    zeros = lambda *s: jnp.zeros(s, F32)

    def gates_of(z, gate_col0, n_seq, rows, tm):
        gc = gate_activations(z, col_block=gate_col0 // LANES, n_raw=MH, tm=tm)
        gr = jnp.swapaxes(gc[:, :GATE_ROWS].reshape(n_seq, rows, GATE_ROWS), 1, 2)
        return gc, gr

    outs_p = [[] for _ in range(9)]
    outs_s = [[] for _ in range(7)]
    for l in range(depth):
        zp, qt_p, kt_p, vt_p = in_proj_prompt(xp, g_mix[l], w_in_p[l], b_in_p[l], w_fox_t[l], b_fox[l],
                                              n_seq=B, tm=tm_in, tn=tn_p)
        gcp, grp = gates_of(zp, gate_col0_p, B, S, tm_p)
        r_p, S_p = retention(zp, cos_p, sin_p, zeros(B, RH, HEAD_DIM, HEAD_DIM), g_ret[l],
                             n_seq=B, L=CHUNK, valid=CHUNK, col0=ret_col0, log_gamma=log_gamma,
                             seqs=seqs_p)
        m_p, C_p, n_p, mm_p = mlstm(zp, gcp, grp, zeros(B, MH, HEAD_DIM, HEAD_DIM), zeros(B, MH, HEAD_DIM),
                                    zeros(B, MH), g_mlstm[l], n_seq=B, L=CHUNK, valid=CHUNK,
                                    col0=mls_col0, n_heads=MH, seqs=seqs_p)
        ka, qa, vb = fox_prep(gcp, grp, zp, qt_p, vt_p, n_seq=B, L=l_prep, k_col0=fox_col0,
                              n_heads=FH, g0=fox_g0)
        f_p = fox_flash(ka, qa, vb, tq=tq, n_heads=FH)
        mkv_t = norm_matmul_t(mem, g_mem[l], w_ckv_t[l], n_seq=B)
        xp = mix_cross_prompt(xp, r_p, m_p, f_p, w_out_b[l], g_cross[l], w_cq_b[l], mkv_t, w_co_b[l],
                              n_heads=MEMH, tm=tm_p)
        xp = sq_relu_mlp(xp, g_mlp[l], w_up_b[l], w_down_b[l], tm=tm_p, tf=tf)

        zs = norm_matmul(xs, g_mix[l], w_in_s[l], b_in_s[l], tm=tm_s, tn=tn_s)
        gcs, grs = gates_of(zs, gate_col0_s, NB, TP, tm_s)
        r_s, S_s = retention(zs, cos_s, sin_s, state_ret[l], g_ret[l], n_seq=NB, L=TP, valid=T,
                             col0=ret_col0, log_gamma=log_gamma, seqs=seqs)
        m_s, C_s, n_s, mm_s = mlstm(zs, gcs, grs, state_mlstm_c[l], state_mlstm_n[l], state_mlstm_m[l],
                                    g_mlstm[l], n_seq=NB, L=TP, valid=T, col0=mls_col0, n_heads=MH,
                                    seqs=seqs)
        f_s = fox_decode(zs, gcs, grs, cache_kt, cache_vt, cache_lf_t, page_table, layer=l, valid=T,
                         col0=fox_col0, n_heads=FH, g0=fox_g0)
        xs = out_proj(xs, r_s, m_s, f_s, w_out_b[l], tm=tm_s)
        xs = cross_attn_sample(xs, g_cross[l], w_cq_b[l], mem_kt, mem_vt, w_co_b[l], layer=l,
                               n_heads=MEMH, seqs=seqs)
        xs = sq_relu_mlp(xs, g_mlp[l], w_up_b[l], w_down_b[l], tm=tm_s, tf=tf)

        heads_last = lambda a, n: jnp.transpose(a.reshape(a.shape[0], n, HEAD_DIM, a.shape[-1]), (0, 3, 1, 2))
        fks = lambda c: zs[:, fox_col0 + c * FW:fox_col0 + (c + 1) * FW].reshape(NB, TP, FH, HEAD_DIM)[:, :T]
        new_p = (S_p, C_p, n_p, mm_p,
                 heads_last(kt_p, FH), heads_last(vt_p, FH),
                 gcp[:, fox_g0:fox_g0 + FH].reshape(B, S, FH),
                 heads_last(mkv_t[:, :MEMW], MEMH), heads_last(mkv_t[:, MEMW:], MEMH))
        new_s = (S_s, C_s, n_s, mm_s, fks(1), fks(2),
                 gcs[:, fox_g0:fox_g0 + FH].reshape(NB, TP, FH)[:, :T])
        for lst, a in zip(outs_p, new_p):
            lst.append(a)
        for lst, a in zip(outs_s, new_s):
            lst.append(a)

    y_prompt = final_norm(xp, g_final, tm=tm_p).reshape(B, S, D)
    y_sample = final_norm(xs, g_final, tm=tm_s).reshape(NB, TP, D)[:, :T]
    ret_p, c_p, n_p, m_p, fk_p, fv_p, flf_p, memk_p, memv_p = [jnp.stack(a, axis=0) for a in outs_p]
    ret_s, c_s, n_s, m_s, fk_s, fv_s, flf_s = [jnp.stack(a, axis=0) for a in outs_s]
    return (y_prompt, y_sample, ret_p, ret_s, c_p, c_s, n_p, n_s, m_p, m_s,
            fk_p, fk_s, fv_p, fv_s, flf_p, flf_s, memk_p, memv_p)
```

```python
import functools
import math

import numpy as np
import jax
import jax.numpy as jnp
from jax import lax
from jax.experimental import pallas as pl
from jax.experimental.pallas import tpu as pltpu

F32 = jnp.float32
BF16 = jnp.bfloat16

HEAD_DIM = 64
HALF = HEAD_DIM // 2
EPS = 1e-6
ROPE_BASE = 10000.0
CHUNK = 128
SUBLANES = 8
LANES = 128
GATE_ROWS = 16
AUG = 2 * HEAD_DIM
SCALE = HEAD_DIM ** -0.5
VMEM_LIMIT = 56 * 1024 * 1024
NEG_INF = float("-inf")


def _cparams(*sem):
    return pltpu.CompilerParams(dimension_semantics=sem, vmem_limit_bytes=VMEM_LIMIT)


def _dot(a, b):
    return jnp.dot(a.astype(BF16), b.astype(BF16), preferred_element_type=F32)


def _dot_nt(a, b):
    return lax.dot_general(a.astype(BF16), b.astype(BF16), (((1,), (1,)), ((), ())),
                           preferred_element_type=F32)


def _dot_tn(a, b):
    return lax.dot_general(a.astype(BF16), b.astype(BF16), (((0,), (0,)), ((), ())),
                           preferred_element_type=F32)


def _split3(x):
    hi = x.astype(BF16)
    r1 = x - hi.astype(F32)
    mid = r1.astype(BF16)
    lo = (r1 - mid.astype(F32)).astype(BF16)
    return hi, mid, lo


def _cumsum_rows(x, n):
    t = lax.broadcasted_iota(jnp.int32, (n, n), 0)
    s = lax.broadcasted_iota(jnp.int32, (n, n), 1)
    tri = jnp.where(t >= s, 1.0, 0.0).astype(BF16)
    return sum(jnp.dot(tri, p, preferred_element_type=F32) for p in _split3(x))


def _cumsum_lanes(x, n):
    t = lax.broadcasted_iota(jnp.int32, (n, n), 0)
    s = lax.broadcasted_iota(jnp.int32, (n, n), 1)
    tri = jnp.where(t <= s, 1.0, 0.0).astype(BF16)
    return sum(jnp.dot(p, tri, preferred_element_type=F32) for p in _split3(x))


def _rms(x, g):
    return x * lax.rsqrt(jnp.mean(x * x, axis=-1, keepdims=True) + EPS) * g


def _sigmoid(x):
    return 1.0 / (1.0 + jnp.exp(-x))


def _log_sigmoid(x):
    return jnp.minimum(x, 0.0) - jnp.log1p(jnp.exp(-jnp.abs(x)))


def _head_norm(o):
    return o * lax.rsqrt(jnp.mean(o * o, axis=-1, keepdims=True) + EPS)


def _softmax_rows(s):
    p = jnp.exp(s - jnp.max(s, axis=1, keepdims=True))
    return p / jnp.sum(p, axis=1, keepdims=True)


def _head_slice(h):
    return slice(h * HEAD_DIM, (h + 1) * HEAD_DIM)


def _norm_matmul_kernel(x_ref, g_ref, w_ref, b_ref, o_ref, xn_ref):
    @pl.when(pl.program_id(1) == 0)
    def _():
        xn_ref[...] = _rms(x_ref[...], g_ref[...]).astype(BF16)

    o_ref[...] = jnp.dot(xn_ref[...], w_ref[...], preferred_element_type=F32) + b_ref[...]


def norm_matmul(x, g, w, b, *, tm, tn):
    M, D = x.shape
    N = w.shape[1]
    return pl.pallas_call(
        _norm_matmul_kernel,
        out_shape=jax.ShapeDtypeStruct((M, N), F32),
        grid=(M // tm, N // tn),
        in_specs=[pl.BlockSpec((tm, D), lambda i, j: (i, 0)),
                  pl.BlockSpec((1, D), lambda i, j: (0, 0)),
                  pl.BlockSpec((D, tn), lambda i, j: (0, j)),
                  pl.BlockSpec((1, tn), lambda i, j: (0, j))],
        out_specs=pl.BlockSpec((tm, tn), lambda i, j: (i, j)),
        scratch_shapes=[pltpu.VMEM((tm, D), BF16)],
        compiler_params=_cparams("parallel", "arbitrary"),
        name="norm_matmul",
    )(x, g.reshape(1, D), w, b.reshape(1, N))


def _in_proj_prompt_kernel(x_ref, g_ref, w_ref, b_ref, wt_ref, bt_ref, k_all_ref, v_all_ref,
                           z_ref, qt_ref, kt_ref, vt_ref, xn_ref, *, fw):
    del k_all_ref, v_all_ref
    @pl.when(pl.program_id(2) == 0)
    def _():
        xn_ref[...] = _rms(x_ref[...], g_ref[...]).astype(BF16)

    z_ref[...] = jnp.dot(xn_ref[...], w_ref[...], preferred_element_type=F32) + b_ref[...]

    @pl.when(pl.program_id(2) == pl.num_programs(2) - 1)
    def _():
        zt = lax.dot_general(wt_ref[...], xn_ref[...], (((1,), (1,)), ((), ())),
                             preferred_element_type=F32) + bt_ref[...]
        qt_ref[...] = zt[:fw]
        kt_ref[...] = zt[fw:2 * fw]
        vt_ref[...] = zt[2 * fw:]


def in_proj_prompt(x, g, w, b, wt, bt, k_all, v_all, *, layer, n_seq, tm, tn):
    M, D = x.shape
    N = w.shape[1]
    S = M // n_seq
    fw = wt.shape[0] // 3
    nt = S // tm
    slab = pl.BlockSpec((None, None, fw, tm), lambda s, i, j: (layer, s, 0, i))
    stacked = jax.ShapeDtypeStruct(k_all.shape, F32)
    return pl.pallas_call(
        functools.partial(_in_proj_prompt_kernel, fw=fw),
        out_shape=(jax.ShapeDtypeStruct((M, N), F32), jax.ShapeDtypeStruct((n_seq, fw, S), F32),
                   stacked, stacked),
        grid=(n_seq, nt, N // tn),
        in_specs=[pl.BlockSpec((tm, D), lambda s, i, j: (s * nt + i, 0)),
                  pl.BlockSpec((1, D), lambda s, i, j: (0, 0)),
                  pl.BlockSpec((D, tn), lambda s, i, j: (0, j)),
                  pl.BlockSpec((1, tn), lambda s, i, j: (0, j)),
                  pl.BlockSpec((3 * fw, D), lambda s, i, j: (0, 0)),
                  pl.BlockSpec((3 * fw, 1), lambda s, i, j: (0, 0)),
                  pl.BlockSpec(memory_space=pl.ANY), pl.BlockSpec(memory_space=pl.ANY)],
        out_specs=(pl.BlockSpec((tm, tn), lambda s, i, j: (s * nt + i, j)),
                   pl.BlockSpec((None, fw, tm), lambda s, i, j: (s, 0, i)), slab, slab),
        scratch_shapes=[pltpu.VMEM((tm, D), BF16)],
        input_output_aliases={6: 2, 7: 3},
        compiler_params=_cparams("parallel", "parallel", "arbitrary"),
        name="in_proj_prompt",
    )(x, g.reshape(1, D), w, b.reshape(1, N), wt, bt.reshape(3 * fw, 1), k_all, v_all)


def _norm_matmul_t_kernel(x_ref, g_ref, wt_ref, o_ref):
    xn = _rms(x_ref[...], g_ref[...]).astype(BF16)
    o_ref[...] = lax.dot_general(wt_ref[...], xn, (((1,), (1,)), ((), ())), preferred_element_type=F32)


def norm_matmul_t(x, g, wt, *, n_seq):
    M, D = x.shape
    N = wt.shape[0]
    R = M // n_seq
    return pl.pallas_call(
        _norm_matmul_t_kernel,
        out_shape=jax.ShapeDtypeStruct((n_seq, N, R), F32),
        grid=(n_seq,),
        in_specs=[pl.BlockSpec((R, D), lambda s: (s, 0)),
                  pl.BlockSpec((1, D), lambda s: (0, 0)),
                  pl.BlockSpec((N, D), lambda s: (0, 0))],
        out_specs=pl.BlockSpec((None, N, R), lambda s: (s, 0, 0)),
        compiler_params=_cparams("parallel"),
        name="norm_matmul_t",
    )(x, g.reshape(1, D), wt)


def _gates_kernel(z_ref, o_ref, *, n_raw):
    z = z_ref[...]
    lane = lax.broadcasted_iota(jnp.int32, z.shape, 1)
    o_ref[...] = jnp.where(lane >= n_raw, _log_sigmoid(z), z)


def gate_activations(z, *, col_block, n_raw, tm):
    M = z.shape[0]
    return pl.pallas_call(
        functools.partial(_gates_kernel, n_raw=n_raw),
        out_shape=jax.ShapeDtypeStruct((M, LANES), F32),
        grid=(M // tm,),
        in_specs=[pl.BlockSpec((tm, LANES), lambda i: (i, col_block))],
        out_specs=pl.BlockSpec((tm, LANES), lambda i: (i, 0)),
        compiler_params=_cparams("parallel"),
        name="gate_activations",
    )(z)


def _out_proj_kernel(x_ref, r_ref, m_ref, f_ref, wr_ref, wm_ref, wf_ref, o_ref):
    o_ref[...] = (x_ref[...] + _dot(r_ref[...], wr_ref[...]) + _dot(m_ref[...], wm_ref[...])
                  + _dot(f_ref[...], wf_ref[...]))


def out_proj(x, r, m, f, w_out, *, tm):
    M, D = x.shape
    wr, wm = r.shape[1], m.shape[1]
    w_r, w_m, w_f = w_out[:wr], w_out[wr:wr + wm], w_out[wr + wm:]
    row = lambda w: pl.BlockSpec((tm, w), lambda i: (i, 0))
    full = lambda a: pl.BlockSpec(a.shape, lambda i: (0, 0))
    return pl.pallas_call(
        _out_proj_kernel,
        out_shape=jax.ShapeDtypeStruct((M, D), F32),
        grid=(M // tm,),
        in_specs=[row(D), row(wr), row(wm), row(f.shape[1]), full(w_r), full(w_m), full(w_f)],
        out_specs=row(D),
        compiler_params=_cparams("parallel"),
        name="out_proj",
    )(x, r, m, f, w_r, w_m, w_f)


def _mlp_kernel(x_ref, g_ref, wu_ref, wd_ref, o_ref, xn_ref, acc_ref):
    j = pl.program_id(1)

    @pl.when(j == 0)
    def _():
        xn_ref[...] = _rms(x_ref[...], g_ref[...]).astype(BF16)
        acc_ref[...] = x_ref[...]

    h = jnp.maximum(jnp.dot(xn_ref[...], wu_ref[...], preferred_element_type=F32), 0.0)
    acc_ref[...] += jnp.dot((h * h).astype(BF16), wd_ref[...], preferred_element_type=F32)

    @pl.when(j == pl.num_programs(1) - 1)
    def _():
        o_ref[...] = acc_ref[...]


def sq_relu_mlp(x, g, w_up, w_down, *, tm, tf):
    M, D = x.shape
    FF = w_up.shape[1]
    return pl.pallas_call(
        _mlp_kernel,
        out_shape=jax.ShapeDtypeStruct((M, D), F32),
        grid=(M // tm, FF // tf),
        in_specs=[pl.BlockSpec((tm, D), lambda i, j: (i, 0)),
                  pl.BlockSpec((1, D), lambda i, j: (0, 0)),
                  pl.BlockSpec((D, tf), lambda i, j: (0, j)),
                  pl.BlockSpec((tf, D), lambda i, j: (j, 0))],
        out_specs=pl.BlockSpec((tm, D), lambda i, j: (i, 0)),
        scratch_shapes=[pltpu.VMEM((tm, D), BF16), pltpu.VMEM((tm, D), F32)],
        compiler_params=_cparams("parallel", "arbitrary"),
        name="sq_relu_mlp",
    )(x, g.reshape(1, D), w_up, w_down)


def _final_norm_kernel(x_ref, g_ref, o_ref):
    o_ref[...] = _rms(x_ref[...], g_ref[...])


def final_norm(x, g, *, tm):
    M, D = x.shape
    return pl.pallas_call(
        _final_norm_kernel,
        out_shape=jax.ShapeDtypeStruct((M, D), F32),
        grid=(M // tm,),
        in_specs=[pl.BlockSpec((tm, D), lambda i: (i, 0)), pl.BlockSpec((1, D), lambda i: (0, 0))],
        out_specs=pl.BlockSpec((tm, D), lambda i: (i, 0)),
        compiler_params=_cparams("parallel"),
        name="final_norm",
    )(x, g.reshape(1, D))


def _retention_kernel(q_ref, k_ref, v_ref, g_ref, cos_ref, sin_ref, s0_ref, gain_ref,
                      y_ref, sout_ref, s_sc, *, L, valid, log_gamma, seqs):
    c = pl.program_id(1)
    n_heads = len(log_gamma)
    W = n_heads * HEAD_DIM
    P = n_heads // 2
    HD = HEAD_DIM
    upper = lax.broadcasted_iota(jnp.int32, (1, LANES), 1) >= HD
    r_i = lax.broadcasted_iota(jnp.int32, (LANES, LANES), 0)
    c_i = lax.broadcasted_iota(jnp.int32, (LANES, LANES), 1)
    same_half = (r_i >= HD) == (c_i >= HD)
    half_ones = jnp.where(same_half, 1.0, 0.0).astype(BF16)

    @pl.when(c == 0)
    def _():
        zero = jnp.zeros((HD, HD), F32)
        for j in range(seqs):
            for p in range(P):
                s_sc[j, p] = jnp.concatenate(
                    [jnp.concatenate([s0_ref[j, 2 * p], zero], axis=1),
                     jnp.concatenate([zero, s0_ref[j, 2 * p + 1]], axis=1)], axis=0)

    cos = cos_ref[...]
    sin = sin_ref[...]
    lane = lax.broadcasted_iota(jnp.int32, (L, W), 1)
    first_half = (lane % HD) < HALF

    def rope(x):
        swapped = jnp.where(first_half, pltpu.roll(x, W - HALF, 1), pltpu.roll(x, HALF, 1))
        return x * cos + swapped * sin

    t_i = lax.broadcasted_iota(jnp.int32, (L, L), 0)
    s_i = lax.broadcasted_iota(jnp.int32, (L, L), 1)
    causal = t_i >= s_i
    diff = jnp.where(causal, (t_i - s_i).astype(F32), 0.0)
    decay = [jnp.where(causal, jnp.exp(lg * diff), 0.0) for lg in log_gamma]
    t_rep = lax.broadcasted_iota(jnp.int32, (L, LANES), 0).astype(F32)
    pair_lg = [jnp.where(upper, log_gamma[2 * p + 1], log_gamma[2 * p]) for p in range(P)]
    q_decay = [jnp.exp(lg * (t_rep + 1.0)) for lg in pair_lg]
    k_decay = [jnp.where(t_rep < valid, jnp.exp(lg * (valid - 1.0 - t_rep)), 0.0) for lg in pair_lg]
    s_decay = [jnp.exp(lg * float(valid)) for lg in pair_lg]

    pairs = [(j, p) for j in range(seqs) for p in range(P)]
    q, k, v = {}, {}, {}
    for j in range(seqs):
        qj = rope(q_ref[j])
        kj = rope(k_ref[j]) * SCALE
        for p in range(P):
            tile = slice(p * LANES, (p + 1) * LANES)
            q[j, p], k[j, p], v[j, p] = qj[:, tile], kj[:, tile], v_ref[j, :, tile]
    scores, cross = {}, {}
    for j, p in pairs:
        qb = q[j, p].astype(BF16)
        for e in range(2):
            scores[j, 2 * p + e] = _dot_nt(qb, jnp.where(upper if e else ~upper, k[j, p], 0.0))
        cross[j, p] = _dot(qb, s_sc[j, p])
    outs = {}
    for j, p in pairs:
        lhs = jnp.concatenate([scores[j, 2 * p] * decay[2 * p], scores[j, 2 * p + 1] * decay[2 * p + 1]],
                              axis=1)
        v_sel = jnp.concatenate([jnp.where(upper, 0.0, v[j, p]), jnp.where(upper, v[j, p], 0.0)], axis=0)
        o = _dot(lhs, v_sel) + cross[j, p] * q_decay[p]
        o2 = o * o
        hi = o2.astype(BF16)
        lo = (o2 - hi.astype(F32)).astype(BF16)
        mean_sq = (jnp.dot(hi, half_ones, preferred_element_type=F32)
                   + jnp.dot(lo, half_ones, preferred_element_type=F32)) * (1.0 / HD)
        outs[j, p] = o * lax.rsqrt(mean_sq + EPS)
    for j, p in pairs:
        update = jnp.where(same_half, _dot_tn(k[j, p] * k_decay[p], v[j, p]), 0.0)
        s_sc[j, p] = s_decay[p] * s_sc[j, p] + update
    for j in range(seqs):
        g = g_ref[j]
        y_ref[j] = jnp.concatenate([outs[j, p] for p in range(P)], axis=-1) * gain_ref[...] * (
            g * _sigmoid(g))

    @pl.when(c == pl.num_programs(1) - 1)
    def _():
        for j in range(seqs):
            for p in range(P):
                blk = s_sc[j, p]
                sout_ref[j, 2 * p] = blk[:HD, :HD]
                sout_ref[j, 2 * p + 1] = blk[HD:, HD:]


def retention(z, cos, sin, s0, gain, *, n_seq, L, valid, col0, log_gamma, seqs):
    M, N = z.shape
    n_heads = len(log_gamma)
    W = n_heads * HEAD_DIM
    rows = M // n_seq
    cb = col0 // W
    z3 = z.reshape(n_seq, rows, N)
    zspec = lambda j: pl.BlockSpec((seqs, L, W), lambda b, c: (b, c, cb + j))
    tab = pl.BlockSpec((L, W), lambda b, c: (c, 0))
    st = pl.BlockSpec((seqs, n_heads, HEAD_DIM, HEAD_DIM), lambda b, c: (b, 0, 0, 0))
    y, s_out = pl.pallas_call(
        functools.partial(_retention_kernel, L=L, valid=valid, log_gamma=log_gamma, seqs=seqs),
        out_shape=(jax.ShapeDtypeStruct((n_seq, rows, W), F32),
                   jax.ShapeDtypeStruct((n_seq, n_heads, HEAD_DIM, HEAD_DIM), F32)),
        grid=(n_seq // seqs, rows // L),
        in_specs=[zspec(0), zspec(1), zspec(2), zspec(3), tab, tab, st,
                  pl.BlockSpec((1, W), lambda b, c: (0, 0))],
        out_specs=(pl.BlockSpec((seqs, L, W), lambda b, c: (b, c, 0)), st),
        scratch_shapes=[pltpu.VMEM((seqs, n_heads // 2, LANES, LANES), F32)],
        compiler_params=_cparams("parallel", "arbitrary"),
        name="retention",
    )(z3, z3, z3, z3, cos, sin, s0, gain.reshape(1, W))
    return y.reshape(M, W), s_out


def _cumsum_rows_any(x, n):
    if n > SUBLANES:
        return _cumsum_rows(x, n)
    t = lax.broadcasted_iota(jnp.int32, x.shape, 0)
    return sum(jnp.where(t >= u, x[u:u + 1, :], 0.0) for u in range(n))


def _cumsum_lanes_any(x, n):
    if n > SUBLANES:
        return _cumsum_lanes(x, n)
    s = lax.broadcasted_iota(jnp.int32, x.shape, 1)
    return sum(jnp.where(s >= u, x[:, u:u + 1], 0.0) for u in range(n))


def _cummax_rows(x, n):
    row = lax.broadcasted_iota(jnp.int32, x.shape, 0)
    shift = 1
    while shift < n:
        x = jnp.maximum(x, jnp.where(row >= shift, pltpu.roll(x, shift, 0), NEG_INF))
        shift *= 2
    return x


def _mlstm_kernel(q_ref, k_ref, v_ref, og_ref, gc_ref, gr_ref, c0_ref, n0_ref, m0_ref, gain_ref,
                  y_ref, cout_ref, nout_ref, mout_ref, c_sc, n_sc, m_sc, *, L, valid, n_heads, seqs):
    c = pl.program_id(1)
    P = n_heads // 2
    HD = HEAD_DIM
    lane = lax.broadcasted_iota(jnp.int32, (1, LANES), 1)
    upper = lane >= HD
    r_i = lax.broadcasted_iota(jnp.int32, (LANES, LANES), 0)
    c_i = lax.broadcasted_iota(jnp.int32, (LANES, LANES), 1)
    same_half = (r_i >= HD) == (c_i >= HD)
    half_ones = jnp.where(same_half, 1.0, 0.0).astype(BF16)

    @pl.when(c == 0)
    def _():
        zero = jnp.zeros((HD, HD), F32)
        for j in range(seqs):
            for p in range(P):
                c_sc[j, p] = jnp.concatenate(
                    [jnp.concatenate([c0_ref[j, 2 * p], zero], axis=1),
                     jnp.concatenate([zero, c0_ref[j, 2 * p + 1]], axis=1)], axis=0)
                n_sc[j, p] = jnp.concatenate([n0_ref[j, 2 * p:2 * p + 1, :], n0_ref[j, 2 * p + 1:2 * p + 2, :]],
                                             axis=1)
        m_sc[...] = m0_ref[...]

    t_i = lax.broadcasted_iota(jnp.int32, (L, L), 0)
    s_i = lax.broadcasted_iota(jnp.int32, (L, L), 1)
    causal = t_i >= s_i
    row_valid = lax.broadcasted_iota(jnp.int32, (L, LANES), 0) < valid
    rr = lax.broadcasted_iota(jnp.int32, (2 * L, LANES), 0)
    cc = lax.broadcasted_iota(jnp.int32, (2 * L, LANES), 1)
    sum_sel = jnp.where((rr >= L) == (cc >= HD), 1.0, 0.0)

    pairs = [(j, p) for j in range(seqs) for p in range(P)]
    heads = [(j, h) for j in range(seqs) for h in range(n_heads)]
    q, k, v, og = {}, {}, {}, {}
    f_rep, i_rep, w_row, m_prev = {}, {}, {}, {}
    for j in range(seqs):
        gc = gc_ref[j]
        gr = gr_ref[j]
        f_col = _cumsum_rows_any(gc, L)
        f_row = _cumsum_lanes_any(gr, L)
        for p in range(P):
            tile = slice(p * LANES, (p + 1) * LANES)
            q[j, p] = q_ref[j, :, tile]
            k[j, p] = k_ref[j, :, tile] * SCALE
            v[j, p] = v_ref[j, :, tile]
            og[j, p] = _sigmoid(og_ref[j, :, tile])
        for h in range(n_heads):
            f_rep[j, h] = jnp.broadcast_to(f_col[:, n_heads + h:n_heads + h + 1], (L, LANES))
            i_rep[j, h] = jnp.broadcast_to(gc[:, h:h + 1], (L, LANES))
            w_row[j, h] = f_row[n_heads + h:n_heads + h + 1, :] - gr[h:h + 1, :]
            m_prev[j, h] = m_sc[j, h:h + 1, :]

    qk_raw, cq, nq = {}, {}, {}
    for j, p in pairs:
        qb = q[j, p].astype(BF16)
        for e in range(2):
            k_half = jnp.where(upper if e else ~upper, k[j, p], 0.0)
            qk_raw[j, 2 * p + e] = _dot_nt(qb, k_half)
        cq[j, p] = _dot_nt(qb, c_sc[j, p])
        n_rep = jnp.where(same_half, jnp.broadcast_to(n_sc[j, p], (LANES, LANES)), 0.0)
        nq[j, p] = _dot_nt(qb, n_rep)

    m_t, u, w_inter, e_rep = {}, {}, {}, {}
    for j, h in heads:
        peak = jnp.maximum(m_prev[j, h], _cummax_rows(i_rep[j, h] - f_rep[j, h], L))
        m_t[j, h] = f_rep[j, h] + peak
        u[j, h] = f_rep[j, h] - m_t[j, h]
        w_inter[j, h] = jnp.exp(u[j, h] + m_prev[j, h])
        e_rep[j, h] = jnp.exp(-m_t[j, h])

    outs = {}
    for j, p in pairs:
        h0, h1 = 2 * p, 2 * p + 1
        gated = [qk_raw[j, h] * jnp.where(causal, jnp.exp(u[j, h][:, :L] - w_row[j, h]), 0.0)
                 for h in (h0, h1)]
        lhs = jnp.concatenate(gated, axis=1)
        v_sel = jnp.concatenate([jnp.where(upper, 0.0, v[j, p]), jnp.where(upper, v[j, p], 0.0)], axis=0)
        mix = _dot(lhs, jnp.concatenate([v_sel, sum_sel], axis=1))
        w_pair = jnp.where(upper, w_inter[j, h1], w_inter[j, h0])
        num = mix[:, :LANES] + w_pair * cq[j, p]
        den = mix[:, LANES:] + w_pair * nq[j, p]
        floor = jnp.where(upper, e_rep[j, h1], e_rep[j, h0])
        y = og[j, p] * (num / jnp.maximum(jnp.abs(den), floor))
        y2 = y * y
        hi = y2.astype(BF16)
        lo = (y2 - hi.astype(F32)).astype(BF16)
        mean_sq = (jnp.dot(hi, half_ones, preferred_element_type=F32)
                   + jnp.dot(lo, half_ones, preferred_element_type=F32)) * (1.0 / HD)
        outs[j, p] = y * lax.rsqrt(mean_sq + EPS)

    for j, p in pairs:
        h0, h1 = 2 * p, 2 * p + 1
        w_s, w_c, m_last = {}, {}, {}
        for h in (h0, h1):
            m_last[h] = m_t[j, h][valid - 1:valid, :]
            f_last = f_rep[j, h][valid - 1:valid, :]
            w_s[h] = jnp.where(row_valid, jnp.exp(f_last - f_rep[j, h] + i_rep[j, h] - m_last[h]), 0.0)
            w_c[h] = jnp.exp(f_last + m_prev[j, h] - m_last[h])
        ws_pair = jnp.where(upper, w_s[h1], w_s[h0])
        wc_pair = jnp.where(upper, w_c[h1], w_c[h0])
        update = jnp.where(same_half, _dot_tn(v[j, p] * ws_pair, k[j, p]), 0.0)
        c_sc[j, p] = wc_pair * c_sc[j, p] + update
        n_sc[j, p] = wc_pair * n_sc[j, p] + jnp.sum(k[j, p] * ws_pair, axis=0, keepdims=True)
        m_sc[j, h0:h0 + 1, :] = m_last[h0]
        m_sc[j, h1:h1 + 1, :] = m_last[h1]
    for j in range(seqs):
        y_ref[j] = jnp.concatenate([outs[j, p] for p in range(P)], axis=-1) * gain_ref[...]

    @pl.when(c == pl.num_programs(1) - 1)
    def _():
        for j in range(seqs):
            for p in range(P):
                blk = c_sc[j, p]
                row = n_sc[j, p]
                cout_ref[j, 2 * p] = blk[:HD, :HD]
                cout_ref[j, 2 * p + 1] = blk[HD:, HD:]
                nout_ref[j, 2 * p:2 * p + 1, :] = row[:, :HD]
                nout_ref[j, 2 * p + 1:2 * p + 2, :] = row[:, HD:]
        mout_ref[...] = m_sc[...]


def mlstm(z, gc, gr, c0, n0, m0, gain, *, n_seq, L, valid, col0, n_heads, seqs):
    M, N = z.shape
    W = n_heads * HEAD_DIM
    P = n_heads // 2
    assert n_heads % 2 == 0 and L <= LANES
    rows = M // n_seq
    cb = col0 // W
    z3 = z.reshape(n_seq, rows, N)
    zspec = lambda j: pl.BlockSpec((seqs, L, W), lambda b, c: (b, c, cb + j))
    cst = pl.BlockSpec((seqs, n_heads, HEAD_DIM, HEAD_DIM), lambda b, c: (b, 0, 0, 0))
    nst = pl.BlockSpec((seqs, n_heads, HEAD_DIM), lambda b, c: (b, 0, 0))
    mst = pl.BlockSpec((seqs, n_heads, LANES), lambda b, c: (b, 0, 0))
    m0_rep = jnp.broadcast_to(m0[:, :, None], (n_seq, n_heads, LANES))
    y, c_out, n_out, m_out = pl.pallas_call(
        functools.partial(_mlstm_kernel, L=L, valid=valid, n_heads=n_heads, seqs=seqs),
        out_shape=(jax.ShapeDtypeStruct((n_seq, rows, W), F32),
                   jax.ShapeDtypeStruct((n_seq, n_heads, HEAD_DIM, HEAD_DIM), F32),
                   jax.ShapeDtypeStruct((n_seq, n_heads, HEAD_DIM), F32),
                   jax.ShapeDtypeStruct((n_seq, n_heads, LANES), F32)),
        grid=(n_seq // seqs, rows // L),
        in_specs=[zspec(0), zspec(1), zspec(2), zspec(3),
                  pl.BlockSpec((seqs, L, LANES), lambda b, c: (b, c, 0)),
                  pl.BlockSpec((seqs, GATE_ROWS, L), lambda b, c: (b, 0, c)),
                  cst, nst, mst,
                  pl.BlockSpec((1, W), lambda b, c: (0, 0))],
        out_specs=(pl.BlockSpec((seqs, L, W), lambda b, c: (b, c, 0)), cst, nst, mst),
        scratch_shapes=[pltpu.VMEM((seqs, P, LANES, LANES), F32),
                        pltpu.VMEM((seqs, P, 1, LANES), F32),
                        pltpu.VMEM((seqs, n_heads, LANES), F32)],
        compiler_params=_cparams("parallel", "arbitrary"),
        name="mlstm",
    )(z3, z3, z3, z3, gc.reshape(n_seq, rows, LANES), gr, c0, n0, m0_rep, gain.reshape(1, W))
    return y.reshape(M, W), c_out, n_out, m_out[:, :, 0]


def _fox_prep_kernel(gc_ref, gr_ref, k_ref, qt_ref, vt_ref, ka_ref, qa_ref, vb_ref, cc_sc, cr_sc,
                     *, L, n_heads, g0):
    @pl.when(pl.program_id(1) == 0)
    def _():
        cc_sc[...] = jnp.zeros_like(cc_sc)
        cr_sc[...] = jnp.zeros_like(cr_sc)

    fc = _cumsum_rows(gc_ref[...], L) + cc_sc[...]
    fr = _cumsum_lanes(gr_ref[...], L) + cr_sc[...]
    cc_sc[...] = fc[L - 1:L, :]
    cr_sc[...] = fr[:, L - 1:L]
    qt = qt_ref[...] * SCALE
    sel_r = lax.broadcasted_iota(jnp.int32, (3 * LANES, n_heads * AUG), 0)
    sel_c = lax.broadcasted_iota(jnp.int32, (3 * LANES, n_heads * AUG), 1)
    place = ((sel_r % LANES - g0) == (sel_c // AUG)) & ((sel_c % AUG) == HEAD_DIM + 3 + sel_r // LANES)
    k_aug = jnp.dot(jnp.concatenate(_split3(fc), axis=1), jnp.where(place, -1.0, 0.0).astype(BF16),
                    preferred_element_type=F32)
    lane_in_tile = lax.broadcasted_iota(jnp.int32, (1, n_heads * AUG), 1) % AUG
    k_aug = k_aug + jnp.where((lane_in_tile >= HEAD_DIM) & (lane_in_tile < HEAD_DIM + 3), 1.0, 0.0)
    low = lax.broadcasted_iota(jnp.int32, (1, AUG), 1) < HEAD_DIM
    for t in range(n_heads // 2):
        k_pair = k_ref[:, t * LANES:(t + 1) * LANES]
        for e, k_low in enumerate((k_pair, pltpu.roll(k_pair, HEAD_DIM, 1))):
            h = 2 * t + e
            ka_ref[:, h * AUG:(h + 1) * AUG] = jnp.where(low, k_low, k_aug[:, h * AUG:(h + 1) * AUG]).astype(BF16)
    row = lax.broadcasted_iota(jnp.int32, (HEAD_DIM, L), 0)
    for h in range(n_heads):
        sl = _head_slice(h)
        qh, qm, ql = (p.astype(F32) for p in _split3(fr[g0 + h:g0 + h + 1, :]))
        q_aug = jnp.where(row == 0, qh,
                          jnp.where(row == 1, qm, jnp.where(row == 2, ql, jnp.where(row < 6, 1.0, 0.0))))
        qa_ref[h * AUG:(h + 1) * AUG, :] = jnp.concatenate([qt[sl, :], q_aug], axis=0).astype(BF16)
    vb_ref[...] = vt_ref[...].astype(BF16)


def fox_prep(gc, gr, z, qt, v_all, *, layer, n_seq, L, k_col0, n_heads, g0):
    M = gc.shape[0]
    S = M // n_seq
    nc = S // L
    W = n_heads * HEAD_DIM
    t_spec = pl.BlockSpec((None, W, L), lambda b, c: (b, 0, c))
    v_spec = pl.BlockSpec((None, None, W, L), lambda b, c: (layer, b, 0, c))
    return pl.pallas_call(
        functools.partial(_fox_prep_kernel, L=L, n_heads=n_heads, g0=g0),
        out_shape=(jax.ShapeDtypeStruct((M, n_heads * AUG), BF16),
                   jax.ShapeDtypeStruct((n_seq, n_heads * AUG, S), BF16),
                   jax.ShapeDtypeStruct((n_seq, W, S), BF16)),
        grid=(n_seq, nc),
        in_specs=[pl.BlockSpec((L, LANES), lambda b, c: (b * nc + c, 0)),
                  pl.BlockSpec((None, GATE_ROWS, L), lambda b, c: (b, 0, c)),
                  pl.BlockSpec((L, W), lambda b, c: (b * nc + c, k_col0 // W)),
                  t_spec, v_spec],
        out_specs=(pl.BlockSpec((L, n_heads * AUG), lambda b, c: (b * nc + c, 0)),
                   pl.BlockSpec((None, n_heads * AUG, L), lambda b, c: (b, 0, c)),
                   t_spec),
        scratch_shapes=[pltpu.VMEM((1, LANES), F32), pltpu.VMEM((GATE_ROWS, 1), F32)],
        compiler_params=_cparams("parallel", "arbitrary"),
        name="fox_prep",
    )(gc, gr, z, qt, v_all)


def _fox_flash_kernel(qi_ref, kj_ref, ka_ref, qa_ref, vt_ref, o_ref, m_sc, l_sc, acc_sc, s_sc, p_sc,
                      *, tq, n_heads):
    step = pl.program_id(1)
    qi = qi_ref[step]
    kj = kj_ref[step]

    @pl.when(kj == 0)
    def _():
        m_sc[...] = jnp.full_like(m_sc, NEG_INF)
        l_sc[...] = jnp.zeros_like(l_sc)
        acc_sc[...] = jnp.zeros_like(acc_sc)

    hq = tq // 2

    def rescale(block_max):
        m_old = m_sc[...]
        m_new = jnp.maximum(m_old, jnp.concatenate(block_max, axis=0))
        alpha = jnp.exp(m_old - m_new)
        m_sc[...] = m_new
        return m_new, alpha

    def accumulate(alpha, sums):
        l_sc[...] = alpha * l_sc[...] + jnp.concatenate(sums, axis=0)
        for h in range(n_heads):
            hs = _head_slice(h)
            acc_sc[hs, :] = alpha[h:h + 1, :] * acc_sc[hs, :] + jnp.dot(
                vt_ref[hs, :], p_sc[h], preferred_element_type=F32)

    def update_full():
        block_max = []
        for h in range(n_heads):
            s = jnp.dot(ka_ref[:, h * AUG:(h + 1) * AUG], qa_ref[h * AUG:(h + 1) * AUG, :],
                        preferred_element_type=F32)
            s_sc[h] = s
            block_max.append(jnp.max(s, axis=0, keepdims=True))
        m_new, alpha = rescale(block_max)
        sums = []
        for h in range(n_heads):
            p = jnp.exp(s_sc[h] - m_new[h:h + 1, :])
            sums.append(jnp.sum(p, axis=0, keepdims=True))
            p_sc[h] = p.astype(BF16)
        accumulate(alpha, sums)

    def update_diagonal():
        key = lax.broadcasted_iota(jnp.int32, (hq, tq), 0)
        qry = lax.broadcasted_iota(jnp.int32, (hq, tq), 1)
        early_visible = key <= qry
        late_visible = early_visible[:, :hq]
        no_max = jnp.full((1, hq), NEG_INF, F32)
        block_max = []
        for h in range(n_heads):
            tile = slice(h * AUG, (h + 1) * AUG)
            s_early = jnp.where(early_visible, jnp.dot(ka_ref[:hq, tile], qa_ref[tile, :],
                                                       preferred_element_type=F32), NEG_INF)
            s_late = jnp.where(late_visible, jnp.dot(ka_ref[hq:, tile], qa_ref[tile, hq:],
                                                     preferred_element_type=F32), NEG_INF)
            s_sc[h, :hq, :] = s_early
            s_sc[h, hq:, hq:] = s_late
            block_max.append(jnp.maximum(
                jnp.max(s_early, axis=0, keepdims=True),
                jnp.concatenate([no_max, jnp.max(s_late, axis=0, keepdims=True)], axis=1)))
        m_new, alpha = rescale(block_max)
        sums = []
        for h in range(n_heads):
            p_early = jnp.exp(s_sc[h, :hq, :] - m_new[h:h + 1, :])
            p_late = jnp.exp(s_sc[h, hq:, hq:] - m_new[h:h + 1, hq:])
            sums.append(jnp.sum(p_early, axis=0, keepdims=True) + jnp.concatenate(
                [jnp.zeros((1, hq), F32), jnp.sum(p_late, axis=0, keepdims=True)], axis=1))
            p_sc[h, :hq, :] = p_early.astype(BF16)
            p_sc[h, hq:, :hq] = jnp.zeros((hq, hq), BF16)
            p_sc[h, hq:, hq:] = p_late.astype(BF16)
        accumulate(alpha, sums)

    @pl.when(kj < qi)
    def _():
        update_full()

    @pl.when(kj == qi)
    def _():
        update_diagonal()
        o_ref[...] = jnp.concatenate(
            [acc_sc[_head_slice(h), :] / l_sc[h:h + 1, :] for h in range(n_heads)], axis=0)


def fox_flash(ka, qa, vt, *, tq, n_heads):
    n_seq, W, S = vt.shape
    nq = S // tq
    pairs = [(i, j) for i in range(nq) for j in range(i + 1)]
    qi_tab = jnp.asarray([p[0] for p in pairs], jnp.int32)
    kj_tab = jnp.asarray([p[1] for p in pairs], jnp.int32)
    grid_spec = pltpu.PrefetchScalarGridSpec(
        num_scalar_prefetch=2,
        grid=(n_seq, len(pairs)),
        in_specs=[pl.BlockSpec((tq, n_heads * AUG), lambda b, s, qi, kj: (b * nq + kj[s], 0)),
                  pl.BlockSpec((None, n_heads * AUG, tq), lambda b, s, qi, kj: (b, 0, qi[s])),
                  pl.BlockSpec((None, W, tq), lambda b, s, qi, kj: (b, 0, kj[s]))],
        out_specs=pl.BlockSpec((None, W, tq), lambda b, s, qi, kj: (b, 0, qi[s])),
        scratch_shapes=[pltpu.VMEM((n_heads, tq), F32), pltpu.VMEM((n_heads, tq), F32),
                        pltpu.VMEM((W, tq), F32),
                        pltpu.VMEM((n_heads, tq, tq), F32), pltpu.VMEM((n_heads, tq, tq), BF16)],
    )
    return pl.pallas_call(
        functools.partial(_fox_flash_kernel, tq=tq, n_heads=n_heads),
        out_shape=jax.ShapeDtypeStruct((n_seq, W, S), F32),
        grid_spec=grid_spec,
        compiler_params=_cparams("parallel", "arbitrary"),
        name="fox_flash",
    )(qi_tab, kj_tab, ka, qa, vt)


def _fox_decode_kernel(pt_ref, q_ref, kn_ref, vn_ref, gc_ref, gr_ref, *rest, valid, n_heads, g0, n_pages):
    k_refs = rest[:n_pages]
    v_refs = rest[n_pages:2 * n_pages]
    lf_refs = rest[2 * n_pages:3 * n_pages]
    o_ref = rest[3 * n_pages]
    del pt_ref
    T = SUBLANES
    H = n_heads
    page = lf_refs[0].shape[-1]

    q = q_ref[...] * SCALE
    kn = kn_ref[...]
    vn = vn_ref[...]
    gc = gc_ref[...]
    gr = gr_ref[...]
    t_i = lax.broadcasted_iota(jnp.int32, (T, LANES), 0)
    cum_col = sum(jnp.where(t_i >= u, gc[u:u + 1, :], 0.0) for u in range(valid))
    u_i = lax.broadcasted_iota(jnp.int32, (GATE_ROWS, T), 1)
    cum_row = sum(jnp.where(u_i >= u, gr[:, u:u + 1], 0.0) for u in range(valid))
    t_q = lax.broadcasted_iota(jnp.int32, (T, T), 0)
    u_k = lax.broadcasted_iota(jnp.int32, (T, T), 1)
    new_visible = (u_k <= t_q) & (u_k < valid)

    s_a = lax.broadcasted_iota(jnp.int32, (page, page), 0)
    s_b = lax.broadcasted_iota(jnp.int32, (page, page), 1)
    later = jnp.where(s_a > s_b, 1.0, 0.0).astype(BF16)
    lf = jnp.concatenate([r[...] for r in lf_refs], axis=0)
    within = sum(jnp.dot(p, later, preferred_element_type=F32) for p in _split3(lf))
    totals = within[:, 0:1] + lf[:, 0:1]
    carry = jnp.zeros((H, 1), F32)
    suffix = [None] * n_pages
    for p in reversed(range(n_pages)):
        suffix[p] = within[p * H:(p + 1) * H, :] + carry
        carry = carry + totals[p * H:(p + 1) * H, :]

    s_past, s_new = [], []
    for h in range(H):
        qh = q[:, _head_slice(h)]
        f_t = cum_col[:, g0 + h:g0 + h + 1]
        kt = jnp.concatenate([r[h] for r in k_refs], axis=1)
        bias = jnp.concatenate([sp[h:h + 1, :] for sp in suffix], axis=1)
        s_past.append(_dot(qh, kt) + (f_t + bias))
        sn = _dot_nt(qh, kn[:, _head_slice(h)]) + (f_t - cum_row[g0 + h:g0 + h + 1, :])
        s_new.append(jnp.where(new_visible, sn, NEG_INF))
    p_past, p_new, denom = [], [], []
    for h in range(H):
        m = jnp.maximum(jnp.max(s_past[h], axis=1, keepdims=True), jnp.max(s_new[h], axis=1, keepdims=True))
        p_past.append(jnp.exp(s_past[h] - m))
        p_new.append(jnp.exp(s_new[h] - m))
        denom.append(jnp.sum(p_past[h], axis=1, keepdims=True) + jnp.sum(p_new[h], axis=1, keepdims=True))
    outs = []
    for h in range(H):
        vt = jnp.concatenate([r[h] for r in v_refs], axis=1)
        outs.append((_dot_nt(p_past[h], vt) + _dot(p_new[h], vn[:, _head_slice(h)])) / denom[h])
    o_ref[...] = jnp.concatenate(outs, axis=1)


def fox_decode(z, gc, gr, cache_kt, cache_vt, cache_lf_t, page_table, *, layer, valid, col0, n_heads, g0):
    M = z.shape[0]
    T = SUBLANES
    n_seq = M // T
    W = n_heads * HEAD_DIM
    n_pages = page_table.shape[1]
    page = cache_kt.shape[-1]
    cb = col0 // W

    def page_idx(i, nd):
        return lambda b, pt: (layer, pt[b * n_pages + i]) + (0,) * nd

    kv_specs = [pl.BlockSpec((None, None, n_heads, HEAD_DIM, page), page_idx(i, 3)) for i in range(n_pages)]
    lf_specs = [pl.BlockSpec((None, None, n_heads, page), page_idx(i, 2)) for i in range(n_pages)]
    zspec = lambda c: pl.BlockSpec((T, W), lambda b, pt: (b, c))
    grid_spec = pltpu.PrefetchScalarGridSpec(
        num_scalar_prefetch=1,
        grid=(n_seq,),
        in_specs=[zspec(cb), zspec(cb + 1), zspec(cb + 2),
                  pl.BlockSpec((T, LANES), lambda b, pt: (b, 0)),
                  pl.BlockSpec((None, GATE_ROWS, T), lambda b, pt: (b, 0, 0))] + kv_specs + kv_specs + lf_specs,
        out_specs=pl.BlockSpec((T, W), lambda b, pt: (b, 0)),
    )
    return pl.pallas_call(
        functools.partial(_fox_decode_kernel, valid=valid, n_heads=n_heads, g0=g0, n_pages=n_pages),
        out_shape=jax.ShapeDtypeStruct((M, W), F32),
        grid_spec=grid_spec,
        compiler_params=_cparams("parallel"),
        name="fox_decode",
    )(page_table.reshape(-1), z, z, z, gc, gr,
      *([cache_kt] * n_pages), *([cache_vt] * n_pages), *([cache_lf_t] * n_pages))


def _mix_cross_prompt_kernel(x_ref, r_ref, m_ref, f_ref, wr_ref, wm_ref, wf_ref,
                             g_ref, wq_ref, kt_ref, vt_ref, wo_ref, o_ref, *, n_heads):
    x = (x_ref[...] + _dot(r_ref[...], wr_ref[...]) + _dot(m_ref[...], wm_ref[...])
         + _dot_tn(f_ref[...], wf_ref[...]))
    q = jnp.dot(_rms(x, g_ref[...]).astype(BF16), wq_ref[...], preferred_element_type=F32) * SCALE
    scores = [_dot(q[:, _head_slice(h)], kt_ref[_head_slice(h), :]) for h in range(n_heads)]
    probs = [_softmax_rows(s) for s in scores]
    outs = [_dot_nt(probs[h], vt_ref[_head_slice(h), :]) for h in range(n_heads)]
    o_ref[...] = x + jnp.dot(jnp.concatenate(outs, axis=-1).astype(BF16), wo_ref[...],
                             preferred_element_type=F32)


def mix_cross_prompt(x, r, m, f_t, w_out, g, w_q, mkv_t, w_o, *, n_heads, tm):
    M, D = x.shape
    W = n_heads * HEAD_DIM
    n_seq, _, mem_len = mkv_t.shape
    per_seq = M // n_seq // tm
    wr, wm = r.shape[1], m.shape[1]
    w_r, w_m, w_f = w_out[:wr], w_out[wr:wr + wm], w_out[wr + wm:]
    row = lambda w: pl.BlockSpec((tm, w), lambda i: (i, 0))
    full = lambda a: pl.BlockSpec(a.shape, lambda i: (0, 0))
    return pl.pallas_call(
        functools.partial(_mix_cross_prompt_kernel, n_heads=n_heads),
        out_shape=jax.ShapeDtypeStruct((M, D), F32),
        grid=(M // tm,),
        in_specs=[row(D), row(wr), row(wm),
                  pl.BlockSpec((None, f_t.shape[1], tm), lambda i: (i // per_seq, 0, i % per_seq)),
                  full(w_r), full(w_m), full(w_f),
                  pl.BlockSpec((1, D), lambda i: (0, 0)),
                  full(w_q),
                  pl.BlockSpec((None, W, mem_len), lambda i: (i // per_seq, 0, 0)),
                  pl.BlockSpec((None, W, mem_len), lambda i: (i // per_seq, 1, 0)),
                  full(w_o)],
        out_specs=row(D),
        compiler_params=_cparams("parallel"),
        name="mix_cross_prompt",
    )(x, r, m, f_t, w_r, w_m, w_f, g.reshape(1, D), w_q, mkv_t, mkv_t, w_o)


def _cross_sample_kernel(x_ref, g_ref, wq_ref, kt_ref, vt_ref, wo_ref, o_ref, *, n_heads, seqs):
    T = SUBLANES
    x = x_ref[...]
    q = jnp.dot(_rms(x, g_ref[...]).astype(BF16), wq_ref[...], preferred_element_type=F32) * SCALE
    items = [(b, h) for b in range(seqs) for h in range(n_heads)]
    scores = {(b, h): _dot(q[b * T:(b + 1) * T, _head_slice(h)], kt_ref[b, h]) for b, h in items}
    probs = {i: _softmax_rows(scores[i]) for i in items}
    outs = {(b, h): _dot_nt(probs[b, h], vt_ref[b, h]) for b, h in items}
    rows = [jnp.concatenate([outs[b, h] for h in range(n_heads)], axis=1) for b in range(seqs)]
    o_ref[...] = x + jnp.dot(jnp.concatenate(rows, axis=0).astype(BF16), wo_ref[...],
                             preferred_element_type=F32)


def cross_attn_sample(x, g, w_q, cache_kt, cache_vt, w_o, *, layer, n_heads, seqs):
    M, D = x.shape
    T = SUBLANES
    W = n_heads * HEAD_DIM
    mem_len = cache_kt.shape[-1]
    kv = pl.BlockSpec((None, seqs, n_heads, HEAD_DIM, mem_len), lambda i: (layer, i, 0, 0, 0))
    return pl.pallas_call(
        functools.partial(_cross_sample_kernel, n_heads=n_heads, seqs=seqs),
        out_shape=jax.ShapeDtypeStruct((M, D), F32),
        grid=(M // (seqs * T),),
        in_specs=[pl.BlockSpec((seqs * T, D), lambda i: (i, 0)),
                  pl.BlockSpec((1, D), lambda i: (0, 0)),
                  pl.BlockSpec((D, W), lambda i: (0, 0)),
                  kv, kv,
                  pl.BlockSpec((W, D), lambda i: (0, 0))],
        out_specs=pl.BlockSpec((seqs * T, D), lambda i: (i, 0)),
        compiler_params=_cparams("parallel"),
        name="cross_attn_sample",
    )(x, g.reshape(1, D), w_q, cache_kt, cache_vt, w_o)


def _rope_tables(pos, n_heads):
    inv = ROPE_BASE ** (-jnp.arange(HALF, dtype=F32) / HALF)
    ang = pos.astype(F32)[:, None] * inv[None, :]
    cos, sin = jnp.cos(ang), jnp.sin(ang)
    return (jnp.tile(jnp.concatenate([cos, cos], axis=-1), (1, n_heads)),
            jnp.tile(jnp.concatenate([-sin, sin], axis=-1), (1, n_heads)))


def _pick(n, candidates):
    for c in candidates:
        if n % c == 0:
            return c
    return n


def kernel(x_prompt, x_sample, mem_prompt, state_ret, state_mlstm_c, state_mlstm_n, state_mlstm_m,
           cache_fox_k, cache_fox_v, cache_fox_logf, cache_mem_k, cache_mem_v, page_table,
           g_mix, w_in, b_in, g_ret, g_mlstm, w_out, g_cross, g_mem, w_cq, w_ckv, w_co,
           g_mlp, w_up, w_down, g_final):
    B, S, D = x_prompt.shape
    NB, T, _ = x_sample.shape
    depth = w_in.shape[0]
    RH = state_ret.shape[2]
    MH = state_mlstm_c.shape[2]
    FH = cache_fox_k.shape[3]
    MEMH = cache_mem_k.shape[3]
    RW, MW, FW, MEMW = RH * HEAD_DIM, MH * HEAD_DIM, FH * HEAD_DIM, MEMH * HEAD_DIM
    n_pages = page_table.shape[1]
    page = cache_fox_k.shape[2]
    mem_len = mem_prompt.shape[1]
    TP = SUBLANES
    assert T <= TP and S % CHUNK == 0 and 2 * MH + FH <= GATE_ROWS
    assert RW % LANES == 0 and MW % LANES == 0 and FW % LANES == 0

    sizes = (RW,) * 4 + (MW,) * 4 + (MH, MH) + (FW,) * 3 + (FH,)
    starts = np.concatenate([[0], np.cumsum(sizes)[:-1]])
    cols = lambda a, *ids: jnp.concatenate([a[..., starts[i]:starts[i] + sizes[i]] for i in ids], axis=-1)
    ret_col0, mls_col0 = 0, 4 * RW
    fox_col0 = mls_col0 + 4 * MW
    assert fox_col0 % FW == 0 and mls_col0 % MW == 0
    fox_g0 = 2 * MH

    def regroup(ids, gate_col0):
        ids = ids + (8, 9, 13)
        n_cols = gate_col0 + 2 * LANES
        pad = n_cols - sum(sizes[i] for i in ids)
        return (jnp.pad(cols(w_in, *ids), ((0, 0), (0, 0), (0, pad))).astype(BF16),
                jnp.pad(cols(b_in, *ids), ((0, 0), (0, pad))), n_cols)

    gate_col0_s = fox_col0 + 3 * FW
    w_in_s, b_in_s, n_cols_s = regroup((0, 1, 2, 3, 4, 5, 6, 7, 10, 11, 12), gate_col0_s)
    gate_col0_p = fox_col0 + FW
    w_in_p, b_in_p, n_cols_p = regroup((0, 1, 2, 3, 4, 5, 6, 7, 11), gate_col0_p)
    w_fox_t = jnp.swapaxes(cols(w_in, 10, 11, 12), 1, 2).astype(BF16)
    b_fox = cols(b_in, 10, 11, 12)

    w_out_b, w_cq_b, w_co_b = (w.astype(BF16) for w in (w_out, w_cq, w_co))
    w_ckv_t = jnp.swapaxes(w_ckv, 1, 2).astype(BF16)
    w_up_b, w_down_b = w_up.astype(BF16), w_down.astype(BF16)

    past = n_pages * page
    cos_p, sin_p = _rope_tables(jnp.arange(S, dtype=jnp.int32), RH)
    pos_s = jnp.minimum(past + jnp.arange(TP, dtype=jnp.int32), past + T - 1)
    cos_s, sin_s = _rope_tables(pos_s, RH)
    log_gamma = tuple(float(np.log1p(-np.exp2(-5.0 - h))) for h in range(RH))

    cache_kt = jnp.transpose(cache_fox_k, (0, 1, 3, 4, 2))
    cache_vt = jnp.transpose(cache_fox_v, (0, 1, 3, 4, 2))
    cache_lf_t = jnp.swapaxes(cache_fox_logf, 2, 3)
    mem_kt = jnp.transpose(cache_mem_k, (0, 1, 3, 4, 2))
    mem_vt = jnp.transpose(cache_mem_v, (0, 1, 3, 4, 2))

    Mp, Ms = B * S, NB * TP
    xp = x_prompt.reshape(Mp, D)
    xs = jnp.pad(x_sample, ((0, 0), (0, TP - T), (0, 0))).reshape(Ms, D)
    mem = mem_prompt.reshape(B * mem_len, D)

    tm_p = _pick(S, (1024, 512, 256, 128))
    tm_in = _pick(S, (512, 256, 128))
    tm_s = _pick(Ms, (1024, 512, 256, 128, 64, 32, 16, 8))
    tn_s = _pick(n_cols_s, (1280, 768, 640, 512, 256, 128))
    tn_p = n_cols_p
    tf = _pick(w_up.shape[2], (2048, 1024, 512, 256, 128))
    tq = _pick(S, (512, 256, 128))
    l_prep = _pick(S, (512, 256, 128))
    seqs = _pick(NB, (16, 8, 4, 2, 1))
    seqs_p = _pick(B, (8, 4, 2, 1))
    zeros = lambda *s: jnp.zeros(s, F32)

    def gates_of(z, gate_col0, n_seq, rows, tm):
        gc = gate_activations(z, col_block=gate_col0 // LANES, n_raw=MH, tm=tm)
        gr = jnp.swapaxes(gc[:, :GATE_ROWS].reshape(n_seq, rows, GATE_ROWS), 1, 2)
        return gc, gr

    outs_p = [[] for _ in range(7)]
    fk_all = zeros(depth, B, FW, S)
    fv_all = zeros(depth, B, FW, S)
    outs_s = [[] for _ in range(7)]
    for l in range(depth):
        zp, qt_p, fk_all, fv_all = in_proj_prompt(xp, g_mix[l], w_in_p[l], b_in_p[l], w_fox_t[l], b_fox[l],
                                                  fk_all, fv_all, layer=l, n_seq=B, tm=tm_in, tn=tn_p)
        gcp, grp = gates_of(zp, gate_col0_p, B, S, tm_p)
        r_p, S_p = retention(zp, cos_p, sin_p, zeros(B, RH, HEAD_DIM, HEAD_DIM), g_ret[l],
                             n_seq=B, L=CHUNK, valid=CHUNK, col0=ret_col0, log_gamma=log_gamma,
                             seqs=seqs_p)
        m_p, C_p, n_p, mm_p = mlstm(zp, gcp, grp, zeros(B, MH, HEAD_DIM, HEAD_DIM), zeros(B, MH, HEAD_DIM),
                                    zeros(B, MH), g_mlstm[l], n_seq=B, L=CHUNK, valid=CHUNK,
                                    col0=mls_col0, n_heads=MH, seqs=seqs_p)
        ka, qa, vb = fox_prep(gcp, grp, zp, qt_p, fv_all, layer=l, n_seq=B, L=l_prep, k_col0=fox_col0,
                              n_heads=FH, g0=fox_g0)
        f_p = fox_flash(ka, qa, vb, tq=tq, n_heads=FH)
        mkv_t = norm_matmul_t(mem, g_mem[l], w_ckv_t[l], n_seq=B)
        xp = mix_cross_prompt(xp, r_p, m_p, f_p, w_out_b[l], g_cross[l], w_cq_b[l], mkv_t, w_co_b[l],
                              n_heads=MEMH, tm=tm_p)
        xp = sq_relu_mlp(xp, g_mlp[l], w_up_b[l], w_down_b[l], tm=tm_p, tf=tf)

        zs = norm_matmul(xs, g_mix[l], w_in_s[l], b_in_s[l], tm=tm_s, tn=tn_s)
        gcs, grs = gates_of(zs, gate_col0_s, NB, TP, tm_s)
        r_s, S_s = retention(zs, cos_s, sin_s, state_ret[l], g_ret[l], n_seq=NB, L=TP, valid=T,
                             col0=ret_col0, log_gamma=log_gamma, seqs=seqs)
        m_s, C_s, n_s, mm_s = mlstm(zs, gcs, grs, state_mlstm_c[l], state_mlstm_n[l], state_mlstm_m[l],
                                    g_mlstm[l], n_seq=NB, L=TP, valid=T, col0=mls_col0, n_heads=MH,
                                    seqs=seqs)
        f_s = fox_decode(zs, gcs, grs, cache_kt, cache_vt, cache_lf_t, page_table, layer=l, valid=T,
                         col0=fox_col0, n_heads=FH, g0=fox_g0)
        xs = out_proj(xs, r_s, m_s, f_s, w_out_b[l], tm=tm_s)
        xs = cross_attn_sample(xs, g_cross[l], w_cq_b[l], mem_kt, mem_vt, w_co_b[l], layer=l,
                               n_heads=MEMH, seqs=seqs)
        xs = sq_relu_mlp(xs, g_mlp[l], w_up_b[l], w_down_b[l], tm=tm_s, tf=tf)

        heads_last = lambda a, n: jnp.transpose(a.reshape(a.shape[0], n, HEAD_DIM, a.shape[-1]), (0, 3, 1, 2))
        fks = lambda c: zs[:, fox_col0 + c * FW:fox_col0 + (c + 1) * FW].reshape(NB, TP, FH, HEAD_DIM)[:, :T]
        new_p = (S_p, C_p, n_p, mm_p,
                 gcp[:, fox_g0:fox_g0 + FH].reshape(B, S, FH),
                 heads_last(mkv_t[:, :MEMW], MEMH), heads_last(mkv_t[:, MEMW:], MEMH))
        new_s = (S_s, C_s, n_s, mm_s, fks(1), fks(2),
                 gcs[:, fox_g0:fox_g0 + FH].reshape(NB, TP, FH)[:, :T])
        for lst, a in zip(outs_p, new_p):
            lst.append(a)
        for lst, a in zip(outs_s, new_s):
            lst.append(a)

    y_prompt = final_norm(xp, g_final, tm=tm_p).reshape(B, S, D)
    y_sample = final_norm(xs, g_final, tm=tm_s).reshape(NB, TP, D)[:, :T]
    ret_p, c_p, n_p, m_p, flf_p, memk_p, memv_p = [jnp.stack(a, axis=0) for a in outs_p]
    stacked_heads_last = lambda a: jnp.transpose(a.reshape(depth, B, FH, HEAD_DIM, S), (0, 1, 4, 2, 3))
    fk_p, fv_p = stacked_heads_last(fk_all), stacked_heads_last(fv_all)
    ret_s, c_s, n_s, m_s, fk_s, fv_s, flf_s = [jnp.stack(a, axis=0) for a in outs_s]
    return (y_prompt, y_sample, ret_p, ret_s, c_p, c_s, n_p, n_s, m_p, m_s,
            fk_p, fk_s, fv_p, fv_s, flf_p, flf_s, memk_p, memv_p)
```
